```python
import math
import jax, jax.numpy as jnp
from jax import lax
import numpy as np

D_MODEL = 2048
BATCH = 4
SEQ = 4096
DEPTH = 2

GRID_W = 64
CTX_LEN = 256
EPS = 1e-6
CHUNK = 64
HY_W = 512
HY_ORDER = 2
HY_SHORT = 3
HY_EMB = 33
HY_FFN = 64
HY_TARGET = 1e-2
HY_FAST_PCT = 0.3
HY_SLOW_PCT = 1.5
GLA_HEADS = 4
GLA_DK = 64
GLA_DV = 128
GLA_RANK = 16
GLA_TAU = 16.0
ML_HEADS = 4
ML_D = 128
RET_HEADS = 4
RET_D = 128
ROPE_BASE = 10000.0
MOE_GROUPS = 4
MOE_EPG = 8
MOE_TOPK = 2
MOE_FF = 256

D_MIX = HY_W + GLA_HEADS * GLA_DV + ML_HEADS * ML_D + RET_HEADS * RET_D
PROJ_SPLITS = (
    ('hy', (HY_ORDER + 1) * HY_W),
    ('gla_q', GLA_HEADS * GLA_DK), ('gla_k', GLA_HEADS * GLA_DK),
    ('gla_v', GLA_HEADS * GLA_DV), ('gla_g', GLA_HEADS * GLA_DV), ('gla_r', 2 * GLA_RANK),
    ('ml_q', ML_HEADS * ML_D), ('ml_k', ML_HEADS * ML_D), ('ml_v', ML_HEADS * ML_D),
    ('ml_o', ML_HEADS * ML_D), ('ml_gates', 2 * 2 * ML_HEADS),
    ('ret_q', RET_HEADS * RET_D), ('ret_k', RET_HEADS * RET_D),
    ('ret_v', RET_HEADS * RET_D), ('ret_g', RET_HEADS * RET_D),
)
D_PROJ = sum(w for _, w in PROJ_SPLITS)

kernel_name = 'hybrid_bidir_hyena_gla_mlstm_retnet_hmoe'

F32 = jnp.float32


def rmsnorm(x, g):
    x = x.astype(F32)
    return x * lax.rsqrt(jnp.mean(x * x, axis=-1, keepdims=True) + EPS) * g


def head_rmsnorm(y, g, n_heads):
    b, l, w = y.shape
    yh = y.reshape(b, l, n_heads, w // n_heads)
    yh = yh * lax.rsqrt(jnp.mean(yh * yh, axis=-1, keepdims=True) + EPS)
    return yh.reshape(b, l, w) * g


def split_proj(p):
    out, off = {}, 0
    for name, width in PROJ_SPLITS:
        out[name] = p[..., off:off + width]
        off += width
    return out


def to_heads(a, n_heads):
    b, l, w = a.shape
    return a.reshape(b, l, n_heads, w // n_heads).transpose(0, 2, 1, 3)


def from_heads(a):
    b, h, l, d = a.shape
    return a.transpose(0, 2, 1, 3).reshape(b, l, h * d)


def short_conv(u, w, b):
    k = w.shape[0]
    y = lax.conv_general_dilated(u, w[:, None, :].astype(u.dtype), window_strides=(1,),
                                 padding=((k // 2, k // 2),), dimension_numbers=('NWC', 'WIO', 'NWC'),
                                 feature_group_count=u.shape[-1])
    return y + b


def hyena_filters(length, lp):
    pos = jnp.arange(length, dtype=F32)
    t = pos / (length - 1)
    bands = (HY_EMB - 1) // 2
    fr = jnp.linspace(1e-4, bands - 1, bands, dtype=F32)
    ang = (2.0 * math.pi / length) * pos[:, None] * fr[None, :]
    z = jnp.concatenate([t[:, None], jnp.cos(ang), -jnp.sin(ang)], axis=-1)
    hdn = jnp.sin(lp['hy_f_freq'][0] * (z @ lp['hy_f_w1'] + lp['hy_f_b1']))
    hdn = jnp.sin(lp['hy_f_freq'][1] * (hdn @ lp['hy_f_w2'] + lp['hy_f_b2']))
    h = (hdn @ lp['hy_f_w3']) * jnp.exp(-t[:, None] * jnp.abs(lp['hy_decay']))
    return h.reshape(length, HY_ORDER, 2, HY_W)


def long_conv(z, h_fwd, h_bwd):
    length, ch = z.shape[1], z.shape[2]
    h_full = jnp.concatenate([h_fwd, jnp.zeros((1, ch), F32), jnp.flip(h_bwd[1:], axis=0)], axis=0)
    zf = jnp.fft.rfft(z, n=2 * length, axis=1)
    hf = jnp.fft.rfft(h_full, n=2 * length, axis=0)
    return jnp.fft.irfft(zf * hf[None], n=2 * length, axis=1)[:, :length]


def hyena_mixer(u, lp):
    length = u.shape[1]
    u = short_conv(u.astype(F32), lp['hy_conv_w'].astype(F32), lp['hy_conv_b'])
    parts = jnp.split(u, HY_ORDER + 1, axis=-1)
    filt = hyena_filters(length, lp)
    z = parts[0]
    for o in range(HY_ORDER):
        z = parts[o + 1] * (long_conv(z, filt[:, o, 0], filt[:, o, 1]) + lp['hy_skip'][o] * z)
    return z


def gla_chunked(q, k, v, log_a, s0):
    b_, h_, length, dk = q.shape
    dv = v.shape[-1]
    n = length // CHUNK
    rs = lambda a: a.astype(F32).reshape(b_, h_, n, CHUNK, a.shape[-1])
    q, k, v, log_a = rs(q), rs(k), rs(v), rs(log_a)
    bc = jnp.cumsum(log_a, axis=3)
    b_last = bc[:, :, :, -1:, :]
    q_in = q * jnp.exp(bc)
    k_in = k * jnp.exp(-bc)
    k_st = k * jnp.exp(b_last - bc)
    causal = jnp.tril(jnp.ones((CHUNK, CHUNK), dtype=bool))
    att = jnp.where(causal, jnp.einsum('bhnid,bhnjd->bhnij', q_in, k_in), 0.0)
    intra = jnp.einsum('bhnij,bhnje->bhnie', att, v)
    u = jnp.einsum('bhnjd,bhnje->bhnde', k_st, v)
    a_chunk = jnp.exp(b_last[:, :, :, 0, :])

    def step(s, xs):
        a_i, u_i = xs
        return a_i[..., None] * s + u_i, s

    s_fin, s_start = lax.scan(step, s0.astype(F32), (jnp.moveaxis(a_chunk, 2, 0), jnp.moveaxis(u, 2, 0)))
    s_start = jnp.moveaxis(s_start, 0, 2)
    inter = jnp.einsum('bhnid,bhnde->bhnie', q_in, s_start)
    return (intra + inter).reshape(b_, h_, length, dv), s_fin


def mlstm_chunked(q, k, v, ig, lf, state):
    b_, h_, length, d = q.shape
    n = length // CHUNK
    chunks = lambda a: jnp.moveaxis(a.astype(F32).reshape(b_, h_, n, CHUNK, *a.shape[3:]), 2, 0)
    causal = jnp.tril(jnp.ones((CHUNK, CHUNK), dtype=bool))

    def step(carry, xs):
        c_mat, n_vec, m = carry
        qc, kc, vc, ic, fc = xs
        bc = jnp.cumsum(fc, axis=-1)
        dmat = jnp.where(causal, bc[..., :, None] - bc[..., None, :] + ic[..., None, :], -jnp.inf)
        inter_log = bc + m[..., None]
        m_t = jnp.maximum(inter_log, jnp.max(dmat, axis=-1))
        s = jnp.einsum('bhid,bhjd->bhij', qc, kc) * jnp.exp(dmat - m_t[..., None])
        inter = jnp.exp(inter_log - m_t)
        num = inter[..., None] * jnp.einsum('bhid,bhde->bhie', qc, c_mat) + jnp.einsum('bhij,bhje->bhie', s, vc)
        den = jnp.abs(inter * jnp.einsum('bhid,bhd->bhi', qc, n_vec) + jnp.sum(s, axis=-1))
        h = num / jnp.maximum(den, jnp.exp(-m_t))[..., None]
        b_end = bc[..., -1]
        g_log = b_end[..., None] - bc + ic
        m_new = jnp.maximum(b_end + m, jnp.max(g_log, axis=-1))
        g = jnp.exp(g_log - m_new[..., None])
        dec = jnp.exp(b_end + m - m_new)
        c_new = dec[..., None, None] * c_mat + jnp.einsum('bhj,bhjd,bhje->bhde', g, kc, vc)
        n_new = dec[..., None] * n_vec + jnp.einsum('bhj,bhjd->bhd', g, kc)
        return (c_new, n_new, m_new), h

    state, h = lax.scan(step, state, tuple(chunks(a) for a in (q, k, v, ig, lf)))
    return jnp.moveaxis(h, 0, 2).reshape(b_, h_, length, d), state


def _flip(a):
    return jnp.flip(a, axis=2)


def run_bidirectional(scan_fn, ctx_qkv, ctx_gates, lat_qkv, lat_gates, init_state):
    out_ctx, out_lat = 0.0, 0.0
    for d in range(2):
        tr = _flip if d == 1 else (lambda a: a)
        o_c, st = scan_fn(*[tr(a) for a in ctx_qkv], *[tr(a) for a in ctx_gates[d]], init_state)
        o_l, _ = scan_fn(*[tr(a) for a in lat_qkv], *[tr(a) for a in lat_gates[d]], st)
        out_ctx = out_ctx + tr(o_c)
        out_lat = out_lat + tr(o_l)
    return out_ctx, out_lat


def gla_mixer(p_ctx, p_lat, lp, with_ctx):
    def prep(p):
        q = to_heads(p['gla_q'], GLA_HEADS) * GLA_DK ** -0.5
        k = to_heads(p['gla_k'], GLA_HEADS)
        v = to_heads(p['gla_v'], GLA_HEADS)
        gates = []
        for d in range(2):
            logit = p['gla_r'][..., d * GLA_RANK:(d + 1) * GLA_RANK] @ lp['gla_wa2'][d] + lp['gla_ba'][d]
            gates.append((to_heads(jax.nn.log_sigmoid(logit) / GLA_TAU, GLA_HEADS),))
        return (q, k, v), gates

    (cq, cg), (lq, lg) = prep(p_ctx), prep(p_lat)
    init = jnp.zeros((cq[0].shape[0], GLA_HEADS, GLA_DK, GLA_DV), F32)
    o_c, o_l = run_bidirectional(gla_chunked, cq, cg, lq, lg, init)
    post = lambda o, p: head_rmsnorm(from_heads(o), lp['gla_norm_g'], GLA_HEADS) * jax.nn.silu(p['gla_g'])
    return (post(o_c, p_ctx) if with_ctx else None), post(o_l, p_lat)


def mlstm_mixer(p_ctx, p_lat, lp, with_ctx):
    def prep(p):
        b_, l_ = p['ml_q'].shape[:2]
        q = to_heads(p['ml_q'], ML_HEADS)
        k = to_heads(p['ml_k'], ML_HEADS) * ML_D ** -0.5
        v = to_heads(p['ml_v'], ML_HEADS)
        g = p['ml_gates'].reshape(b_, l_, 2, 2, ML_HEADS) + lp['ml_gate_b']
        g = g.transpose(2, 3, 0, 4, 1)
        gates = [(g[d, 0], jax.nn.log_sigmoid(g[d, 1])) for d in range(2)]
        return (q, k, v), gates

    (cq, cg), (lq, lg) = prep(p_ctx), prep(p_lat)
    b_ = cq[0].shape[0]
    init = (jnp.zeros((b_, ML_HEADS, ML_D, ML_D), F32), jnp.zeros((b_, ML_HEADS, ML_D), F32),
            jnp.zeros((b_, ML_HEADS), F32))
    o_c, o_l = run_bidirectional(mlstm_chunked, cq, cg, lq, lg, init)
    post = lambda o, p: head_rmsnorm(jax.nn.sigmoid(p['ml_o']) * from_heads(o), lp['ml_norm_g'], ML_HEADS)
    return (post(o_c, p_ctx) if with_ctx else None), post(o_l, p_lat)


def rotary_2d(a):
    length, d = a.shape[2], a.shape[3]
    rows = length // GRID_W
    row = jnp.repeat(jnp.arange(rows, dtype=F32), GRID_W)
    col = jnp.tile(jnp.arange(GRID_W, dtype=F32), rows)
    nf = d // 4
    inv = ROPE_BASE ** (-jnp.arange(nf, dtype=F32) / nf)
    ang = jnp.concatenate([row[:, None] * inv, col[:, None] * inv], axis=-1)
    cos, sin = jnp.cos(ang), jnp.sin(ang)
    a1, a2 = a[..., :d // 2], a[..., d // 2:]
    return jnp.concatenate([a1 * cos - a2 * sin, a1 * sin + a2 * cos], axis=-1)


def retention_mixer(p_ctx, p_lat, lp, with_ctx):
    def prep(p, rotate):
        q = to_heads(p['ret_q'], RET_HEADS)
        k = to_heads(p['ret_k'], RET_HEADS)
        if rotate:
            q, k = rotary_2d(q), rotary_2d(k)
        q = q * RET_D ** -0.5
        v = to_heads(p['ret_v'], RET_HEADS)
        gates = [(jnp.broadcast_to(jax.nn.log_sigmoid(lp['ret_decay'][d])[None, :, None, None], q.shape),)
                 for d in range(2)]
        return (q, k, v), gates

    (cq, cg), (lq, lg) = prep(p_ctx, False), prep(p_lat, True)
    init = jnp.zeros((cq[0].shape[0], RET_HEADS, RET_D, RET_D), F32)
    o_c, o_l = run_bidirectional(gla_chunked, cq, cg, lq, lg, init)
    post = lambda o, p: head_rmsnorm(from_heads(o), lp['ret_norm_g'], RET_HEADS) * jax.nn.silu(p['ret_g'])
    return (post(o_c, p_ctx) if with_ctx else None), post(o_l, p_lat)


def token_mixers(p_ctx, p_lat, lp, with_ctx):
    lat_parts = [hyena_mixer(p_lat['hy'], lp)]
    ctx_parts = [hyena_mixer(p_ctx['hy'], lp)] if with_ctx else []
    for mixer in (gla_mixer, mlstm_mixer, retention_mixer):
        o_ctx, o_lat = mixer(p_ctx, p_lat, lp, with_ctx)
        lat_parts.append(o_lat)
        ctx_parts.append(o_ctx)
    y_lat = jnp.concatenate(lat_parts, axis=-1)
    y_ctx = jnp.concatenate(ctx_parts, axis=-1) if with_ctx else None
    return y_ctx, y_lat


def hier_moe(h, lp):
    b_, l_, d = h.shape
    t = h.reshape(-1, d).astype(F32)
    g_prob = jax.nn.softmax(t @ lp['moe_wg'] + lp['moe_bg'], axis=-1)
    pg, gidx = lax.top_k(g_prob, 1)
    e_logits = (t @ lp['moe_we'] + lp['moe_be']).reshape(-1, MOE_GROUPS, MOE_EPG)
    sel = jnp.take_along_axis(e_logits, gidx[:, :, None], axis=1)[:, 0]
    tv, ti = lax.top_k(sel, MOE_TOPK)
    w = jax.nn.softmax(tv, axis=-1) * pg
    esel = jnp.sum(jax.nn.one_hot(ti, MOE_EPG, dtype=F32) * w[..., None], axis=1)
    combine = jax.nn.one_hot(gidx[:, 0], MOE_GROUPS, dtype=F32)[:, :, None] * esel[:, None, :]
    y = jnp.zeros_like(t)
    for g in range(MOE_GROUPS):
        a = jnp.einsum('td,edf->tef', t, lp['moe_w1'][g])
        b = jnp.einsum('td,edf->tef', t, lp['moe_w3'][g])
        act = jax.nn.silu(a) * b * combine[:, g, :, None]
        y = y + jnp.einsum('tef,efd->td', act, lp['moe_w2'][g])
    return y.reshape(b_, l_, d)


def setup_inputs(seed: int = 0) -> dict:
    key = jax.random.key(seed)
    ks = iter(jax.random.split(key, 48))
    nrm = lambda shape, scale: jax.random.normal(next(ks), shape, F32) * scale
    gain = lambda shape: 1.0 + nrm(shape, 0.05)
    n_e = MOE_GROUPS * MOE_EPG
    dmin = -math.log(HY_TARGET) / HY_SLOW_PCT
    dmax = -math.log(HY_TARGET) / HY_FAST_PCT
    base_decay = jnp.tile(jnp.linspace(dmin, dmax, HY_W, dtype=F32), HY_ORDER * 2)
    ret_base = jnp.log(2.0 ** (5.0 + jnp.arange(RET_HEADS, dtype=F32)) - 1.0)
    f_base = jnp.linspace(3.0, 6.0, ML_HEADS, dtype=F32)
    return {
        'x': nrm((BATCH, SEQ, D_MODEL), 1.0),
        'c': nrm((BATCH, D_MODEL), 1.0),
        'ctx': nrm((BATCH, CTX_LEN, D_MODEL), 1.0),
        'c_ctx': nrm((D_MODEL,), 1.0),
        'ada_w': nrm((DEPTH, D_MODEL, 6 * D_MODEL), 0.5 * D_MODEL ** -0.5),
        'ada_b': nrm((DEPTH, 6 * D_MODEL), 0.02),
        'norm1_g': gain((DEPTH, D_MODEL)),
        'norm2_g': gain((DEPTH, D_MODEL)),
        'w_in': nrm((DEPTH, D_MODEL, D_PROJ), D_MODEL ** -0.5),
        'hy_conv_w': nrm((DEPTH, HY_SHORT, (HY_ORDER + 1) * HY_W), HY_SHORT ** -0.5),
        'hy_conv_b': nrm((DEPTH, (HY_ORDER + 1) * HY_W), 0.02),
        'hy_f_w1': nrm((DEPTH, HY_EMB, HY_FFN), HY_EMB ** -0.5),
        'hy_f_b1': nrm((DEPTH, HY_FFN), 0.1),
        'hy_f_w2': nrm((DEPTH, HY_FFN, HY_FFN), HY_FFN ** -0.5),
        'hy_f_b2': nrm((DEPTH, HY_FFN), 0.1),
        'hy_f_freq': 1.0 + nrm((DEPTH, 2, HY_FFN), 0.1),
        'hy_f_w3': nrm((DEPTH, HY_FFN, HY_ORDER * 2 * HY_W), 0.01),
        'hy_decay': base_decay * (1.0 + nrm((DEPTH, HY_ORDER * 2 * HY_W), 0.05)),
        'hy_skip': nrm((DEPTH, HY_ORDER, HY_W), 0.5),
        'gla_wa2': nrm((DEPTH, 2, GLA_RANK, GLA_HEADS * GLA_DK), GLA_RANK ** -0.5),
        'gla_ba': nrm((DEPTH, 2, GLA_HEADS * GLA_DK), 0.1),
        'gla_norm_g': gain((DEPTH, GLA_HEADS * GLA_DV)),
        'ml_gate_b': jnp.concatenate([nrm((DEPTH, 2, 1, ML_HEADS), 0.1),
                                      f_base + nrm((DEPTH, 2, 1, ML_HEADS), 0.1)], axis=2),
        'ml_norm_g': gain((DEPTH, ML_HEADS * ML_D)),
        'ret_decay': ret_base + nrm((DEPTH, 2, RET_HEADS), 0.05),
        'ret_norm_g': gain((DEPTH, RET_HEADS * RET_D)),
        'w_out': nrm((DEPTH, D_MIX, D_MODEL), D_MIX ** -0.5),
        'moe_wg': nrm((DEPTH, D_MODEL, MOE_GROUPS), D_MODEL ** -0.5),
        'moe_bg': nrm((DEPTH, MOE_GROUPS), 0.01),
        'moe_we': nrm((DEPTH, D_MODEL, n_e), D_MODEL ** -0.5),
        'moe_be': nrm((DEPTH, n_e), 0.01),
        'moe_w1': nrm((DEPTH, MOE_GROUPS, MOE_EPG, D_MODEL, MOE_FF), D_MODEL ** -0.5),
        'moe_w3': nrm((DEPTH, MOE_GROUPS, MOE_EPG, D_MODEL, MOE_FF), D_MODEL ** -0.5),
        'moe_w2': nrm((DEPTH, MOE_GROUPS, MOE_EPG, MOE_FF, D_MODEL), MOE_FF ** -0.5),
        'final_g': gain((D_MODEL,)),
    }


def reference(x, c, ctx, c_ctx, ada_w, ada_b, norm1_g, norm2_g, w_in, hy_conv_w, hy_conv_b,
              hy_f_w1, hy_f_b1, hy_f_w2, hy_f_b2, hy_f_freq, hy_f_w3, hy_decay, hy_skip,
              gla_wa2, gla_ba, gla_norm_g, ml_gate_b, ml_norm_g, ret_decay, ret_norm_g, w_out,
              moe_wg, moe_bg, moe_we, moe_be, moe_w1, moe_w3, moe_w2, final_g):
    lat = x.astype(F32)
    cx = ctx.astype(F32)
    for l in range(DEPTH):
        with_ctx = l < DEPTH - 1
        lp = {
            'hy_conv_w': hy_conv_w[l], 'hy_conv_b': hy_conv_b[l], 'hy_f_w1': hy_f_w1[l], 'hy_f_b1': hy_f_b1[l],
            'hy_f_w2': hy_f_w2[l], 'hy_f_b2': hy_f_b2[l], 'hy_f_freq': hy_f_freq[l], 'hy_f_w3': hy_f_w3[l],
            'hy_decay': hy_decay[l], 'hy_skip': hy_skip[l],
            'gla_wa2': gla_wa2[l], 'gla_ba': gla_ba[l], 'gla_norm_g': gla_norm_g[l],
            'ml_gate_b': ml_gate_b[l], 'ml_norm_g': ml_norm_g[l],
            'ret_decay': ret_decay[l], 'ret_norm_g': ret_norm_g[l],
            'moe_wg': moe_wg[l], 'moe_bg': moe_bg[l], 'moe_we': moe_we[l], 'moe_be': moe_be[l],
            'moe_w1': moe_w1[l], 'moe_w3': moe_w3[l], 'moe_w2': moe_w2[l],
        }
        mod = jax.nn.silu(c.astype(F32)) @ ada_w[l] + ada_b[l]
        mod_c = jax.nn.silu(c_ctx.astype(F32)) @ ada_w[l] + ada_b[l]
        sh1, sc1, g1, sh2, sc2, g2 = [m[:, None, :] for m in jnp.split(mod, 6, axis=-1)]
        csh1, csc1, cg1, csh2, csc2, cg2 = jnp.split(mod_c, 6, axis=-1)
        p_lat = split_proj((rmsnorm(lat, norm1_g[l]) * (1.0 + sc1) + sh1) @ w_in[l])
        p_ctx = split_proj((rmsnorm(cx, norm1_g[l]) * (1.0 + csc1) + csh1) @ w_in[l])
        y_ctx, y_lat = token_mixers(p_ctx, p_lat, lp, with_ctx)
        lat = lat + g1 * (y_lat @ w_out[l])
        lat = lat + g2 * hier_moe(rmsnorm(lat, norm2_g[l]) * (1.0 + sc2) + sh2, lp)
        if with_ctx:
            cx = cx + cg1 * (y_ctx @ w_out[l])
            cx = cx + cg2 * hier_moe(rmsnorm(cx, norm2_g[l]) * (1.0 + csc2) + csh2, lp)
    return rmsnorm(lat, final_g).astype(x.dtype)
```

```python
import functools
import math

import jax
import jax.numpy as jnp
from jax import lax
from jax.experimental import pallas as pl
from jax.experimental.pallas import tpu as pltpu

F32 = jnp.float32
BF16 = jnp.bfloat16
HIGHEST = lax.Precision.HIGHEST

EPS = 1e-6
CHUNK = 64
GRID_W = 64
ROPE_BASE = 10000.0
HEADS = 4
HEAD_W = 128
HY_W = 512
HY_EMB = 33
GLA_DK = 64
GLA_RANK = 16
GLA_TAU = 16.0
MOE_GROUPS = 4
MOE_EPG = 8
MOE_FF = 256
N_EXPERTS = MOE_GROUPS * MOE_EPG

GROUP_W = 512
OFF_HY = 0
OFF_GLA_Q = 1536
OFF_GLA_K = OFF_GLA_Q + GROUP_W
OFF_GLA_V = OFF_GLA_K + GROUP_W
OFF_GLA_G = OFF_GLA_V + GROUP_W
OFF_ML_Q = OFF_GLA_G + GROUP_W
OFF_ML_K = OFF_ML_Q + GROUP_W
OFF_ML_V = OFF_ML_K + GROUP_W
OFF_ML_O = OFF_ML_V + GROUP_W
OFF_RET_Q = OFF_ML_O + GROUP_W
OFF_RET_K = OFF_RET_Q + GROUP_W
OFF_RET_V = OFF_RET_K + GROUP_W
OFF_RET_G = OFF_RET_V + GROUP_W
N_PROJ = OFF_RET_G + GROUP_W
SMALL_BASE = GLA_DK
SMALL_R = SMALL_BASE
SMALL_MLG = SMALL_BASE + 2 * GLA_RANK

VMEM_LIMIT_BYTES = 56 * 1024 * 1024


def _cparams(*sem):
    return pltpu.CompilerParams(dimension_semantics=sem, vmem_limit_bytes=VMEM_LIMIT_BYTES)


def _pick(n, candidates):
    for c in candidates:
        if n % c == 0:
            return c
    return n


def _silu(x):
    return x / (1.0 + jnp.exp(-x))


def _sigmoid(x):
    return 1.0 / (1.0 + jnp.exp(-x))


def _log_sigmoid(x):
    return jnp.minimum(x, 0.0) - jnp.log(1.0 + jnp.exp(-jnp.abs(x)))


def _dot(a, b):
    return jnp.dot(a, b, preferred_element_type=F32)


def _dot_hi(a, b):
    return jnp.dot(a, b, preferred_element_type=F32, precision=HIGHEST)


def _dot_nt(a, b):
    return lax.dot_general(a, b, (((1,), (1,)), ((), ())), preferred_element_type=F32)


def _dot_tn(a, b):
    return lax.dot_general(a, b, (((0,), (0,)), ((), ())), preferred_element_type=F32)


def _adaln_kernel(c_ref, w_ref, b_ref, o_ref):
    o_ref[0] = _dot_hi(_silu(c_ref[...]), w_ref[0]) + b_ref[0]


def adaln(cvec, ada_w, ada_b):
    depth, d, n = ada_w.shape
    tn = _pick(n, (1024, 512, 256, 128))
    return pl.pallas_call(
        _adaln_kernel,
        grid=(depth, n // tn),
        in_specs=[pl.BlockSpec((8, d), lambda l, j: (0, 0)),
                  pl.BlockSpec((1, d, tn), lambda l, j: (l, 0, j)),
                  pl.BlockSpec((1, 1, tn), lambda l, j: (l, 0, j))],
        out_specs=pl.BlockSpec((1, 8, tn), lambda l, j: (l, 0, j)),
        out_shape=jax.ShapeDtypeStruct((depth, 8, n), F32),
        compiler_params=_cparams("parallel", "parallel"),
        name="adaln",
    )(cvec, ada_w, ada_b.reshape(depth, 1, n))


def _norm_mod(x, g, sc, sh):
    ms = jnp.mean(x * x, axis=-1, keepdims=True)
    return (x * lax.rsqrt(ms + EPS) * g) * (1.0 + sc) + sh


def _inproj_kernel(x_ref, g_ref, sc_ref, sh_ref, w_ref, o_ref, xn_ref):
    @pl.when(pl.program_id(2) == 0)
    def _():
        xn_ref[...] = _norm_mod(x_ref[0], g_ref[...], sc_ref[...], sh_ref[...]).astype(BF16)

    o_ref[0] = _dot(xn_ref[...], w_ref[...])


def inproj(x, g, mod4, row_of_batch, i_sc, i_sh, w):
    b, l, d = x.shape
    n = w.shape[1]
    tm = _pick(l, (1024, 512, 256, 128, 64))
    tn = _pick(n, (768, 512, 256, 128))
    return pl.pallas_call(
        _inproj_kernel,
        grid=(b, l // tm, n // tn),
        in_specs=[pl.BlockSpec((1, tm, d), lambda bi, i, j: (bi, i, 0)),
                  pl.BlockSpec((1, d), lambda bi, i, j: (0, 0)),
                  pl.BlockSpec((None, None, 1, d), lambda bi, i, j: (row_of_batch(bi), i_sc, 0, 0)),
                  pl.BlockSpec((None, None, 1, d), lambda bi, i, j: (row_of_batch(bi), i_sh, 0, 0)),
                  pl.BlockSpec((d, tn), lambda bi, i, j: (0, j))],
        out_specs=pl.BlockSpec((1, tm, tn), lambda bi, i, j: (bi, i, j)),
        out_shape=jax.ShapeDtypeStruct((b, l, n), F32),
        scratch_shapes=[pltpu.VMEM((tm, d), BF16)],
        compiler_params=_cparams("parallel", "parallel", "arbitrary"),
        name="inproj",
    )(x, g.reshape(1, d), mod4, mod4, w)


def _causal_mask(reverse):
    r = lax.broadcasted_iota(jnp.int32, (CHUNK, CHUNK), 0)
    c = lax.broadcasted_iota(jnp.int32, (CHUNK, CHUNK), 1)
    return (c >= r) if reverse else (c <= r)


def _gla_scan_kernel(*refs, reverse, mode, rotate, accumulate, n_chunks, q_scale):
    it = iter(refs)
    q_ref, k_ref, v_ref = next(it), next(it), next(it)
    if mode == "gla":
        gsrc_ref, wa_ref, ba_ref = next(it), next(it), next(it)
    else:
        dec_ref = next(it)
    if rotate:
        cos_ref, sin_ref = next(it), next(it)
    s0_ref = next(it)
    if accumulate:
        oprev_ref = next(it)
    o_ref, sfin_ref, st_ref = next(it), next(it), next(it)

    i = pl.program_id(2)

    @pl.when(i == 0)
    def _():
        st_ref[...] = s0_ref[0, 0]

    mask = _causal_mask(reverse)
    maskf = mask.astype(F32)

    def chunk(c, carry):
        cc = (n_chunks - 1 - c) if reverse else c
        sl = pl.ds(pl.multiple_of(cc * CHUNK, CHUNK), CHUNK)
        q = q_ref[0, sl, :]
        k = k_ref[0, sl, :]
        v = v_ref[0, sl, :]
        if rotate:
            cos2 = cos_ref[sl, :]
            sin2 = sin_ref[sl, :]
            q = q * cos2 + pltpu.roll(q, HEAD_W // 2, axis=1) * sin2
            k = k * cos2 + pltpu.roll(k, HEAD_W // 2, axis=1) * sin2
        if mode == "gla":
            logit = _dot_hi(gsrc_ref[0, sl, :], wa_ref[...]) + ba_ref[...]
            la = _log_sigmoid(logit) * (1.0 / GLA_TAU)
        else:
            la = jnp.broadcast_to(_log_sigmoid(dec_ref[...]), (CHUNK, HEAD_W))
        bc = _dot_hi(maskf, la)
        tot = bc[0:1, :] if reverse else bc[CHUNK - 1:CHUNK, :]
        q_in = (q * q_scale * jnp.exp(bc)).astype(BF16)
        k_in = (k * jnp.exp(-bc)).astype(BF16)
        k_st = (k * jnp.exp(tot - bc)).astype(BF16)
        vb = v.astype(BF16)
        att = jnp.where(mask, _dot_nt(q_in, k_in), 0.0).astype(BF16)
        st = st_ref[...]
        o = _dot(att, vb) + _dot_nt(q_in, st.astype(BF16))
        if accumulate:
            o = o + oprev_ref[0, sl, :]
        o_ref[0, sl, :] = o
        st_ref[...] = st * jnp.exp(tot) + _dot_tn(vb, k_st)
        return carry

    lax.fori_loop(0, n_chunks, chunk, 0)

    @pl.when(i == pl.num_programs(2) - 1)
    def _():
        sfin_ref[0, 0] = st_ref[...]


def gla_scan(p, off_q, off_k, off_v, s0, *, reverse, mode, gate=None, rot=None, o_prev=None, q_scale):
    b, l, _ = p.shape
    tb = _pick(l, (512, 256, 128, 64))
    nblk = l // tb
    blk = (lambda i: nblk - 1 - i) if reverse else (lambda i: i)
    cq, ck, cv = off_q // HEAD_W, off_k // HEAD_W, off_v // HEAD_W

    def pspec(cb, per_head=True):
        if per_head:
            return pl.BlockSpec((1, tb, HEAD_W), lambda bi, h, i: (bi, blk(i), cb + h))
        return pl.BlockSpec((1, tb, HEAD_W), lambda bi, h, i: (bi, blk(i), cb))

    in_specs = [pspec(cq), pspec(ck), pspec(cv)]
    args = [p, p, p]
    if mode == "gla":
        wa, ba = gate
        in_specs += [pspec(OFF_GLA_K // HEAD_W, per_head=False),
                     pl.BlockSpec((HEAD_W, HEAD_W), lambda bi, h, i: (0, h)),
                     pl.BlockSpec((1, HEAD_W), lambda bi, h, i: (0, h))]
        args += [p, wa, ba]
    else:
        in_specs += [pl.BlockSpec((1, HEAD_W), lambda bi, h, i: (0, h))]
        args += [gate]
    if rot is not None:
        in_specs += [pl.BlockSpec((tb, HEAD_W), lambda bi, h, i: (blk(i), 0))] * 2
        args += [rot[0], rot[1]]
    in_specs += [pl.BlockSpec((1, 1, HEAD_W, HEAD_W), lambda bi, h, i: (bi, h, 0, 0))]
    args += [s0]
    o_spec = pl.BlockSpec((1, tb, HEAD_W), lambda bi, h, i: (bi, blk(i), h))
    aliases = {}
    if o_prev is not None:
        in_specs += [o_spec]
        aliases = {len(args): 0}
        args += [o_prev]
    kern = functools.partial(_gla_scan_kernel, reverse=reverse, mode=mode, rotate=rot is not None,
                             accumulate=o_prev is not None, n_chunks=tb // CHUNK, q_scale=q_scale)
    return pl.pallas_call(
        kern,
        grid=(b, HEADS, nblk),
        in_specs=in_specs,
        out_specs=[o_spec, pl.BlockSpec((1, 1, HEAD_W, HEAD_W), lambda bi, h, i: (bi, h, 0, 0))],
        out_shape=[jax.ShapeDtypeStruct((b, l, HEADS * HEAD_W), F32),
                   jax.ShapeDtypeStruct((b, HEADS, HEAD_W, HEAD_W), F32)],
        scratch_shapes=[pltpu.VMEM((HEAD_W, HEAD_W), F32)],
        input_output_aliases=aliases,
        compiler_params=_cparams("parallel", "parallel", "arbitrary"),
        name=f"{mode}_scan_{'bwd' if reverse else 'fwd'}",
    )(*args)


def _mlstm_kernel(*refs, reverse, accumulate, n_chunks, k_scale):
    it = iter(refs)
    q_ref, k_ref, v_ref, gsrc_ref, seli_ref, self_ref, bi_ref, bf_ref = (next(it) for _ in range(8))
    c0_ref, n0_ref, m0_ref = next(it), next(it), next(it)
    if accumulate:
        oprev_ref = next(it)
    o_ref, cfin_ref, nfin_ref, mfin_ref = next(it), next(it), next(it), next(it)
    ct_ref, n_ref, m_ref = next(it), next(it), next(it)

    i = pl.program_id(2)

    @pl.when(i == 0)
    def _():
        ct_ref[...] = c0_ref[0, 0]
        n_ref[...] = n0_ref[0, 0]
        m_ref[...] = m0_ref[0, 0]

    mask = _causal_mask(reverse)
    maskf = mask.astype(F32)
    r = lax.broadcasted_iota(jnp.int32, (CHUNK, CHUNK), 0)
    cidx = lax.broadcasted_iota(jnp.int32, (CHUNK, CHUNK), 1)
    eye = r == cidx
    ones_cc = jnp.ones((CHUNK, CHUNK), F32)

    def chunk(c, carry):
        cc = (n_chunks - 1 - c) if reverse else c
        sl = pl.ds(pl.multiple_of(cc * CHUNK, CHUNK), CHUNK)
        q = q_ref[0, sl, :]
        k = k_ref[0, sl, :] * k_scale
        v = v_ref[0, sl, :]
        gsrc = gsrc_ref[0, sl, :]
        ig = _dot_hi(gsrc, seli_ref[...]) + bi_ref[...]
        fl = _log_sigmoid(_dot_hi(gsrc, self_ref[...]) + bf_ref[...])
        bc = _dot_hi(maskf, fl)
        b_end = bc[0:1, :] if reverse else bc[CHUNK - 1:CHUNK, :]
        w = ig - bc
        w_row = _dot_hi(ones_cc, jnp.where(eye, w[:, :CHUNK], 0.0))
        m_prev = m_ref[...]
        dmat = jnp.where(mask, bc[:, :CHUNK] + w_row, -jnp.inf)
        rowmax = jnp.max(dmat, axis=-1, keepdims=True)
        inter_log = bc + m_prev
        m_t = jnp.maximum(inter_log, rowmax)
        qb = q.astype(BF16)
        kb = k.astype(BF16)
        vb = v.astype(BF16)
        s = _dot_nt(qb, kb) * jnp.exp(dmat - m_t[:, :CHUNK])
        inter = jnp.exp(inter_log - m_t)
        ct = ct_ref[...]
        nvec = n_ref[...]
        num = inter * _dot_nt(qb, ct.astype(BF16)) + _dot(s.astype(BF16), vb)
        den = jnp.abs(inter * jnp.sum(q * nvec, axis=-1, keepdims=True) + jnp.sum(s, axis=-1, keepdims=True))
        h = num / jnp.maximum(den, jnp.exp(-m_t))
        if accumulate:
            h = h + oprev_ref[0, sl, :]
        o_ref[0, sl, :] = h
        g_log = b_end + w
        m_new = jnp.maximum(b_end + m_prev, jnp.max(g_log, axis=0, keepdims=True))
        g = jnp.exp(g_log - m_new)
        dec = jnp.exp(b_end + m_prev - m_new)
        gk = g * k
        ct_ref[...] = dec * ct + _dot_tn(vb, gk.astype(BF16))
        n_ref[...] = dec * nvec + jnp.sum(gk, axis=0, keepdims=True)
        m_ref[...] = m_new
        return carry

    lax.fori_loop(0, n_chunks, chunk, 0)

    @pl.when(i == pl.num_programs(2) - 1)
    def _():
        cfin_ref[0, 0] = ct_ref[...]
        nfin_ref[0, 0] = n_ref[...]
        mfin_ref[0, 0] = m_ref[...]


def mlstm_scan(p, state, sel, *, reverse, o_prev=None):
    b, l, _ = p.shape
    tb = _pick(l, (512, 256, 128, 64))
    nblk = l // tb
    blk = (lambda i: nblk - 1 - i) if reverse else (lambda i: i)
    sel_i, sel_f, bias_i, bias_f = sel
    c0, n0, m0 = state

    def pspec(off, per_head=True):
        cb = off // HEAD_W
        if per_head:
            return pl.BlockSpec((1, tb, HEAD_W), lambda bi, h, i: (bi, blk(i), cb + h))
        return pl.BlockSpec((1, tb, HEAD_W), lambda bi, h, i: (bi, blk(i), cb))

    wspec = pl.BlockSpec((HEAD_W, HEAD_W), lambda bi, h, i: (0, h))
    rspec = pl.BlockSpec((1, HEAD_W), lambda bi, h, i: (0, h))
    cspec = pl.BlockSpec((1, 1, HEAD_W, HEAD_W), lambda bi, h, i: (bi, h, 0, 0))
    vspec = pl.BlockSpec((1, 1, 1, HEAD_W), lambda bi, h, i: (bi, h, 0, 0))
    in_specs = [pspec(OFF_ML_Q), pspec(OFF_ML_K), pspec(OFF_ML_V), pspec(OFF_GLA_K, per_head=False),
                wspec, wspec, rspec, rspec, cspec, vspec, vspec]
    args = [p, p, p, p, sel_i, sel_f, bias_i, bias_f, c0, n0, m0]
    o_spec = pl.BlockSpec((1, tb, HEAD_W), lambda bi, h, i: (bi, blk(i), h))
    aliases = {}
    if o_prev is not None:
        in_specs += [o_spec]
        aliases = {len(args): 0}
        args += [o_prev]
    kern = functools.partial(_mlstm_kernel, reverse=reverse, accumulate=o_prev is not None,
                             n_chunks=tb // CHUNK, k_scale=HEAD_W ** -0.5)
    outs = pl.pallas_call(
        kern,
        grid=(b, HEADS, nblk),
        in_specs=in_specs,
        out_specs=[o_spec, cspec, vspec, vspec],
        out_shape=[jax.ShapeDtypeStruct((b, l, HEADS * HEAD_W), F32),
                   jax.ShapeDtypeStruct((b, HEADS, HEAD_W, HEAD_W), F32),
                   jax.ShapeDtypeStruct((b, HEADS, 1, HEAD_W), F32),
                   jax.ShapeDtypeStruct((b, HEADS, 1, HEAD_W), F32)],
        scratch_shapes=[pltpu.VMEM((HEAD_W, HEAD_W), F32), pltpu.VMEM((1, HEAD_W), F32),
                        pltpu.VMEM((1, HEAD_W), F32)],
        input_output_aliases=aliases,
        compiler_params=_cparams("parallel", "parallel", "arbitrary"),
        name=f"mlstm_scan_{'bwd' if reverse else 'fwd'}",
    )(*args)
    return outs[0], (outs[1], outs[2], outs[3])


def _hy_pre_kernel(u_ref, up_ref, un_ref, w_ref, b_ref, o_ref, ob_ref, *, rows):
    i = pl.program_id(2)
    u = u_ref[0]
    prev_row = jnp.where(i == 0, 0.0, up_ref[0, 7:8, :])
    next_row = jnp.where(i == pl.num_programs(2) - 1, 0.0, un_ref[0, 0:1, :])
    ridx = lax.broadcasted_iota(jnp.int32, u.shape, 0)
    u_dn = jnp.where(ridx == 0, prev_row, pltpu.roll(u, 1, axis=0))
    u_up = jnp.where(ridx == rows - 1, next_row, pltpu.roll(u, rows - 1, axis=0))
    y = w_ref[0:1, :] * u_dn + w_ref[1:2, :] * u + w_ref[2:3, :] * u_up + b_ref[...]
    o_ref[0] = y
    ob_ref[0] = y.astype(BF16)


def hy_pre(p, conv_w, conv_b):
    b, l, _ = p.shape
    rows = _pick(l, (512, 256, 128, 64))
    nr = l // rows
    r8 = rows // 8
    n8 = l // 8
    wpad = jnp.zeros((8, 3 * HY_W), F32).at[:3].set(conv_w)
    return pl.pallas_call(
        functools.partial(_hy_pre_kernel, rows=rows),
        grid=(b, 3, nr),
        in_specs=[pl.BlockSpec((1, rows, HY_W), lambda bi, j, i: (bi, i, j)),
                  pl.BlockSpec((1, 8, HY_W), lambda bi, j, i: (bi, jnp.maximum(i * r8 - 1, 0), j)),
                  pl.BlockSpec((1, 8, HY_W), lambda bi, j, i: (bi, jnp.minimum((i + 1) * r8, n8 - 1), j)),
                  pl.BlockSpec((8, HY_W), lambda bi, j, i: (0, j)),
                  pl.BlockSpec((1, HY_W), lambda bi, j, i: (0, j))],
        out_specs=[pl.BlockSpec((1, rows, HY_W), lambda bi, j, i: (bi, i, j))] * 2,
        out_shape=[jax.ShapeDtypeStruct((b, l, 3 * HY_W), F32),
                   jax.ShapeDtypeStruct((b, l, 3 * HY_W), BF16)],
        compiler_params=_cparams("parallel", "parallel", "parallel"),
        name="hy_shortconv",
    )(p, p, p, wpad, conv_b.reshape(1, 3 * HY_W))


def _mm_kernel(a_ref, b_ref, o_ref):
    o_ref[...] = _dot(a_ref[...], b_ref[...])


def matmul_bf16(a, bm):
    m, k = a.shape
    n = bm.shape[1]
    tm = _pick(m, (512, 256, 128, 64))
    tn = _pick(n, (512, 256, 128))
    return pl.pallas_call(
        _mm_kernel,
        grid=(n // tn, m // tm),
        in_specs=[pl.BlockSpec((tm, k), lambda j, i: (i, 0)),
                  pl.BlockSpec((k, tn), lambda j, i: (0, j))],
        out_specs=pl.BlockSpec((tm, tn), lambda j, i: (i, j)),
        out_shape=jax.ShapeDtypeStruct((m, n), F32),
        compiler_params=_cparams("parallel", "parallel"),
        name="matmul_bf16",
    )(a, bm)


def _dft_fwd_kernel(c_ref, s_ref, z_ref, hr_ref, hi_ref, yr_ref, yi_ref):
    z = z_ref[0]
    cz = _dot(c_ref[...], z)
    sz = _dot(s_ref[...], z)
    hr = hr_ref[...]
    hi = hi_ref[...]
    yr_ref[0] = (cz * hr + sz * hi).astype(BF16)
    yi_ref[0] = (cz * hi - sz * hr).astype(BF16)


def dft_fwd(cmat, smat, zb, z_col, hr, hi, h_col):
    b, l, _ = zb.shape
    f = cmat.shape[0]
    tf = _pick(f, (512, 256, 128, 64))
    return pl.pallas_call(
        _dft_fwd_kernel,
        grid=(f // tf, b),
        in_specs=[pl.BlockSpec((tf, l), lambda i, bi: (i, 0)),
                  pl.BlockSpec((tf, l), lambda i, bi: (i, 0)),
                  pl.BlockSpec((1, l, HY_W), lambda i, bi: (bi, 0, z_col)),
                  pl.BlockSpec((tf, HY_W), lambda i, bi: (i, h_col)),
                  pl.BlockSpec((tf, HY_W), lambda i, bi: (i, h_col))],
        out_specs=[pl.BlockSpec((1, tf, HY_W), lambda i, bi: (bi, i, 0))] * 2,
        out_shape=[jax.ShapeDtypeStruct((b, f, HY_W), BF16)] * 2,
        compiler_params=_cparams("parallel", "parallel"),
        name="hy_dft_fwd",
    )(cmat, smat, zb, hr, hi)


def _dft_inv_kernel(ct_ref, st_ref, yr_ref, yi_ref, x_ref, zp_ref, skip_ref, o_ref, ob_ref):
    conv = _dot(ct_ref[...], yr_ref[0]) - _dot(st_ref[...], yi_ref[0])
    z = x_ref[0] * (conv + skip_ref[...] * zp_ref[0])
    o_ref[0] = z
    ob_ref[0] = z.astype(BF16)


def dft_inv(ctmat, stmat, yr, yi, x_arr, x_col, zp_arr, zp_col, skip):
    b, f, _ = yr.shape
    l = ctmat.shape[0]
    tt = _pick(l, (512, 256, 128, 64))
    return pl.pallas_call(
        _dft_inv_kernel,
        grid=(l // tt, b),
        in_specs=[pl.BlockSpec((tt, f), lambda i, bi: (i, 0)),
                  pl.BlockSpec((tt, f), lambda i, bi: (i, 0)),
                  pl.BlockSpec((1, f, HY_W), lambda i, bi: (bi, 0, 0)),
                  pl.BlockSpec((1, f, HY_W), lambda i, bi: (bi, 0, 0)),
                  pl.BlockSpec((1, tt, HY_W), lambda i, bi: (bi, i, x_col)),
                  pl.BlockSpec((1, tt, HY_W), lambda i, bi: (bi, i, zp_col)),
                  pl.BlockSpec((1, HY_W), lambda i, bi: (0, 0))],
        out_specs=[pl.BlockSpec((1, tt, HY_W), lambda i, bi: (bi, i, 0))] * 2,
        out_shape=[jax.ShapeDtypeStruct((b, l, HY_W), F32),
                   jax.ShapeDtypeStruct((b, l, HY_W), BF16)],
        compiler_params=_cparams("parallel", "parallel"),
        name="hy_dft_inv",
    )(ctmat, stmat, yr, yi, x_arr, zp_arr, skip.reshape(1, HY_W))


def dft_tables(l):
    r = _pick(l, (64, 32, 16, 8))
    period = 4 * l
    f = jnp.arange(l, dtype=jnp.int32)
    odd = 2 * f + 1
    s1 = jnp.arange(l // r, dtype=jnp.int32) * r
    s0 = jnp.arange(r, dtype=jnp.int32)
    ang = lambda ph: ph.astype(F32) * (2.0 * math.pi / period)
    a = ang((odd[:, None] * s1[None, :]) % period)
    bb = ang((odd[:, None] * s0[None, :]) % period)
    ca, sa, cb, sb = jnp.cos(a), jnp.sin(a), jnp.cos(bb), jnp.sin(bb)
    cmat = (ca[:, :, None] * cb[:, None, :] - sa[:, :, None] * sb[:, None, :]).reshape(l, l).astype(BF16)
    smat = (sa[:, :, None] * cb[:, None, :] + ca[:, :, None] * sb[:, None, :]).reshape(l, l).astype(BF16)
    cat, sat, cbt, sbt = ca.T, sa.T, cb.T, sb.T
    ctmat = (cat[:, None, :] * cbt[None, :, :] - sat[:, None, :] * sbt[None, :, :]).reshape(l, l).astype(BF16)
    stmat = (sat[:, None, :] * cbt[None, :, :] + cat[:, None, :] * sbt[None, :, :]).reshape(l, l).astype(BF16)
    return cmat, smat, ctmat, stmat


def hyena_filter_taps(length, lp):
    pos = jnp.arange(length, dtype=F32)
    t = pos / (length - 1)
    bands = (HY_EMB - 1) // 2
    fr = jnp.linspace(1e-4, bands - 1, bands, dtype=F32)
    ang = (2.0 * math.pi / length) * pos[:, None] * fr[None, :]
    z = jnp.concatenate([t[:, None], jnp.cos(ang), -jnp.sin(ang)], axis=-1)
    mm = functools.partial(jnp.matmul, precision=HIGHEST)
    hdn = jnp.sin(lp['hy_f_freq'][0] * (mm(z, lp['hy_f_w1']) + lp['hy_f_b1']))
    hdn = jnp.sin(lp['hy_f_freq'][1] * (mm(hdn, lp['hy_f_w2']) + lp['hy_f_b2']))
    return mm(hdn, lp['hy_f_w3']) * jnp.exp(-t[:, None] * jnp.abs(lp['hy_decay']))


def hyena_mixer(p, lp, tables):
    b, l, _ = p.shape
    cmat, smat, ctmat, stmat = tables
    uc, ucb = hy_pre(p, lp['hy_conv_w'], lp['hy_conv_b'])
    taps = hyena_filter_taps(l, lp).reshape(l, 2, 2, HY_W)
    hf = taps[:, :, 0, :]
    hb = taps[:, :, 1, :].at[0].set(0.0)
    hsum = (hf + hb).reshape(l, 2 * HY_W).astype(BF16)
    hdiff = (hf - hb).reshape(l, 2 * HY_W).astype(BF16)
    hr = matmul_bf16(cmat, hsum) * (1.0 / l)
    hi = matmul_bf16(smat, hdiff) * (-1.0 / l)
    yr, yi = dft_fwd(cmat, smat, ucb, 0, hr, hi, 0)
    z1, z1b = dft_inv(ctmat, stmat, yr, yi, uc, 1, uc, 0, lp['hy_skip'][0])
    yr, yi = dft_fwd(cmat, smat, z1b, 0, hr, hi, 1)
    z2, _ = dft_inv(ctmat, stmat, yr, yi, uc, 2, z1, 0, lp['hy_skip'][1])
    return z2


def _head_rms(y):
    parts = []
    for h in range(HEADS):
        yh = y[:, h * HEAD_W:(h + 1) * HEAD_W]
        parts.append(yh * lax.rsqrt(jnp.mean(yh * yh, axis=-1, keepdims=True) + EPS))
    return jnp.concatenate(parts, axis=-1)


def _outproj_kernel(hy_ref, gla_ref, glag_ref, ml_ref, mlo_ref, ret_ref, retg_ref,
                    gn_ref, w_ref, res_ref, gate_ref, o_ref):
    gn = gn_ref[...]
    y_gla = _head_rms(gla_ref[0]) * gn[0:1, :] * _silu(glag_ref[0])
    y_ml = _head_rms(_sigmoid(mlo_ref[0]) * ml_ref[0]) * gn[1:2, :]
    y_ret = _head_rms(ret_ref[0]) * gn[2:3, :] * _silu(retg_ref[0])
    acc = _dot(hy_ref[0].astype(BF16), w_ref[0:GROUP_W, :])
    acc += _dot(y_gla.astype(BF16), w_ref[GROUP_W:2 * GROUP_W, :])
    acc += _dot(y_ml.astype(BF16), w_ref[2 * GROUP_W:3 * GROUP_W, :])
    acc += _dot(y_ret.astype(BF16), w_ref[3 * GROUP_W:4 * GROUP_W, :])
    o_ref[0] = res_ref[0] + gate_ref[...] * acc


def outproj(hy, o_gla, o_ml, o_ret, p, gains, w_out, res, mod4, row_of_batch, i_gate):
    b, l, d = res.shape
    tm = _pick(l, (256, 128, 64))
    gw = GROUP_W
    ospec = pl.BlockSpec((1, tm, gw), lambda bi, i: (bi, i, 0))
    pspec = lambda off: pl.BlockSpec((1, tm, gw), lambda bi, i: (bi, i, off // gw))
    return pl.pallas_call(
        _outproj_kernel,
        grid=(b, l // tm),
        in_specs=[ospec, ospec, pspec(OFF_GLA_G), ospec, pspec(OFF_ML_O), ospec, pspec(OFF_RET_G),
                  pl.BlockSpec((8, gw), lambda bi, i: (0, 0)),
                  pl.BlockSpec((4 * gw, d), lambda bi, i: (0, 0)),
                  pl.BlockSpec((1, tm, d), lambda bi, i: (bi, i, 0)),
                  pl.BlockSpec((None, None, 1, d), lambda bi, i: (row_of_batch(bi), i_gate, 0, 0))],
        out_specs=pl.BlockSpec((1, tm, d), lambda bi, i: (bi, i, 0)),
        out_shape=jax.ShapeDtypeStruct((b, l, d), F32),
        compiler_params=_cparams("parallel", "parallel"),
        name="outproj",
    )(hy, o_gla, p, o_ml, p, o_ret, p, gains, w_out, res, mod4)


def _moe_pre_kernel(x_ref, g_ref, sc_ref, sh_ref, wr_ref, br_ref, t_ref, lg_ref):
    t = _norm_mod(x_ref[0], g_ref[...], sc_ref[...], sh_ref[...])
    t_ref[0] = t.astype(BF16)
    lg_ref[0] = _dot_hi(t, wr_ref[...]) + br_ref[...]


def moe_pre(x, g, mod4, row_of_batch, i_sc, i_sh, w_router, b_router):
    b, l, d = x.shape
    tm = _pick(l, (512, 256, 128, 64))
    nr = w_router.shape[1]
    return pl.pallas_call(
        _moe_pre_kernel,
        grid=(b, l // tm),
        in_specs=[pl.BlockSpec((1, tm, d), lambda bi, i: (bi, i, 0)),
                  pl.BlockSpec((1, d), lambda bi, i: (0, 0)),
                  pl.BlockSpec((None, None, 1, d), lambda bi, i: (row_of_batch(bi), i_sc, 0, 0)),
                  pl.BlockSpec((None, None, 1, d), lambda bi, i: (row_of_batch(bi), i_sh, 0, 0)),
                  pl.BlockSpec((d, nr), lambda bi, i: (0, 0)),
                  pl.BlockSpec((1, nr), lambda bi, i: (0, 0))],
        out_specs=[pl.BlockSpec((1, tm, d), lambda bi, i: (bi, i, 0)),
                   pl.BlockSpec((1, tm, nr), lambda bi, i: (bi, i, 0))],
        out_shape=[jax.ShapeDtypeStruct((b, l, d), BF16), jax.ShapeDtypeStruct((b, l, nr), F32)],
        compiler_params=_cparams("parallel", "parallel"),
        name="moe_pre",
    )(x, g.reshape(1, d), mod4, mod4, w_router, b_router)


def _moe_expert_kernel(te_ref, nu_ref, x_ref, w13_ref, w2_ref, rw_ref, o_ref):
    @pl.when(pl.program_id(0) < nu_ref[0])
    def _():
        h = _dot(x_ref[...], w13_ref[0])
        act = _silu(h[:, :MOE_FF]) * h[:, MOE_FF:] * rw_ref[...]
        o_ref[...] = _dot(act.astype(BF16), w2_ref[0])


def moe_experts(x_sorted, w13, w2, row_w, tile_expert, n_used, tm):
    pmax, d = x_sorted.shape
    grid_spec = pltpu.PrefetchScalarGridSpec(
        num_scalar_prefetch=2,
        grid=(pmax // tm,),
        in_specs=[pl.BlockSpec((tm, d), lambda t, te, nu: (t, 0)),
                  pl.BlockSpec((1, d, 2 * MOE_FF), lambda t, te, nu: (te[t], 0, 0)),
                  pl.BlockSpec((1, MOE_FF, d), lambda t, te, nu: (te[t], 0, 0)),
                  pl.BlockSpec((tm, 1), lambda t, te, nu: (t, 0))],
        out_specs=pl.BlockSpec((tm, d), lambda t, te, nu: (t, 0)),
    )
    return pl.pallas_call(
        _moe_expert_kernel,
        grid_spec=grid_spec,
        out_shape=jax.ShapeDtypeStruct((pmax, d), F32),
        compiler_params=_cparams("arbitrary"),
        name="moe_experts",
    )(tile_expert, n_used, x_sorted, w13, w2, row_w)


def _moe_combine_kernel(res_ref, y1_ref, y2_ref, gate_ref, o_ref):
    o_ref[0] = res_ref[0] + gate_ref[...] * (y1_ref[0] + y2_ref[0])


def moe_combine(res, y1, y2, mod4, row_of_batch, i_gate):
    b, l, d = res.shape
    tm = _pick(l, (512, 256, 128, 64))
    spec = pl.BlockSpec((1, tm, d), lambda bi, i: (bi, i, 0))
    return pl.pallas_call(
        _moe_combine_kernel,
        grid=(b, l // tm),
        in_specs=[spec, spec, spec,
                  pl.BlockSpec((None, None, 1, d), lambda bi, i: (row_of_batch(bi), i_gate, 0, 0))],
        out_specs=spec,
        out_shape=jax.ShapeDtypeStruct((b, l, d), F32),
        compiler_params=_cparams("parallel", "parallel"),
        name="moe_combine",
    )(res, y1, y2, mod4)


def moe_route(logits, tm):
    t = logits.shape[0]
    g_prob = jax.nn.softmax(logits[:, :MOE_GROUPS], axis=-1)
    gidx = jnp.argmax(g_prob, axis=-1)
    pg = jnp.max(g_prob, axis=-1)
    e_logits = logits[:, MOE_GROUPS:MOE_GROUPS + N_EXPERTS].reshape(t, MOE_GROUPS, MOE_EPG)
    sel = jnp.take_along_axis(e_logits, gidx[:, None, None], axis=1)[:, 0]
    tv, ti = lax.top_k(sel, 2)
    w = jax.nn.softmax(tv, axis=-1) * pg[:, None]
    eid = (gidx[:, None] * MOE_EPG + ti).astype(jnp.int32).reshape(-1)
    wflat = w.reshape(-1)
    order = jnp.argsort(eid, stable=True)
    counts = jnp.sum(eid[:, None] == jnp.arange(N_EXPERTS)[None, :], axis=0).astype(jnp.int32)
    padded = ((counts + tm - 1) // tm) * tm
    pend = jnp.cumsum(padded)
    pstart = pend - padded
    cstart = jnp.cumsum(counts) - counts
    pmax = 2 * t + N_EXPERTS * tm
    n_tiles = pmax // tm
    rows = jnp.arange(pmax, dtype=jnp.int32)
    row_e = jnp.minimum(jnp.searchsorted(pend, rows, side='right'), N_EXPERTS - 1).astype(jnp.int32)
    j = rows - pstart[row_e]
    valid = j < counts[row_e]
    src = jnp.clip(cstart[row_e] + j, 0, 2 * t - 1)
    pair = order[src]
    row_token = jnp.where(valid, pair // 2, 0).astype(jnp.int32)
    row_w = jnp.where(valid, wflat[pair], 0.0)
    inv = jnp.zeros((2 * t,), jnp.int32).at[order].set(jnp.arange(2 * t, dtype=jnp.int32))
    pos = (pstart[eid] + inv - cstart[eid]).reshape(t, 2)
    tile_expert = row_e[::tm]
    n_used = (pend[-1] // tm).astype(jnp.int32).reshape(1)
    return row_token, row_w.reshape(pmax, 1), pos, tile_expert, n_used, n_tiles


def hier_moe(x, lp, mod4, row_of_batch):
    b, l, d = x.shape
    t = b * l
    tm = 256 if t >= 4096 else 64
    tok, logits = moe_pre(x, lp['norm2_g'], mod4, row_of_batch, 4, 3, lp['w_router'], lp['b_router'])
    tok = tok.reshape(t, d)
    row_token, row_w, pos, tile_expert, n_used, _ = moe_route(logits.reshape(t, -1), tm)
    x_sorted = jnp.take(tok, row_token, axis=0)
    y_sorted = moe_experts(x_sorted, lp['w13'], lp['w2'], row_w, tile_expert, n_used, tm)
    y1 = jnp.take(y_sorted, pos[:, 0], axis=0).reshape(b, l, d)
    y2 = jnp.take(y_sorted, pos[:, 1], axis=0).reshape(b, l, d)
    return moe_combine(x, y1, y2, mod4, row_of_batch, 5)


def _final_norm_kernel(x_ref, g_ref, o_ref):
    x = x_ref[0]
    o_ref[0] = x * lax.rsqrt(jnp.mean(x * x, axis=-1, keepdims=True) + EPS) * g_ref[...]


def final_norm(x, g):
    b, l, d = x.shape
    tm = _pick(l, (512, 256, 128, 64))
    spec = pl.BlockSpec((1, tm, d), lambda bi, i: (bi, i, 0))
    return pl.pallas_call(
        _final_norm_kernel,
        grid=(b, l // tm),
        in_specs=[spec, pl.BlockSpec((1, d), lambda bi, i: (0, 0))],
        out_specs=spec,
        out_shape=jax.ShapeDtypeStruct((b, l, d), F32),
        compiler_params=_cparams("parallel", "parallel"),
        name="final_norm",
    )(x, g.reshape(1, d))


def _pad_heads(w, dk):
    d = w.shape[0]
    return jnp.pad(w.reshape(d, HEADS, dk), ((0, 0), (0, 0), (0, HEAD_W - dk))).reshape(d, HEADS * HEAD_W)


def prep_w_in(w_in):
    o = 0
    take = lambda n: (w_in[:, o:o + n], o + n)
    hy, o = take(3 * HY_W)
    gq, o = take(HEADS * GLA_DK)
    gk, o = take(HEADS * GLA_DK)
    gv, o = take(GROUP_W)
    gg, o = take(GROUP_W)
    gr, o = take(2 * GLA_RANK)
    ml, o = take(4 * GROUP_W)
    mg, o = take(16)
    ret, o = take(4 * GROUP_W)
    gk_p = _pad_heads(gk, GLA_DK)
    gk_p = gk_p.at[:, SMALL_R:SMALL_R + 2 * GLA_RANK].set(gr).at[:, SMALL_MLG:SMALL_MLG + 16].set(mg)
    return jnp.concatenate([hy, _pad_heads(gq, GLA_DK), gk_p, gv, gg, ml, ret], axis=1).astype(BF16)


def prep_layer(l, prm):
    lp = {k: v[l] for k, v in prm.items()}
    lp['w_in_p'] = prep_w_in(lp['w_in'])
    lp['w_out_b'] = lp['w_out'].astype(BF16)
    wa, ba = [], []
    for d in range(2):
        w = jnp.zeros((HEAD_W, HEADS * HEAD_W), F32)
        w = w.at[SMALL_R + d * GLA_RANK:SMALL_R + (d + 1) * GLA_RANK].set(_pad_heads(lp['gla_wa2'][d], GLA_DK))
        wa.append(w)
        ba.append(_pad_heads(lp['gla_ba'][d][None, :], GLA_DK))
    lp['gla_gate'] = list(zip(wa, ba))
    sel = []
    lane = jnp.arange(HEAD_W)[:, None]
    for d in range(2):
        pick = lambda gsel: jnp.concatenate(
            [jnp.broadcast_to((lane == SMALL_MLG + d * 8 + gsel * HEADS + h).astype(F32), (HEAD_W, HEAD_W))
             for h in range(HEADS)], axis=1)
        bias = lambda gsel: jnp.repeat(lp['ml_gate_b'][d, gsel], HEAD_W)[None, :]
        sel.append((pick(0), pick(1), bias(0), bias(1)))
    lp['ml_sel'] = sel
    lp['ret_dec'] = [jnp.repeat(lp['ret_decay'][d], HEAD_W)[None, :] for d in range(2)]
    gains = jnp.zeros((8, GROUP_W), F32)
    lp['gains'] = gains.at[0].set(lp['gla_norm_g']).at[1].set(lp['ml_norm_g']).at[2].set(lp['ret_norm_g'])
    d_model = lp['moe_wg'].shape[0]
    wr = jnp.zeros((d_model, HEAD_W), F32)
    lp['w_router'] = wr.at[:, :MOE_GROUPS].set(lp['moe_wg']).at[:, MOE_GROUPS:MOE_GROUPS + N_EXPERTS].set(lp['moe_we'])
    br = jnp.zeros((1, HEAD_W), F32)
    lp['b_router'] = br.at[0, :MOE_GROUPS].set(lp['moe_bg']).at[0, MOE_GROUPS:MOE_GROUPS + N_EXPERTS].set(lp['moe_be'])
    w1 = lp['moe_w1'].reshape(N_EXPERTS, d_model, MOE_FF)
    w3 = lp['moe_w3'].reshape(N_EXPERTS, d_model, MOE_FF)
    lp['w13'] = jnp.concatenate([w1, w3], axis=-1).astype(BF16)
    lp['w2'] = lp['moe_w2'].reshape(N_EXPERTS, MOE_FF, d_model).astype(BF16)
    return lp


def rotary_tables(length):
    rows = length // GRID_W
    row = jnp.repeat(jnp.arange(rows, dtype=F32), GRID_W)
    col = jnp.tile(jnp.arange(GRID_W, dtype=F32), rows)
    nf = HEAD_W // 4
    inv = ROPE_BASE ** (-jnp.arange(nf, dtype=F32) / nf)
    ang = jnp.concatenate([row[:, None] * inv, col[:, None] * inv], axis=-1)
    cos, sin = jnp.cos(ang), jnp.sin(ang)
    return jnp.concatenate([cos, cos], axis=-1), jnp.concatenate([-sin, sin], axis=-1)


def bidir_scans(p_ctx, p_lat, lp, rot):
    b = p_lat.shape[0]
    zeros_s = jnp.zeros((b, HEADS, HEAD_W, HEAD_W), F32)
    zeros_v = jnp.zeros((b, HEADS, 1, HEAD_W), F32)
    out = {}
    for name, mode, offs, scale in (("gla", "gla", (OFF_GLA_Q, OFF_GLA_K, OFF_GLA_V), GLA_DK ** -0.5),
                                    ("ret", "ret", (OFF_RET_Q, OFF_RET_K, OFF_RET_V), HEAD_W ** -0.5)):
        o_c, o_l = None, None
        for d in range(2):
            gate = lp['gla_gate'][d] if mode == "gla" else lp['ret_dec'][d]
            o_c, st = gla_scan(p_ctx, *offs, zeros_s, reverse=d == 1, mode=mode, gate=gate,
                               o_prev=o_c, q_scale=scale)
            o_l, _ = gla_scan(p_lat, *offs, st, reverse=d == 1, mode=mode, gate=gate,
                              rot=rot if mode == "ret" else None, o_prev=o_l, q_scale=scale)
        out[name] = (o_c, o_l)
    o_c, o_l = None, None
    for d in range(2):
        o_c, st = mlstm_scan(p_ctx, (zeros_s, zeros_v, zeros_v), lp['ml_sel'][d], reverse=d == 1, o_prev=o_c)
        o_l, _ = mlstm_scan(p_lat, st, lp['ml_sel'][d], reverse=d == 1, o_prev=o_l)
    out["ml"] = (o_c, o_l)
    return out


def kernel(x, c, ctx, c_ctx, ada_w, ada_b, norm1_g, norm2_g, w_in, hy_conv_w, hy_conv_b, hy_f_w1, hy_f_b1, hy_f_w2, hy_f_b2, hy_f_freq, hy_f_w3, hy_decay, hy_skip, gla_wa2, gla_ba, gla_norm_g, ml_gate_b, ml_norm_g, ret_decay, ret_norm_g, w_out, moe_wg, moe_bg, moe_we, moe_be, moe_w1, moe_w3, moe_w2, final_g):
    prm = dict(norm1_g=norm1_g, norm2_g=norm2_g, w_in=w_in, hy_conv_w=hy_conv_w, hy_conv_b=hy_conv_b,
               hy_f_w1=hy_f_w1, hy_f_b1=hy_f_b1, hy_f_w2=hy_f_w2, hy_f_b2=hy_f_b2, hy_f_freq=hy_f_freq,
               hy_f_w3=hy_f_w3, hy_decay=hy_decay, hy_skip=hy_skip, gla_wa2=gla_wa2, gla_ba=gla_ba,
               gla_norm_g=gla_norm_g, ml_gate_b=ml_gate_b, ml_norm_g=ml_norm_g, ret_decay=ret_decay,
               ret_norm_g=ret_norm_g, w_out=w_out, moe_wg=moe_wg, moe_bg=moe_bg, moe_we=moe_we,
               moe_be=moe_be, moe_w1=moe_w1, moe_w3=moe_w3, moe_w2=moe_w2)
    depth = ada_w.shape[0]
    b, seq, d = x.shape
    lc = ctx.shape[1]
    lat = x.astype(F32)
    cx = ctx.astype(F32)
    cvec = jnp.zeros((8, d), F32).at[:b].set(c.astype(F32)).at[b].set(c_ctx.astype(F32))
    mod_all = adaln(cvec, ada_w, ada_b)
    lat_row = lambda bi: bi
    ctx_row = lambda bi: b
    rot = rotary_tables(seq)
    tab_lat = dft_tables(seq)
    tab_ctx = dft_tables(lc)
    for l in range(depth):
        with_ctx = l < depth - 1
        lp = prep_layer(l, prm)
        mod4 = mod_all[l].reshape(8, 6, 1, d)
        p_lat = inproj(lat, lp['norm1_g'], mod4, lat_row, 1, 0, lp['w_in_p'])
        p_ctx = inproj(cx, lp['norm1_g'], mod4, ctx_row, 1, 0, lp['w_in_p'])
        scans = bidir_scans(p_ctx, p_lat, lp, rot)
        hy_lat = hyena_mixer(p_lat, lp, tab_lat)
        lat = outproj(hy_lat, scans["gla"][1], scans["ml"][1], scans["ret"][1], p_lat, lp['gains'],
                      lp['w_out_b'], lat, mod4, lat_row, 2)
        lat = hier_moe(lat, lp, mod4, lat_row)
        if with_ctx:
            hy_ctx = hyena_mixer(p_ctx, lp, tab_ctx)
            cx = outproj(hy_ctx, scans["gla"][0], scans["ml"][0], scans["ret"][0], p_ctx, lp['gains'],
                         lp['w_out_b'], cx, mod4, ctx_row, 2)
            cx = hier_moe(cx, lp, mod4, ctx_row)
    return final_norm(lat, final_g).astype(x.dtype)
```

```python
import functools
import math

import jax
import jax.numpy as jnp
from jax import lax
from jax.experimental import pallas as pl
from jax.experimental.pallas import tpu as pltpu

F32 = jnp.float32
BF16 = jnp.bfloat16
HIGHEST = lax.Precision.HIGHEST

EPS = 1e-6
CHUNK = 64
SCAN_BLOCK = 256
GRID_W = 64
ROPE_BASE = 10000.0
HEADS = 4
HEAD_W = 128
HY_W = 512
HY_EMB = 33
GLA_DK = 64
GLA_RANK = 16
GLA_TAU = 16.0
MOE_GROUPS = 4
MOE_EPG = 8
MOE_FF = 256
N_EXPERTS = MOE_GROUPS * MOE_EPG

GROUP_W = 512
OFF_HY = 0
OFF_GLA_Q = 1536
OFF_GLA_K = OFF_GLA_Q + GROUP_W
OFF_GLA_V = OFF_GLA_K + GROUP_W
OFF_GLA_G = OFF_GLA_V + GROUP_W
OFF_ML_Q = OFF_GLA_G + GROUP_W
OFF_ML_K = OFF_ML_Q + GROUP_W
OFF_ML_V = OFF_ML_K + GROUP_W
OFF_ML_O = OFF_ML_V + GROUP_W
OFF_RET_Q = OFF_ML_O + GROUP_W
OFF_RET_K = OFF_RET_Q + GROUP_W
OFF_RET_V = OFF_RET_K + GROUP_W
OFF_RET_G = OFF_RET_V + GROUP_W
N_PROJ = OFF_RET_G + GROUP_W
SMALL_BASE = GLA_DK
SMALL_R = SMALL_BASE
SMALL_MLG = SMALL_BASE + 2 * GLA_RANK

VMEM_LIMIT_BYTES = 56 * 1024 * 1024


def _cparams(*sem):
    return pltpu.CompilerParams(dimension_semantics=sem, vmem_limit_bytes=VMEM_LIMIT_BYTES)


def _pick(n, candidates):
    for c in candidates:
        if n % c == 0:
            return c
    return n


def _silu(x):
    return x / (1.0 + jnp.exp(-x))


def _sigmoid(x):
    return 1.0 / (1.0 + jnp.exp(-x))


def _log_sigmoid(x):
    return jnp.minimum(x, 0.0) - jnp.log(1.0 + jnp.exp(-jnp.abs(x)))


def _dot(a, b):
    return jnp.dot(a, b, preferred_element_type=F32)


def _dot_hi(a, b):
    return jnp.dot(a, b, preferred_element_type=F32, precision=HIGHEST)


def _dot_nt(a, b):
    return lax.dot_general(a, b, (((1,), (1,)), ((), ())), preferred_element_type=F32)


def _dot_tn(a, b):
    return lax.dot_general(a, b, (((0,), (0,)), ((), ())), preferred_element_type=F32)


def _adaln_kernel(c_ref, w_ref, b_ref, o_ref):
    o_ref[0] = _dot_hi(_silu(c_ref[...]), w_ref[0]) + b_ref[0]


def adaln(cvec, ada_w, ada_b):
    depth, d, n = ada_w.shape
    tn = _pick(n, (1024, 512, 256, 128))
    return pl.pallas_call(
        _adaln_kernel,
        grid=(depth, n // tn),
        in_specs=[pl.BlockSpec((8, d), lambda l, j: (0, 0)),
                  pl.BlockSpec((1, d, tn), lambda l, j: (l, 0, j)),
                  pl.BlockSpec((1, 1, tn), lambda l, j: (l, 0, j))],
        out_specs=pl.BlockSpec((1, 8, tn), lambda l, j: (l, 0, j)),
        out_shape=jax.ShapeDtypeStruct((depth, 8, n), F32),
        compiler_params=_cparams("parallel", "parallel"),
        name="adaln",
    )(cvec, ada_w, ada_b.reshape(depth, 1, n))


def _norm_mod(x, g, sc, sh):
    ms = jnp.mean(x * x, axis=-1, keepdims=True)
    return (x * lax.rsqrt(ms + EPS) * g) * (1.0 + sc) + sh


def _inproj_kernel(x_ref, g_ref, sc_ref, sh_ref, w_ref, o_ref, xn_ref):
    @pl.when(pl.program_id(2) == 0)
    def _():
        xn_ref[...] = _norm_mod(x_ref[0], g_ref[...], sc_ref[...], sh_ref[...]).astype(BF16)

    o_ref[0] = _dot(xn_ref[...], w_ref[...])


def inproj(x, g, mod4, row_of_batch, i_sc, i_sh, w):
    b, l, d = x.shape
    n = w.shape[1]
    tm = _pick(l, (1024, 512, 256, 128, 64))
    tn = _pick(n, (768, 512, 256, 128))
    return pl.pallas_call(
        _inproj_kernel,
        grid=(b, l // tm, n // tn),
        in_specs=[pl.BlockSpec((1, tm, d), lambda bi, i, j: (bi, i, 0)),
                  pl.BlockSpec((1, d), lambda bi, i, j: (0, 0)),
                  pl.BlockSpec((None, None, 1, d), lambda bi, i, j: (row_of_batch(bi), i_sc, 0, 0)),
                  pl.BlockSpec((None, None, 1, d), lambda bi, i, j: (row_of_batch(bi), i_sh, 0, 0)),
                  pl.BlockSpec((d, tn), lambda bi, i, j: (0, j))],
        out_specs=pl.BlockSpec((1, tm, tn), lambda bi, i, j: (bi, i, j)),
        out_shape=jax.ShapeDtypeStruct((b, l, n), F32),
        scratch_shapes=[pltpu.VMEM((tm, d), BF16)],
        compiler_params=_cparams("parallel", "parallel", "arbitrary"),
        name="inproj",
    )(x, g.reshape(1, d), mod4, mod4, w)


def _causal_mask(reverse):
    r = lax.broadcasted_iota(jnp.int32, (CHUNK, CHUNK), 0)
    c = lax.broadcasted_iota(jnp.int32, (CHUNK, CHUNK), 1)
    return (c >= r) if reverse else (c <= r)


def _gla_scan_kernel(*refs, reverse, mode, rotate, accumulate, n_chunks, q_scale):
    it = iter(refs)
    q_ref, k_ref, v_ref = next(it), next(it), next(it)
    if mode == "gla":
        gsrc_ref, wa_ref, ba_ref = next(it), next(it), next(it)
    else:
        dec_ref = next(it)
    if rotate:
        cos_ref, sin_ref = next(it), next(it)
    s0_ref = next(it)
    if accumulate:
        oprev_ref = next(it)
    o_ref, sfin_ref, st_ref = next(it), next(it), next(it)

    i = pl.program_id(1)

    @pl.when(i == 0)
    def _():
        st_ref[...] = s0_ref[0]

    mask = _causal_mask(reverse)
    maskf = mask.astype(F32)
    states = [st_ref[h] for h in range(HEADS)]

    def decay_terms(la):
        bc = _dot_hi(maskf, la)
        tot = bc[0:1, :] if reverse else bc[CHUNK - 1:CHUNK, :]
        return jnp.exp(bc), jnp.exp(-bc), jnp.exp(tot - bc), jnp.exp(tot)

    if mode == "ret":
        terms = decay_terms(jnp.broadcast_to(_log_sigmoid(dec_ref[...]), (CHUNK, GROUP_W)))

    for c in range(n_chunks):
        cc = (n_chunks - 1 - c) if reverse else c
        sl = slice(cc * CHUNK, (cc + 1) * CHUNK)
        q = q_ref[0, sl, :]
        k = k_ref[0, sl, :]
        vb = v_ref[0, sl, :].astype(BF16)
        if mode == "gla":
            logit = _dot_hi(gsrc_ref[0, sl, :], wa_ref[...]) + ba_ref[...]
            terms = decay_terms(_log_sigmoid(logit) * (1.0 / GLA_TAU))
        e_bc, e_nbc, e_st, e_tot = terms
        if rotate:
            cos2 = cos_ref[sl, :]
            sin2 = sin_ref[sl, :]
        outs = []
        for h in range(HEADS):
            hs = slice(h * HEAD_W, (h + 1) * HEAD_W)
            qh, kh = q[:, hs], k[:, hs]
            if rotate:
                qh = qh * cos2 + pltpu.roll(qh, HEAD_W // 2, axis=1) * sin2
                kh = kh * cos2 + pltpu.roll(kh, HEAD_W // 2, axis=1) * sin2
            q_in = (qh * q_scale * e_bc[:, hs]).astype(BF16)
            k_in = (kh * e_nbc[:, hs]).astype(BF16)
            k_st = (kh * e_st[:, hs]).astype(BF16)
            att = jnp.where(mask, _dot_nt(q_in, k_in), 0.0).astype(BF16)
            st = states[h]
            outs.append(_dot(att, vb[:, hs]) + _dot_nt(q_in, st.astype(BF16)))
            states[h] = st * e_tot[:, hs] + _dot_tn(vb[:, hs], k_st)
        o = jnp.concatenate(outs, axis=-1)
        if accumulate:
            o = o + oprev_ref[0, sl, :]
        o_ref[0, sl, :] = o

    for h in range(HEADS):
        st_ref[h] = states[h]

    @pl.when(i == pl.num_programs(1) - 1)
    def _():
        sfin_ref[0] = st_ref[...]


def gla_scan(p, off_q, off_k, off_v, s0, *, reverse, mode, gate=None, rot=None, o_prev=None, q_scale):
    b, l, _ = p.shape
    tb = _pick(l, (SCAN_BLOCK, 128, 64))
    nblk = l // tb
    blk = (lambda i: nblk - 1 - i) if reverse else (lambda i: i)
    gspec = lambda off: pl.BlockSpec((1, tb, GROUP_W), lambda bi, i: (bi, blk(i), off // GROUP_W))
    const = lambda shape: pl.BlockSpec(shape, lambda bi, i: (0,) * len(shape))
    in_specs = [gspec(off_q), gspec(off_k), gspec(off_v)]
    args = [p, p, p]
    if mode == "gla":
        wa, ba = gate
        in_specs += [pl.BlockSpec((1, tb, HEAD_W), lambda bi, i: (bi, blk(i), OFF_GLA_K // HEAD_W)),
                     const((HEAD_W, GROUP_W)), const((1, GROUP_W))]
        args += [p, wa, ba]
    else:
        in_specs += [const((1, GROUP_W))]
        args += [gate]
    if rot is not None:
        in_specs += [pl.BlockSpec((tb, HEAD_W), lambda bi, i: (blk(i), 0))] * 2
        args += [rot[0], rot[1]]
    sspec = pl.BlockSpec((1, HEADS, HEAD_W, HEAD_W), lambda bi, i: (bi, 0, 0, 0))
    in_specs += [sspec]
    args += [s0]
    o_spec = pl.BlockSpec((1, tb, GROUP_W), lambda bi, i: (bi, blk(i), 0))
    aliases = {}
    if o_prev is not None:
        in_specs += [o_spec]
        aliases = {len(args): 0}
        args += [o_prev]
    kern = functools.partial(_gla_scan_kernel, reverse=reverse, mode=mode, rotate=rot is not None,
                             accumulate=o_prev is not None, n_chunks=tb // CHUNK, q_scale=q_scale)
    return pl.pallas_call(
        kern,
        grid=(b, nblk),
        in_specs=in_specs,
        out_specs=[o_spec, sspec],
        out_shape=[jax.ShapeDtypeStruct((b, l, GROUP_W), F32),
                   jax.ShapeDtypeStruct((b, HEADS, HEAD_W, HEAD_W), F32)],
        scratch_shapes=[pltpu.VMEM((HEADS, HEAD_W, HEAD_W), F32)],
        input_output_aliases=aliases,
        compiler_params=_cparams("parallel", "arbitrary"),
        name=f"{mode}_scan_{'bwd' if reverse else 'fwd'}",
    )(*args)


def _mlstm_kernel(*refs, reverse, accumulate, n_chunks, k_scale):
    it = iter(refs)
    q_ref, k_ref, v_ref, gsrc_ref, seli_ref, self_ref, bi_ref, bf_ref = (next(it) for _ in range(8))
    c0_ref, n0_ref, m0_ref = next(it), next(it), next(it)
    if accumulate:
        oprev_ref = next(it)
    o_ref, cfin_ref, nfin_ref, mfin_ref = next(it), next(it), next(it), next(it)
    ct_ref, n_ref, m_ref = next(it), next(it), next(it)

    i = pl.program_id(1)

    @pl.when(i == 0)
    def _():
        ct_ref[...] = c0_ref[0]
        n_ref[...] = n0_ref[0]
        m_ref[...] = m0_ref[0]

    mask = _causal_mask(reverse)
    maskf = mask.astype(F32)
    cts = [ct_ref[h] for h in range(HEADS)]
    nvs = [n_ref[h] for h in range(HEADS)]
    mss = [m_ref[h] for h in range(HEADS)]

    for c in range(n_chunks):
        cc = (n_chunks - 1 - c) if reverse else c
        sl = slice(cc * CHUNK, (cc + 1) * CHUNK)
        q = q_ref[0, sl, :]
        k = k_ref[0, sl, :] * k_scale
        vb = v_ref[0, sl, :].astype(BF16)
        gsrc = gsrc_ref[0, sl, :]
        ig = _dot_hi(gsrc, seli_ref[...]) + bi_ref[...]
        fl = _log_sigmoid(_dot_hi(gsrc, self_ref[...]) + bf_ref[...])
        bc = _dot_hi(maskf, fl)
        b_end = bc[0:1, :] if reverse else bc[CHUNK - 1:CHUNK, :]
        w = ig - bc
        g_log = b_end + w
        outs = []
        for h in range(HEADS):
            hs = slice(h * HEAD_W, (h + 1) * HEAD_W)
            qh, kh, vh = q[:, hs], k[:, hs], vb[:, hs]
            bch, wh, bend = bc[:, hs], w[:, hs], b_end[:, hs]
            w_row = jnp.transpose(wh)[:CHUNK, :]
            dmat = jnp.where(mask, bch[:, :CHUNK] + w_row, -jnp.inf)
            rowmax = jnp.max(dmat, axis=-1, keepdims=True)
            qb = qh.astype(BF16)
            s_raw = _dot_nt(qb, kh.astype(BF16)) * jnp.exp(dmat - rowmax)
            sv = _dot(s_raw.astype(BF16), vh)
            s_sum = jnp.sum(s_raw, axis=-1, keepdims=True)
            glh = g_log[:, hs]
            gmax = jnp.max(glh, axis=0, keepdims=True)
            gk = jnp.exp(glh - gmax) * kh
            upd = _dot_tn(vh, gk.astype(BF16))
            n_upd = jnp.sum(gk, axis=0, keepdims=True)
            ct, nv, m_prev = cts[h], nvs[h], mss[h]
            inter_log = bch + m_prev
            m_t = jnp.maximum(inter_log, rowmax)
            e_intra = jnp.exp(rowmax - m_t)
            inter = jnp.exp(inter_log - m_t)
            num = inter * _dot_nt(qb, ct.astype(BF16)) + e_intra * sv
            den = jnp.abs(inter * jnp.sum(qh * nv, axis=-1, keepdims=True) + e_intra * s_sum)
            outs.append(num / jnp.maximum(den, jnp.exp(-m_t)))
            m_new = jnp.maximum(bend + m_prev, gmax)
            dec = jnp.exp(bend + m_prev - m_new)
            e_upd = jnp.exp(gmax - m_new)
            cts[h] = dec * ct + e_upd * upd
            nvs[h] = dec * nv + e_upd * n_upd
            mss[h] = m_new
        o = jnp.concatenate(outs, axis=-1)
        if accumulate:
            o = o + oprev_ref[0, sl, :]
        o_ref[0, sl, :] = o

    for h in range(HEADS):
        ct_ref[h] = cts[h]
        n_ref[h] = nvs[h]
        m_ref[h] = mss[h]

    @pl.when(i == pl.num_programs(1) - 1)
    def _():
        cfin_ref[0] = ct_ref[...]
        nfin_ref[0] = n_ref[...]
        mfin_ref[0] = m_ref[...]


def mlstm_scan(p, state, sel, *, reverse, o_prev=None):
    b, l, _ = p.shape
    tb = _pick(l, (SCAN_BLOCK, 128, 64))
    nblk = l // tb
    blk = (lambda i: nblk - 1 - i) if reverse else (lambda i: i)
    sel_i, sel_f, bias_i, bias_f = sel
    c0, n0, m0 = state
    gspec = lambda off: pl.BlockSpec((1, tb, GROUP_W), lambda bi, i: (bi, blk(i), off // GROUP_W))
    const = lambda shape: pl.BlockSpec(shape, lambda bi, i: (0,) * len(shape))
    cspec = pl.BlockSpec((1, HEADS, HEAD_W, HEAD_W), lambda bi, i: (bi, 0, 0, 0))
    vspec = pl.BlockSpec((1, HEADS, 1, HEAD_W), lambda bi, i: (bi, 0, 0, 0))
    in_specs = [gspec(OFF_ML_Q), gspec(OFF_ML_K), gspec(OFF_ML_V),
                pl.BlockSpec((1, tb, HEAD_W), lambda bi, i: (bi, blk(i), OFF_GLA_K // HEAD_W)),
                const((HEAD_W, GROUP_W)), const((HEAD_W, GROUP_W)), const((1, GROUP_W)), const((1, GROUP_W)),
                cspec, vspec, vspec]
    args = [p, p, p, p, sel_i, sel_f, bias_i, bias_f, c0, n0, m0]
    o_spec = pl.BlockSpec((1, tb, GROUP_W), lambda bi, i: (bi, blk(i), 0))
    aliases = {}
    if o_prev is not None:
        in_specs += [o_spec]
        aliases = {len(args): 0}
        args += [o_prev]
    kern = functools.partial(_mlstm_kernel, reverse=reverse, accumulate=o_prev is not None,
                             n_chunks=tb // CHUNK, k_scale=HEAD_W ** -0.5)
    outs = pl.pallas_call(
        kern,
        grid=(b, nblk),
        in_specs=in_specs,
        out_specs=[o_spec, cspec, vspec, vspec],
        out_shape=[jax.ShapeDtypeStruct((b, l, GROUP_W), F32),
                   jax.ShapeDtypeStruct((b, HEADS, HEAD_W, HEAD_W), F32),
                   jax.ShapeDtypeStruct((b, HEADS, 1, HEAD_W), F32),
                   jax.ShapeDtypeStruct((b, HEADS, 1, HEAD_W), F32)],
        scratch_shapes=[pltpu.VMEM((HEADS, HEAD_W, HEAD_W), F32), pltpu.VMEM((HEADS, 1, HEAD_W), F32),
                        pltpu.VMEM((HEADS, 1, HEAD_W), F32)],
        input_output_aliases=aliases,
        compiler_params=_cparams("parallel", "arbitrary"),
        name=f"mlstm_scan_{'bwd' if reverse else 'fwd'}",
    )(*args)
    return outs[0], (outs[1], outs[2], outs[3])


def _hy_pre_kernel(u_ref, up_ref, un_ref, w_ref, b_ref, o_ref, ob_ref, *, rows):
    i = pl.program_id(2)
    u = u_ref[0]
    prev_row = jnp.where(i == 0, 0.0, up_ref[0, 7:8, :])
    next_row = jnp.where(i == pl.num_programs(2) - 1, 0.0, un_ref[0, 0:1, :])
    ridx = lax.broadcasted_iota(jnp.int32, u.shape, 0)
    u_dn = jnp.where(ridx == 0, prev_row, pltpu.roll(u, 1, axis=0))
    u_up = jnp.where(ridx == rows - 1, next_row, pltpu.roll(u, rows - 1, axis=0))
    y = w_ref[0:1, :] * u_dn + w_ref[1:2, :] * u + w_ref[2:3, :] * u_up + b_ref[...]
    o_ref[0] = y
    ob_ref[0] = y.astype(BF16)


def hy_pre(p, conv_w, conv_b):
    b, l, _ = p.shape
    rows = _pick(l, (512, 256, 128, 64))
    nr = l // rows
    r8 = rows // 8
    n8 = l // 8
    wpad = jnp.zeros((8, 3 * HY_W), F32).at[:3].set(conv_w)
    return pl.pallas_call(
        functools.partial(_hy_pre_kernel, rows=rows),
        grid=(b, 3, nr),
        in_specs=[pl.BlockSpec((1, rows, HY_W), lambda bi, j, i: (bi, i, j)),
                  pl.BlockSpec((1, 8, HY_W), lambda bi, j, i: (bi, jnp.maximum(i * r8 - 1, 0), j)),
                  pl.BlockSpec((1, 8, HY_W), lambda bi, j, i: (bi, jnp.minimum((i + 1) * r8, n8 - 1), j)),
                  pl.BlockSpec((8, HY_W), lambda bi, j, i: (0, j)),
                  pl.BlockSpec((1, HY_W), lambda bi, j, i: (0, j))],
        out_specs=[pl.BlockSpec((1, rows, HY_W), lambda bi, j, i: (bi, i, j))] * 2,
        out_shape=[jax.ShapeDtypeStruct((b, l, 3 * HY_W), F32),
                   jax.ShapeDtypeStruct((b, l, 3 * HY_W), BF16)],
        compiler_params=_cparams("parallel", "parallel", "parallel"),
        name="hy_shortconv",
    )(p, p, p, wpad, conv_b.reshape(1, 3 * HY_W))


def _mm_kernel(a_ref, b_ref, o_ref):
    o_ref[...] = _dot(a_ref[...], b_ref[...])


def matmul_bf16(a, bm):
    m, k = a.shape
    n = bm.shape[1]
    tm = _pick(m, (512, 256, 128, 64))
    tn = _pick(n, (512, 256, 128))
    return pl.pallas_call(
        _mm_kernel,
        grid=(n // tn, m // tm),
        in_specs=[pl.BlockSpec((tm, k), lambda j, i: (i, 0)),
                  pl.BlockSpec((k, tn), lambda j, i: (0, j))],
        out_specs=pl.BlockSpec((tm, tn), lambda j, i: (i, j)),
        out_shape=jax.ShapeDtypeStruct((m, n), F32),
        compiler_params=_cparams("parallel", "parallel"),
        name="matmul_bf16",
    )(a, bm)


def _dft_fwd_kernel(c_ref, s_ref, z_ref, hr_ref, hi_ref, yr_ref, yi_ref):
    z = z_ref[0]
    cz = _dot(c_ref[...], z)
    sz = _dot(s_ref[...], z)
    hr = hr_ref[...]
    hi = hi_ref[...]
    yr_ref[0] = (cz * hr + sz * hi).astype(BF16)
    yi_ref[0] = (cz * hi - sz * hr).astype(BF16)


def dft_fwd(cmat, smat, zb, z_col, hr, hi, h_col):
    b, l, _ = zb.shape
    f = cmat.shape[0]
    tf = _pick(f, (512, 256, 128, 64))
    return pl.pallas_call(
        _dft_fwd_kernel,
        grid=(f // tf, b),
        in_specs=[pl.BlockSpec((tf, l), lambda i, bi: (i, 0)),
                  pl.BlockSpec((tf, l), lambda i, bi: (i, 0)),
                  pl.BlockSpec((1, l, HY_W), lambda i, bi: (bi, 0, z_col)),
                  pl.BlockSpec((tf, HY_W), lambda i, bi: (i, h_col)),
                  pl.BlockSpec((tf, HY_W), lambda i, bi: (i, h_col))],
        out_specs=[pl.BlockSpec((1, tf, HY_W), lambda i, bi: (bi, i, 0))] * 2,
        out_shape=[jax.ShapeDtypeStruct((b, f, HY_W), BF16)] * 2,
        compiler_params=_cparams("parallel", "parallel"),
        name="hy_dft_fwd",
    )(cmat, smat, zb, hr, hi)


def _dft_inv_kernel(ct_ref, st_ref, yr_ref, yi_ref, x_ref, zp_ref, skip_ref, o_ref, ob_ref):
    conv = _dot(ct_ref[...], yr_ref[0]) - _dot(st_ref[...], yi_ref[0])
    z = x_ref[0] * (conv + skip_ref[...] * zp_ref[0])
    o_ref[0] = z
    ob_ref[0] = z.astype(BF16)


def dft_inv(ctmat, stmat, yr, yi, x_arr, x_col, zp_arr, zp_col, skip):
    b, f, _ = yr.shape
    l = ctmat.shape[0]
    tt = _pick(l, (512, 256, 128, 64))
    return pl.pallas_call(
        _dft_inv_kernel,
        grid=(l // tt, b),
        in_specs=[pl.BlockSpec((tt, f), lambda i, bi: (i, 0)),
                  pl.BlockSpec((tt, f), lambda i, bi: (i, 0)),
                  pl.BlockSpec((1, f, HY_W), lambda i, bi: (bi, 0, 0)),
                  pl.BlockSpec((1, f, HY_W), lambda i, bi: (bi, 0, 0)),
                  pl.BlockSpec((1, tt, HY_W), lambda i, bi: (bi, i, x_col)),
                  pl.BlockSpec((1, tt, HY_W), lambda i, bi: (bi, i, zp_col)),
                  pl.BlockSpec((1, HY_W), lambda i, bi: (0, 0))],
        out_specs=[pl.BlockSpec((1, tt, HY_W), lambda i, bi: (bi, i, 0))] * 2,
        out_shape=[jax.ShapeDtypeStruct((b, l, HY_W), F32),
                   jax.ShapeDtypeStruct((b, l, HY_W), BF16)],
        compiler_params=_cparams("parallel", "parallel"),
        name="hy_dft_inv",
    )(ctmat, stmat, yr, yi, x_arr, zp_arr, skip.reshape(1, HY_W))


def dft_tables(l):
    r = _pick(l, (64, 32, 16, 8))
    period = 8 * l
    odd = 2 * jnp.arange(l, dtype=jnp.int32) + 1
    s1 = jnp.arange(l // r, dtype=jnp.int32) * (2 * r)
    s0 = 2 * jnp.arange(r, dtype=jnp.int32) + 1
    ang = lambda ph: ph.astype(F32) * (2.0 * math.pi / period)
    a = ang((odd[:, None] * s1[None, :]) % period)
    bb = ang((odd[:, None] * s0[None, :]) % period)
    ca, sa, cb, sb = jnp.cos(a), jnp.sin(a), jnp.cos(bb), jnp.sin(bb)
    cmat = (ca[:, :, None] * cb[:, None, :] - sa[:, :, None] * sb[:, None, :]).reshape(l, l).astype(BF16)
    smat = (sa[:, :, None] * cb[:, None, :] + ca[:, :, None] * sb[:, None, :]).reshape(l, l).astype(BF16)
    phi = ang(odd)[:, None]
    return cmat, smat, jnp.cos(phi), jnp.sin(phi)


def hyena_filter_taps(length, lp):
    pos = jnp.arange(length, dtype=F32)
    t = pos / (length - 1)
    bands = (HY_EMB - 1) // 2
    fr = jnp.linspace(1e-4, bands - 1, bands, dtype=F32)
    ang = (2.0 * math.pi / length) * pos[:, None] * fr[None, :]
    z = jnp.concatenate([t[:, None], jnp.cos(ang), -jnp.sin(ang)], axis=-1)
    mm = functools.partial(jnp.matmul, precision=HIGHEST)
    hdn = jnp.sin(lp['hy_f_freq'][0] * (mm(z, lp['hy_f_w1']) + lp['hy_f_b1']))
    hdn = jnp.sin(lp['hy_f_freq'][1] * (mm(hdn, lp['hy_f_w2']) + lp['hy_f_b2']))
    return mm(hdn, lp['hy_f_w3']) * jnp.exp(-t[:, None] * jnp.abs(lp['hy_decay']))


def hyena_mixer(p, lp, tables):
    b, l, _ = p.shape
    cmat, smat, cphi, sphi = tables
    uc, ucb = hy_pre(p, lp['hy_conv_w'], lp['hy_conv_b'])
    taps = hyena_filter_taps(l, lp).reshape(l, 2, 2, HY_W)
    hf = taps[:, :, 0, :]
    hbs = jnp.concatenate([taps[1:, :, 1, :], jnp.zeros((1, 2, HY_W), F32)], axis=0)
    hsum = (hf + hbs).reshape(l, 2 * HY_W).astype(BF16)
    hdiff = (hf - hbs).reshape(l, 2 * HY_W).astype(BF16)
    hr_s = matmul_bf16(cmat, hsum)
    hi_s = -matmul_bf16(smat, hdiff)
    hr = (cphi * hr_s - sphi * hi_s) * (1.0 / l)
    hi = (sphi * hr_s + cphi * hi_s) * (1.0 / l)
    yr, yi = dft_fwd(cmat, smat, ucb, 0, hr, hi, 0)
    z1, z1b = dft_inv(cmat, smat, yr, yi, uc, 1, uc, 0, lp['hy_skip'][0])
    yr, yi = dft_fwd(cmat, smat, z1b, 0, hr, hi, 1)
    z2, _ = dft_inv(cmat, smat, yr, yi, uc, 2, z1, 0, lp['hy_skip'][1])
    return z2


def _head_rms(y):
    parts = []
    for h in range(HEADS):
        yh = y[:, h * HEAD_W:(h + 1) * HEAD_W]
        parts.append(yh * lax.rsqrt(jnp.mean(yh * yh, axis=-1, keepdims=True) + EPS))
    return jnp.concatenate(parts, axis=-1)


def _outproj_kernel(hy_ref, gla_ref, glag_ref, ml_ref, mlo_ref, ret_ref, retg_ref,
                    gn_ref, w_ref, res_ref, gate_ref, o_ref):
    gn = gn_ref[...]
    y_gla = _head_rms(gla_ref[0]) * gn[0:1, :] * _silu(glag_ref[0])
    y_ml = _head_rms(_sigmoid(mlo_ref[0]) * ml_ref[0]) * gn[1:2, :]
    y_ret = _head_rms(ret_ref[0]) * gn[2:3, :] * _silu(retg_ref[0])
    acc = _dot(hy_ref[0].astype(BF16), w_ref[0:GROUP_W, :])
    acc += _dot(y_gla.astype(BF16), w_ref[GROUP_W:2 * GROUP_W, :])
    acc += _dot(y_ml.astype(BF16), w_ref[2 * GROUP_W:3 * GROUP_W, :])
    acc += _dot(y_ret.astype(BF16), w_ref[3 * GROUP_W:4 * GROUP_W, :])
    o_ref[0] = res_ref[0] + gate_ref[...] * acc


def outproj(hy, o_gla, o_ml, o_ret, p, gains, w_out, res, mod4, row_of_batch, i_gate):
    b, l, d = res.shape
    tm = _pick(l, (256, 128, 64))
    gw = GROUP_W
    ospec = pl.BlockSpec((1, tm, gw), lambda bi, i: (bi, i, 0))
    pspec = lambda off: pl.BlockSpec((1, tm, gw), lambda bi, i: (bi, i, off // gw))
    return pl.pallas_call(
        _outproj_kernel,
        grid=(b, l // tm),
        in_specs=[ospec, ospec, pspec(OFF_GLA_G), ospec, pspec(OFF_ML_O), ospec, pspec(OFF_RET_G),
                  pl.BlockSpec((8, gw), lambda bi, i: (0, 0)),
                  pl.BlockSpec((4 * gw, d), lambda bi, i: (0, 0)),
                  pl.BlockSpec((1, tm, d), lambda bi, i: (bi, i, 0)),
                  pl.BlockSpec((None, None, 1, d), lambda bi, i: (row_of_batch(bi), i_gate, 0, 0))],
        out_specs=pl.BlockSpec((1, tm, d), lambda bi, i: (bi, i, 0)),
        out_shape=jax.ShapeDtypeStruct((b, l, d), F32),
        compiler_params=_cparams("parallel", "parallel"),
        name="outproj",
    )(hy, o_gla, p, o_ml, p, o_ret, p, gains, w_out, res, mod4)


def _moe_pre_kernel(x_ref, g_ref, sc_ref, sh_ref, wr_ref, br_ref, t_ref, lg_ref):
    t = _norm_mod(x_ref[0], g_ref[...], sc_ref[...], sh_ref[...])
    t_ref[0] = t.astype(BF16)
    lg_ref[0] = _dot_hi(t, wr_ref[...]) + br_ref[...]


def moe_pre(x, g, mod4, row_of_batch, i_sc, i_sh, w_router, b_router):
    b, l, d = x.shape
    tm = _pick(l, (512, 256, 128, 64))
    nr = w_router.shape[1]
    return pl.pallas_call(
        _moe_pre_kernel,
        grid=(b, l // tm),
        in_specs=[pl.BlockSpec((1, tm, d), lambda bi, i: (bi, i, 0)),
                  pl.BlockSpec((1, d), lambda bi, i: (0, 0)),
                  pl.BlockSpec((None, None, 1, d), lambda bi, i: (row_of_batch(bi), i_sc, 0, 0)),
                  pl.BlockSpec((None, None, 1, d), lambda bi, i: (row_of_batch(bi), i_sh, 0, 0)),
                  pl.BlockSpec((d, nr), lambda bi, i: (0, 0)),
                  pl.BlockSpec((1, nr), lambda bi, i: (0, 0))],
        out_specs=[pl.BlockSpec((1, tm, d), lambda bi, i: (bi, i, 0)),
                   pl.BlockSpec((1, tm, nr), lambda bi, i: (bi, i, 0))],
        out_shape=[jax.ShapeDtypeStruct((b, l, d), BF16), jax.ShapeDtypeStruct((b, l, nr), F32)],
        compiler_params=_cparams("parallel", "parallel"),
        name="moe_pre",
    )(x, g.reshape(1, d), mod4, mod4, w_router, b_router)


def _moe_expert_kernel(te_ref, nu_ref, x_ref, w1_ref, w3_ref, w2_ref, rw_ref, o_ref, w1b_ref, w3b_ref, w2b_ref):
    t = pl.program_id(0)

    @pl.when(t < nu_ref[0])
    def _():
        @pl.when(jnp.logical_or(t == 0, te_ref[t] != te_ref[jnp.maximum(t - 1, 0)]))
        def _():
            w1b_ref[...] = w1_ref[0].astype(BF16)
            w3b_ref[...] = w3_ref[0].astype(BF16)
            w2b_ref[...] = w2_ref[0].astype(BF16)

        x = x_ref[...]
        act = _silu(_dot(x, w1b_ref[...])) * _dot(x, w3b_ref[...]) * rw_ref[...]
        o_ref[...] = _dot(act.astype(BF16), w2b_ref[...])


def moe_experts(x_sorted, w1, w3, w2, row_w, tile_expert, n_used, tm):
    pmax, d = x_sorted.shape
    ff = w1.shape[2]
    grid_spec = pltpu.PrefetchScalarGridSpec(
        num_scalar_prefetch=2,
        grid=(pmax // tm,),
        in_specs=[pl.BlockSpec((tm, d), lambda t, te, nu: (t, 0)),
                  pl.BlockSpec((1, d, ff), lambda t, te, nu: (te[t], 0, 0)),
                  pl.BlockSpec((1, d, ff), lambda t, te, nu: (te[t], 0, 0)),
                  pl.BlockSpec((1, ff, d), lambda t, te, nu: (te[t], 0, 0)),
                  pl.BlockSpec((tm, 1), lambda t, te, nu: (t, 0))],
        out_specs=pl.BlockSpec((tm, d), lambda t, te, nu: (t, 0)),
        scratch_shapes=[pltpu.VMEM((d, ff), BF16), pltpu.VMEM((d, ff), BF16), pltpu.VMEM((ff, d), BF16)],
    )
    return pl.pallas_call(
        _moe_expert_kernel,
        grid_spec=grid_spec,
        out_shape=jax.ShapeDtypeStruct((pmax, d), F32),
        compiler_params=_cparams("arbitrary"),
        name="moe_experts",
    )(tile_expert, n_used, x_sorted, w1, w3, w2, row_w)


def _moe_combine_kernel(res_ref, y1_ref, y2_ref, gate_ref, o_ref):
    o_ref[0] = res_ref[0] + gate_ref[...] * (y1_ref[0] + y2_ref[0])


def moe_combine(res, y1, y2, mod4, row_of_batch, i_gate):
    b, l, d = res.shape
    tm = _pick(l, (512, 256, 128, 64))
    spec = pl.BlockSpec((1, tm, d), lambda bi, i: (bi, i, 0))
    return pl.pallas_call(
        _moe_combine_kernel,
        grid=(b, l // tm),
        in_specs=[spec, spec, spec,
                  pl.BlockSpec((None, None, 1, d), lambda bi, i: (row_of_batch(bi), i_gate, 0, 0))],
        out_specs=spec,
        out_shape=jax.ShapeDtypeStruct((b, l, d), F32),
        compiler_params=_cparams("parallel", "parallel"),
        name="moe_combine",
    )(res, y1, y2, mod4)


def moe_route(logits, tm):
    t = logits.shape[0]
    g_prob = jax.nn.softmax(logits[:, :MOE_GROUPS], axis=-1)
    gidx = jnp.argmax(g_prob, axis=-1)
    pg = jnp.max(g_prob, axis=-1)
    e_logits = logits[:, MOE_GROUPS:MOE_GROUPS + N_EXPERTS].reshape(t, MOE_GROUPS, MOE_EPG)
    sel = jnp.take_along_axis(e_logits, gidx[:, None, None], axis=1)[:, 0]
    tv, ti = lax.top_k(sel, 2)
    w = jax.nn.softmax(tv, axis=-1) * pg[:, None]
    eid = (gidx[:, None] * MOE_EPG + ti).astype(jnp.int32).reshape(-1)
    wflat = w.reshape(-1)
    order = jnp.argsort(eid, stable=True)
    counts = jnp.sum(eid[:, None] == jnp.arange(N_EXPERTS)[None, :], axis=0).astype(jnp.int32)
    padded = ((counts + tm - 1) // tm) * tm
    pend = jnp.cumsum(padded)
    pstart = pend - padded
    cstart = jnp.cumsum(counts) - counts
    pmax = 2 * t + N_EXPERTS * tm
    n_tiles = pmax // tm
    rows = jnp.arange(pmax, dtype=jnp.int32)
    row_e = jnp.minimum(jnp.sum(rows[:, None] >= pend[None, :], axis=1), N_EXPERTS - 1).astype(jnp.int32)
    j = rows - pstart[row_e]
    valid = j < counts[row_e]
    src = jnp.clip(cstart[row_e] + j, 0, 2 * t - 1)
    pair = order[src]
    row_token = jnp.where(valid, pair // 2, 0).astype(jnp.int32)
    row_w = jnp.where(valid, wflat[pair], 0.0)
    inv = jnp.zeros((2 * t,), jnp.int32).at[order].set(jnp.arange(2 * t, dtype=jnp.int32))
    pos = (pstart[eid] + inv - cstart[eid]).reshape(t, 2)
    tile_expert = row_e[::tm]
    n_used = (pend[-1] // tm).astype(jnp.int32).reshape(1)
    return row_token, row_w.reshape(pmax, 1), pos, tile_expert, n_used, n_tiles


def hier_moe(x, lp, mod4, row_of_batch):
    b, l, d = x.shape
    t = b * l
    tm = 256 if t >= 4096 else 64
    tok, logits = moe_pre(x, lp['norm2_g'], mod4, row_of_batch, 4, 3, lp['w_router'], lp['b_router'])
    tok = tok.reshape(t, d)
    row_token, row_w, pos, tile_expert, n_used, _ = moe_route(logits.reshape(t, -1), tm)
    x_sorted = jnp.take(tok, row_token, axis=0)
    y_sorted = moe_experts(x_sorted, lp['moe_w1e'], lp['moe_w3e'], lp['moe_w2e'], row_w,
                           tile_expert + lp['expert_base'], n_used, tm)
    y1 = jnp.take(y_sorted, pos[:, 0], axis=0).reshape(b, l, d)
    y2 = jnp.take(y_sorted, pos[:, 1], axis=0).reshape(b, l, d)
    return moe_combine(x, y1, y2, mod4, row_of_batch, 5)


def _final_norm_kernel(x_ref, g_ref, o_ref):
    x = x_ref[0]
    o_ref[0] = x * lax.rsqrt(jnp.mean(x * x, axis=-1, keepdims=True) + EPS) * g_ref[...]


def final_norm(x, g):
    b, l, d = x.shape
    tm = _pick(l, (512, 256, 128, 64))
    spec = pl.BlockSpec((1, tm, d), lambda bi, i: (bi, i, 0))
    return pl.pallas_call(
        _final_norm_kernel,
        grid=(b, l // tm),
        in_specs=[spec, pl.BlockSpec((1, d), lambda bi, i: (0, 0))],
        out_specs=spec,
        out_shape=jax.ShapeDtypeStruct((b, l, d), F32),
        compiler_params=_cparams("parallel", "parallel"),
        name="final_norm",
    )(x, g.reshape(1, d))


def _pad_heads(w, dk):
    d = w.shape[0]
    return jnp.pad(w.reshape(d, HEADS, dk), ((0, 0), (0, 0), (0, HEAD_W - dk))).reshape(d, HEADS * HEAD_W)


def prep_w_in(w_in):
    o = 0
    take = lambda n: (w_in[:, o:o + n], o + n)
    hy, o = take(3 * HY_W)
    gq, o = take(HEADS * GLA_DK)
    gk, o = take(HEADS * GLA_DK)
    gv, o = take(GROUP_W)
    gg, o = take(GROUP_W)
    gr, o = take(2 * GLA_RANK)
    ml, o = take(4 * GROUP_W)
    mg, o = take(16)
    ret, o = take(4 * GROUP_W)
    gk_p = _pad_heads(gk, GLA_DK)
    gk_p = gk_p.at[:, SMALL_R:SMALL_R + 2 * GLA_RANK].set(gr).at[:, SMALL_MLG:SMALL_MLG + 16].set(mg)
    return jnp.concatenate([hy, _pad_heads(gq, GLA_DK), gk_p, gv, gg, ml, ret], axis=1).astype(BF16)


def prep_layer(l, prm):
    lp = {k: v[l] for k, v in prm.items()}
    lp['w_in_p'] = prep_w_in(lp['w_in'])
    lp['w_out_b'] = lp['w_out'].astype(BF16)
    wa, ba = [], []
    for d in range(2):
        w = jnp.zeros((HEAD_W, HEADS * HEAD_W), F32)
        w = w.at[SMALL_R + d * GLA_RANK:SMALL_R + (d + 1) * GLA_RANK].set(_pad_heads(lp['gla_wa2'][d], GLA_DK))
        wa.append(w)
        ba.append(_pad_heads(lp['gla_ba'][d][None, :], GLA_DK))
    lp['gla_gate'] = list(zip(wa, ba))
    sel = []
    lane = jnp.arange(HEAD_W)[:, None]
    for d in range(2):
        pick = lambda gsel: jnp.concatenate(
            [jnp.broadcast_to((lane == SMALL_MLG + d * 8 + gsel * HEADS + h).astype(F32), (HEAD_W, HEAD_W))
             for h in range(HEADS)], axis=1)
        bias = lambda gsel: jnp.repeat(lp['ml_gate_b'][d, gsel], HEAD_W)[None, :]
        sel.append((pick(0), pick(1), bias(0), bias(1)))
    lp['ml_sel'] = sel
    lp['ret_dec'] = [jnp.repeat(lp['ret_decay'][d], HEAD_W)[None, :] for d in range(2)]
    gains = jnp.zeros((8, GROUP_W), F32)
    lp['gains'] = gains.at[0].set(lp['gla_norm_g']).at[1].set(lp['ml_norm_g']).at[2].set(lp['ret_norm_g'])
    d_model = lp['moe_wg'].shape[0]
    wr = jnp.zeros((d_model, HEAD_W), F32)
    lp['w_router'] = wr.at[:, :MOE_GROUPS].set(lp['moe_wg']).at[:, MOE_GROUPS:MOE_GROUPS + N_EXPERTS].set(lp['moe_we'])
    br = jnp.zeros((1, HEAD_W), F32)
    lp['b_router'] = br.at[0, :MOE_GROUPS].set(lp['moe_bg']).at[0, MOE_GROUPS:MOE_GROUPS + N_EXPERTS].set(lp['moe_be'])
    lp['moe_w1e'] = prm['moe_w1'].reshape(-1, d_model, MOE_FF)
    lp['moe_w3e'] = prm['moe_w3'].reshape(-1, d_model, MOE_FF)
    lp['moe_w2e'] = prm['moe_w2'].reshape(-1, MOE_FF, d_model)
    lp['expert_base'] = l * N_EXPERTS
    return lp


def rotary_tables(length):
    rows = length // GRID_W
    row = jnp.repeat(jnp.arange(rows, dtype=F32), GRID_W)
    col = jnp.tile(jnp.arange(GRID_W, dtype=F32), rows)
    nf = HEAD_W // 4
    inv = ROPE_BASE ** (-jnp.arange(nf, dtype=F32) / nf)
    ang = jnp.concatenate([row[:, None] * inv, col[:, None] * inv], axis=-1)
    cos, sin = jnp.cos(ang), jnp.sin(ang)
    return jnp.concatenate([cos, cos], axis=-1), jnp.concatenate([-sin, sin], axis=-1)


def bidir_scans(p_ctx, p_lat, lp, rot):
    b = p_lat.shape[0]
    zeros_s = jnp.zeros((b, HEADS, HEAD_W, HEAD_W), F32)
    zeros_v = jnp.zeros((b, HEADS, 1, HEAD_W), F32)
    out = {}
    for name, mode, offs, scale in (("gla", "gla", (OFF_GLA_Q, OFF_GLA_K, OFF_GLA_V), GLA_DK ** -0.5),
                                    ("ret", "ret", (OFF_RET_Q, OFF_RET_K, OFF_RET_V), HEAD_W ** -0.5)):
        o_c, o_l = None, None
        for d in range(2):
            gate = lp['gla_gate'][d] if mode == "gla" else lp['ret_dec'][d]
            o_c, st = gla_scan(p_ctx, *offs, zeros_s, reverse=d == 1, mode=mode, gate=gate,
                               o_prev=o_c, q_scale=scale)
            o_l, _ = gla_scan(p_lat, *offs, st, reverse=d == 1, mode=mode, gate=gate,
                              rot=rot if mode == "ret" else None, o_prev=o_l, q_scale=scale)
        out[name] = (o_c, o_l)
    o_c, o_l = None, None
    for d in range(2):
        o_c, st = mlstm_scan(p_ctx, (zeros_s, zeros_v, zeros_v), lp['ml_sel'][d], reverse=d == 1, o_prev=o_c)
        o_l, _ = mlstm_scan(p_lat, st, lp['ml_sel'][d], reverse=d == 1, o_prev=o_l)
    out["ml"] = (o_c, o_l)
    return out


def kernel(x, c, ctx, c_ctx, ada_w, ada_b, norm1_g, norm2_g, w_in, hy_conv_w, hy_conv_b, hy_f_w1, hy_f_b1, hy_f_w2, hy_f_b2, hy_f_freq, hy_f_w3, hy_decay, hy_skip, gla_wa2, gla_ba, gla_norm_g, ml_gate_b, ml_norm_g, ret_decay, ret_norm_g, w_out, moe_wg, moe_bg, moe_we, moe_be, moe_w1, moe_w3, moe_w2, final_g):
    prm = dict(norm1_g=norm1_g, norm2_g=norm2_g, w_in=w_in, hy_conv_w=hy_conv_w, hy_conv_b=hy_conv_b,
               hy_f_w1=hy_f_w1, hy_f_b1=hy_f_b1, hy_f_w2=hy_f_w2, hy_f_b2=hy_f_b2, hy_f_freq=hy_f_freq,
               hy_f_w3=hy_f_w3, hy_decay=hy_decay, hy_skip=hy_skip, gla_wa2=gla_wa2, gla_ba=gla_ba,
               gla_norm_g=gla_norm_g, ml_gate_b=ml_gate_b, ml_norm_g=ml_norm_g, ret_decay=ret_decay,
               ret_norm_g=ret_norm_g, w_out=w_out, moe_wg=moe_wg, moe_bg=moe_bg, moe_we=moe_we,
               moe_be=moe_be, moe_w1=moe_w1, moe_w3=moe_w3, moe_w2=moe_w2)
    depth = ada_w.shape[0]
    b, seq, d = x.shape
    lc = ctx.shape[1]
    lat = x.astype(F32)
    cx = ctx.astype(F32)
    cvec = jnp.zeros((8, d), F32).at[:b].set(c.astype(F32)).at[b].set(c_ctx.astype(F32))
    mod_all = adaln(cvec, ada_w, ada_b)
    lat_row = lambda bi: bi
    ctx_row = lambda bi: b
    rot = rotary_tables(seq)
    tab_lat = dft_tables(seq)
    tab_ctx = dft_tables(lc)
    for l in range(depth):
        with_ctx = l < depth - 1
        lp = prep_layer(l, prm)
        mod4 = mod_all[l].reshape(8, 6, 1, d)
        p_lat = inproj(lat, lp['norm1_g'], mod4, lat_row, 1, 0, lp['w_in_p'])
        p_ctx = inproj(cx, lp['norm1_g'], mod4, ctx_row, 1, 0, lp['w_in_p'])
        scans = bidir_scans(p_ctx, p_lat, lp, rot)
        hy_lat = hyena_mixer(p_lat, lp, tab_lat)
        lat = outproj(hy_lat, scans["gla"][1], scans["ml"][1], scans["ret"][1], p_lat, lp['gains'],
                      lp['w_out_b'], lat, mod4, lat_row, 2)
        lat = hier_moe(lat, lp, mod4, lat_row)
        if with_ctx:
            hy_ctx = hyena_mixer(p_ctx, lp, tab_ctx)
            cx = outproj(hy_ctx, scans["gla"][0], scans["ml"][0], scans["ret"][0], p_ctx, lp['gains'],
                         lp['w_out_b'], cx, mod4, ctx_row, 2)
            cx = hier_moe(cx, lp, mod4, ctx_row)
    return final_norm(lat, final_g).astype(x.dtype)
```

```python
import functools
import math

import jax
import jax.numpy as jnp
from jax import lax
from jax.experimental import pallas as pl
from jax.experimental.pallas import tpu as pltpu

F32 = jnp.float32
BF16 = jnp.bfloat16
HIGHEST = lax.Precision.HIGHEST

EPS = 1e-6
CHUNK = 64
SCAN_BLOCK = 256
GRID_W = 64
ROPE_BASE = 10000.0
HEADS = 4
HEAD_W = 128
HY_W = 512
HY_EMB = 33
GLA_DK = 64
GLA_RANK = 16
GLA_TAU = 16.0
MOE_GROUPS = 4
MOE_EPG = 8
MOE_FF = 256
N_EXPERTS = MOE_GROUPS * MOE_EPG

GROUP_W = 512
OFF_HY = 0
OFF_GLA_Q = 1536
OFF_GLA_K = OFF_GLA_Q + GROUP_W
OFF_GLA_V = OFF_GLA_K + GROUP_W
OFF_GLA_G = OFF_GLA_V + GROUP_W
OFF_ML_Q = OFF_GLA_G + GROUP_W
OFF_ML_K = OFF_ML_Q + GROUP_W
OFF_ML_V = OFF_ML_K + GROUP_W
OFF_ML_O = OFF_ML_V + GROUP_W
OFF_RET_Q = OFF_ML_O + GROUP_W
OFF_RET_K = OFF_RET_Q + GROUP_W
OFF_RET_V = OFF_RET_K + GROUP_W
OFF_RET_G = OFF_RET_V + GROUP_W
N_PROJ = OFF_RET_G + GROUP_W
SMALL_BASE = GLA_DK
SMALL_R = SMALL_BASE
SMALL_MLG = SMALL_BASE + 2 * GLA_RANK

VMEM_LIMIT_BYTES = 56 * 1024 * 1024


def _cparams(*sem):
    return pltpu.CompilerParams(dimension_semantics=sem, vmem_limit_bytes=VMEM_LIMIT_BYTES)


def _pick(n, candidates):
    for c in candidates:
        if n % c == 0:
            return c
    return n


def _silu(x):
    return x / (1.0 + jnp.exp(-x))


def _sigmoid(x):
    return 1.0 / (1.0 + jnp.exp(-x))


def _log_sigmoid(x):
    return jnp.minimum(x, 0.0) - jnp.log(1.0 + jnp.exp(-jnp.abs(x)))


def _dot(a, b):
    return jnp.dot(a, b, preferred_element_type=F32)


def _dot_hi(a, b):
    return jnp.dot(a, b, preferred_element_type=F32, precision=HIGHEST)


def _split_bf16(x, parts):
    out = []
    for _ in range(parts - 1):
        hi = x.astype(BF16)
        out.append(hi)
        x = x - hi.astype(F32)
    out.append(x.astype(BF16))
    return out


def _dot_exact_lhs(a_bf16, x):
    return sum(_dot(a_bf16, p) for p in _split_bf16(x, 3))


def _dot_nt(a, b):
    return lax.dot_general(a, b, (((1,), (1,)), ((), ())), preferred_element_type=F32)


def _dot_tn(a, b):
    return lax.dot_general(a, b, (((0,), (0,)), ((), ())), preferred_element_type=F32)


def _adaln_kernel(c_ref, w_ref, b_ref, o_ref):
    o_ref[0] = _dot_hi(_silu(c_ref[...]), w_ref[0]) + b_ref[0]


def adaln(cvec, ada_w, ada_b):
    depth, d, n = ada_w.shape
    tn = _pick(n, (1024, 512, 256, 128))
    return pl.pallas_call(
        _adaln_kernel,
        grid=(depth, n // tn),
        in_specs=[pl.BlockSpec((8, d), lambda l, j: (0, 0)),
                  pl.BlockSpec((1, d, tn), lambda l, j: (l, 0, j)),
                  pl.BlockSpec((1, 1, tn), lambda l, j: (l, 0, j))],
        out_specs=pl.BlockSpec((1, 8, tn), lambda l, j: (l, 0, j)),
        out_shape=jax.ShapeDtypeStruct((depth, 8, n), F32),
        compiler_params=_cparams("parallel", "parallel"),
        name="adaln",
    )(cvec, ada_w, ada_b.reshape(depth, 1, n))


def _norm_mod(x, g, sc, sh):
    ms = jnp.mean(x * x, axis=-1, keepdims=True)
    return (x * lax.rsqrt(ms + EPS) * g) * (1.0 + sc) + sh


def _inproj_kernel(x_ref, g_ref, sc_ref, sh_ref, w_ref, o_ref, xn_ref):
    @pl.when(pl.program_id(2) == 0)
    def _():
        xn_ref[...] = _norm_mod(x_ref[0], g_ref[...], sc_ref[...], sh_ref[...]).astype(BF16)

    o_ref[0] = _dot(xn_ref[...], w_ref[...])


def inproj(x, g, mod4, row_of_batch, i_sc, i_sh, w):
    b, l, d = x.shape
    n = w.shape[1]
    tm = _pick(l, (1024, 512, 256, 128, 64))
    tn = _pick(n, (768, 512, 256, 128))
    return pl.pallas_call(
        _inproj_kernel,
        grid=(b, l // tm, n // tn),
        in_specs=[pl.BlockSpec((1, tm, d), lambda bi, i, j: (bi, i, 0)),
                  pl.BlockSpec((1, d), lambda bi, i, j: (0, 0)),
                  pl.BlockSpec((None, None, 1, d), lambda bi, i, j: (row_of_batch(bi), i_sc, 0, 0)),
                  pl.BlockSpec((None, None, 1, d), lambda bi, i, j: (row_of_batch(bi), i_sh, 0, 0)),
                  pl.BlockSpec((d, tn), lambda bi, i, j: (0, j))],
        out_specs=pl.BlockSpec((1, tm, tn), lambda bi, i, j: (bi, i, j)),
        out_shape=jax.ShapeDtypeStruct((b, l, n), F32),
        scratch_shapes=[pltpu.VMEM((tm, d), BF16)],
        compiler_params=_cparams("parallel", "parallel", "arbitrary"),
        name="inproj",
    )(x, g.reshape(1, d), mod4, mod4, w)


def _causal_mask(n, reverse):
    r = lax.broadcasted_iota(jnp.int32, (n, n), 0)
    c = lax.broadcasted_iota(jnp.int32, (n, n), 1)
    same_chunk = (r // CHUNK) == (c // CHUNK)
    return jnp.logical_and(same_chunk, (c >= r) if reverse else (c <= r))


def _gla_scan_kernel(*refs, reverse, mode, rotate, accumulate, n_chunks, q_scale):
    it = iter(refs)
    q_ref, k_ref, v_ref = next(it), next(it), next(it)
    if mode == "gla":
        gsrc_ref, wah_ref, wal_ref, ba_ref = next(it), next(it), next(it), next(it)
    else:
        dec_ref = next(it)
    if rotate:
        cos_ref, sin_ref = next(it), next(it)
    s0_ref = next(it)
    if accumulate:
        oprev_ref = next(it)
    o_ref, sfin_ref, st_ref = next(it), next(it), next(it)

    i = pl.program_id(1)

    @pl.when(i == 0)
    def _():
        st_ref[...] = s0_ref[0]

    tb = n_chunks * CHUNK
    mask = _causal_mask(tb, reverse)
    maskb = mask.astype(BF16)
    order = [(n_chunks - 1 - c) if reverse else c for c in range(n_chunks)]
    rows = [slice(c * CHUNK, (c + 1) * CHUNK) for c in range(n_chunks)]

    q = q_ref[0]
    k = k_ref[0]
    vb = v_ref[0].astype(BF16)
    if mode == "gla":
        g_hi, g_lo = _split_bf16(gsrc_ref[0], 2)
        logit = (_dot(g_hi, wah_ref[...]) + _dot(g_lo, wah_ref[...]) + _dot(g_hi, wal_ref[...])) + ba_ref[...]
        la = _log_sigmoid(logit) * (1.0 / GLA_TAU)
    else:
        la = jnp.broadcast_to(_log_sigmoid(dec_ref[...]), (tb, GROUP_W))
    bc = _dot_exact_lhs(maskb, la)
    tots = [bc[r.start:r.start + 1, :] if reverse else bc[r.stop - 1:r.stop, :] for r in rows]
    tot_rows = jnp.concatenate([jnp.broadcast_to(t, (CHUNK, GROUP_W)) for t in tots], axis=0)
    e_bc = jnp.exp(bc)
    e_nbc = jnp.exp(-bc)
    e_st = jnp.exp(tot_rows - bc)
    e_tots = [jnp.exp(t) for t in tots]
    if rotate:
        cos2 = cos_ref[...]
        sin2 = sin_ref[...]

    outs = []
    for h in range(HEADS):
        hs = slice(h * HEAD_W, (h + 1) * HEAD_W)
        qh, kh, vh = q[:, hs], k[:, hs], vb[:, hs]
        if rotate:
            qh = qh * cos2 + pltpu.roll(qh, HEAD_W // 2, axis=1) * sin2
            kh = kh * cos2 + pltpu.roll(kh, HEAD_W // 2, axis=1) * sin2
        q_in = (qh * q_scale * e_bc[:, hs]).astype(BF16)
        k_in = (kh * e_nbc[:, hs]).astype(BF16)
        k_st = (kh * e_st[:, hs]).astype(BF16)
        att = jnp.where(mask, _dot_nt(q_in, k_in), 0.0).astype(BF16)
        intra = _dot(att, vh)
        st = st_ref[h]
        inter = [None] * n_chunks
        for cc in order:
            r = rows[cc]
            inter[cc] = _dot_nt(q_in[r], st.astype(BF16))
            st = st * e_tots[cc][:, hs] + _dot_tn(vh[r], k_st[r])
        st_ref[h] = st
        outs.append(intra + jnp.concatenate(inter, axis=0))
    o = jnp.concatenate(outs, axis=-1)
    if accumulate:
        o = o + oprev_ref[0]
    o_ref[0] = o

    @pl.when(i == pl.num_programs(1) - 1)
    def _():
        sfin_ref[0] = st_ref[...]


def gla_scan(p, off_q, off_k, off_v, s0, *, reverse, mode, gate=None, rot=None, o_prev=None, q_scale):
    b, l, _ = p.shape
    tb = _pick(l, (SCAN_BLOCK, 128, 64))
    nblk = l // tb
    blk = (lambda i: nblk - 1 - i) if reverse else (lambda i: i)
    gspec = lambda off: pl.BlockSpec((1, tb, GROUP_W), lambda bi, i: (bi, blk(i), off // GROUP_W))
    const = lambda shape: pl.BlockSpec(shape, lambda bi, i: (0,) * len(shape))
    in_specs = [gspec(off_q), gspec(off_k), gspec(off_v)]
    args = [p, p, p]
    if mode == "gla":
        wa_hi, wa_lo, ba = gate
        in_specs += [pl.BlockSpec((1, tb, HEAD_W), lambda bi, i: (bi, blk(i), OFF_GLA_K // HEAD_W)),
                     const((HEAD_W, GROUP_W)), const((HEAD_W, GROUP_W)), const((1, GROUP_W))]
        args += [p, wa_hi, wa_lo, ba]
    else:
        in_specs += [const((1, GROUP_W))]
        args += [gate]
    if rot is not None:
        in_specs += [pl.BlockSpec((tb, HEAD_W), lambda bi, i: (blk(i), 0))] * 2
        args += [rot[0], rot[1]]
    sspec = pl.BlockSpec((1, HEADS, HEAD_W, HEAD_W), lambda bi, i: (bi, 0, 0, 0))
    in_specs += [sspec]
    args += [s0]
    o_spec = pl.BlockSpec((1, tb, GROUP_W), lambda bi, i: (bi, blk(i), 0))
    aliases = {}
    if o_prev is not None:
        in_specs += [o_spec]
        aliases = {len(args): 0}
        args += [o_prev]
    kern = functools.partial(_gla_scan_kernel, reverse=reverse, mode=mode, rotate=rot is not None,
                             accumulate=o_prev is not None, n_chunks=tb // CHUNK, q_scale=q_scale)
    return pl.pallas_call(
        kern,
        grid=(b, nblk),
        in_specs=in_specs,
        out_specs=[o_spec, sspec],
        out_shape=[jax.ShapeDtypeStruct((b, l, GROUP_W), F32),
                   jax.ShapeDtypeStruct((b, HEADS, HEAD_W, HEAD_W), F32)],
        scratch_shapes=[pltpu.VMEM((HEADS, HEAD_W, HEAD_W), F32)],
        input_output_aliases=aliases,
        compiler_params=_cparams("parallel", "arbitrary"),
        name=f"{mode}_scan_{'bwd' if reverse else 'fwd'}",
    )(*args)


def _mlstm_kernel(*refs, reverse, accumulate, n_chunks, k_scale):
    it = iter(refs)
    q_ref, k_ref, v_ref, gsrc_ref, bias_ref = (next(it) for _ in range(5))
    c0_ref, n0_ref, m0_ref = next(it), next(it), next(it)
    if accumulate:
        oprev_ref = next(it)
    o_ref, cfin_ref, nfin_ref, mfin_ref = next(it), next(it), next(it), next(it)
    ct_ref, n_ref, m_ref = next(it), next(it), next(it)

    i = pl.program_id(1)

    @pl.when(i == 0)
    def _():
        ct_ref[...] = c0_ref[0]
        n_ref[...] = n0_ref[0]
        m_ref[...] = m0_ref[0]

    tb = n_chunks * CHUNK
    mask = _causal_mask(tb, reverse)
    maskb = mask.astype(BF16)
    order = [(n_chunks - 1 - c) if reverse else c for c in range(n_chunks)]
    rows = [slice(c * CHUNK, (c + 1) * CHUNK) for c in range(n_chunks)]
    lane_i = SMALL_MLG + (8 if reverse else 0)
    lane_f = lane_i + HEADS

    q = q_ref[0]
    k = k_ref[0] * k_scale
    vb = v_ref[0].astype(BF16)
    gates = gsrc_ref[0] + bias_ref[...]
    bc_all = _dot_exact_lhs(maskb, _log_sigmoid(gates))
    rep = lambda a, c: jnp.broadcast_to(a[:, c:c + 1], (a.shape[0], HEAD_W))
    outs = []
    for h in range(HEADS):
        hs = slice(h * HEAD_W, (h + 1) * HEAD_W)
        qh, kh, vh = q[:, hs], k[:, hs], vb[:, hs]
        bch = rep(bc_all, lane_f + h)
        wh = rep(gates, lane_i + h) - bch
        bends = [bch[r.start:r.start + 1, :] if reverse else bch[r.stop - 1:r.stop, :] for r in rows]
        w_row = jnp.broadcast_to(jnp.transpose(wh)[0:1, :], (tb, tb))
        dmat = jnp.where(mask, bch[:, 0:1] + w_row, -jnp.inf)
        rowmax = jnp.max(dmat, axis=-1, keepdims=True)
        qb = qh.astype(BF16)
        s_raw = _dot_nt(qb, kh.astype(BF16)) * jnp.exp(dmat - rowmax)
        sv = _dot(s_raw.astype(BF16), vh)
        s_sum = jnp.sum(s_raw, axis=-1, keepdims=True)
        ct, nv, m_prev = ct_ref[h], n_ref[h], m_ref[h]
        out = [None] * n_chunks
        for cc in order:
            r = rows[cc]
            bend = bends[cc]
            glh = bend + wh[r]
            gmax = jnp.max(glh, axis=0, keepdims=True)
            gk = jnp.exp(glh - gmax) * kh[r]
            inter_log = bch[r] + m_prev
            m_t = jnp.maximum(inter_log, rowmax[r])
            e_intra = jnp.exp(rowmax[r] - m_t)
            inter = jnp.exp(inter_log - m_t)
            num = inter * _dot_nt(qb[r], ct.astype(BF16)) + e_intra * sv[r]
            den = jnp.abs(inter * jnp.sum(qh[r] * nv, axis=-1, keepdims=True) + e_intra * s_sum[r])
            out[cc] = num / jnp.maximum(den, jnp.exp(-m_t))
            m_new = jnp.maximum(bend + m_prev, gmax)
            dec = jnp.exp(bend + m_prev - m_new)
            e_upd = jnp.exp(gmax - m_new)
            ct = dec * ct + e_upd * _dot_tn(vh[r], gk.astype(BF16))
            nv = dec * nv + e_upd * jnp.sum(gk, axis=0, keepdims=True)
            m_prev = m_new
        ct_ref[h] = ct
        n_ref[h] = nv
        m_ref[h] = m_prev
        outs.append(jnp.concatenate(out, axis=0))
    o = jnp.concatenate(outs, axis=-1)
    if accumulate:
        o = o + oprev_ref[0]
    o_ref[0] = o

    @pl.when(i == pl.num_programs(1) - 1)
    def _():
        cfin_ref[0] = ct_ref[...]
        nfin_ref[0] = n_ref[...]
        mfin_ref[0] = m_ref[...]


def mlstm_scan(p, state, gate_bias, *, reverse, o_prev=None):
    b, l, _ = p.shape
    tb = _pick(l, (SCAN_BLOCK, 128, 64))
    nblk = l // tb
    blk = (lambda i: nblk - 1 - i) if reverse else (lambda i: i)
    c0, n0, m0 = state
    gspec = lambda off: pl.BlockSpec((1, tb, GROUP_W), lambda bi, i: (bi, blk(i), off // GROUP_W))
    const = lambda shape: pl.BlockSpec(shape, lambda bi, i: (0,) * len(shape))
    cspec = pl.BlockSpec((1, HEADS, HEAD_W, HEAD_W), lambda bi, i: (bi, 0, 0, 0))
    vspec = pl.BlockSpec((1, HEADS, 1, HEAD_W), lambda bi, i: (bi, 0, 0, 0))
    in_specs = [gspec(OFF_ML_Q), gspec(OFF_ML_K), gspec(OFF_ML_V),
                pl.BlockSpec((1, tb, HEAD_W), lambda bi, i: (bi, blk(i), OFF_GLA_K // HEAD_W)),
                const((1, HEAD_W)), cspec, vspec, vspec]
    args = [p, p, p, p, gate_bias, c0, n0, m0]
    o_spec = pl.BlockSpec((1, tb, GROUP_W), lambda bi, i: (bi, blk(i), 0))
    aliases = {}
    if o_prev is not None:
        in_specs += [o_spec]
        aliases = {len(args): 0}
        args += [o_prev]
    kern = functools.partial(_mlstm_kernel, reverse=reverse, accumulate=o_prev is not None,
                             n_chunks=tb // CHUNK, k_scale=HEAD_W ** -0.5)
    outs = pl.pallas_call(
        kern,
        grid=(b, nblk),
        in_specs=in_specs,
        out_specs=[o_spec, cspec, vspec, vspec],
        out_shape=[jax.ShapeDtypeStruct((b, l, GROUP_W), F32),
                   jax.ShapeDtypeStruct((b, HEADS, HEAD_W, HEAD_W), F32),
                   jax.ShapeDtypeStruct((b, HEADS, 1, HEAD_W), F32),
                   jax.ShapeDtypeStruct((b, HEADS, 1, HEAD_W), F32)],
        scratch_shapes=[pltpu.VMEM((HEADS, HEAD_W, HEAD_W), F32), pltpu.VMEM((HEADS, 1, HEAD_W), F32),
                        pltpu.VMEM((HEADS, 1, HEAD_W), F32)],
        input_output_aliases=aliases,
        compiler_params=_cparams("parallel", "arbitrary"),
        name=f"mlstm_scan_{'bwd' if reverse else 'fwd'}",
    )(*args)
    return outs[0], (outs[1], outs[2], outs[3])


def _hy_pre_kernel(u_ref, up_ref, un_ref, w_ref, b_ref, o_ref, ob_ref, *, rows):
    i = pl.program_id(2)
    u = u_ref[0]
    prev_row = jnp.where(i == 0, 0.0, up_ref[0, 7:8, :])
    next_row = jnp.where(i == pl.num_programs(2) - 1, 0.0, un_ref[0, 0:1, :])
    ridx = lax.broadcasted_iota(jnp.int32, u.shape, 0)
    u_dn = jnp.where(ridx == 0, prev_row, pltpu.roll(u, 1, axis=0))
    u_up = jnp.where(ridx == rows - 1, next_row, pltpu.roll(u, rows - 1, axis=0))
    y = w_ref[0:1, :] * u_dn + w_ref[1:2, :] * u + w_ref[2:3, :] * u_up + b_ref[...]
    o_ref[0] = y
    ob_ref[0] = y.astype(BF16)


def hy_pre(p, conv_w, conv_b):
    b, l, _ = p.shape
    rows = _pick(l, (512, 256, 128, 64))
    nr = l // rows
    r8 = rows // 8
    n8 = l // 8
    wpad = jnp.zeros((8, 3 * HY_W), F32).at[:3].set(conv_w)
    return pl.pallas_call(
        functools.partial(_hy_pre_kernel, rows=rows),
        grid=(b, 3, nr),
        in_specs=[pl.BlockSpec((1, rows, HY_W), lambda bi, j, i: (bi, i, j)),
                  pl.BlockSpec((1, 8, HY_W), lambda bi, j, i: (bi, jnp.maximum(i * r8 - 1, 0), j)),
                  pl.BlockSpec((1, 8, HY_W), lambda bi, j, i: (bi, jnp.minimum((i + 1) * r8, n8 - 1), j)),
                  pl.BlockSpec((8, HY_W), lambda bi, j, i: (0, j)),
                  pl.BlockSpec((1, HY_W), lambda bi, j, i: (0, j))],
        out_specs=[pl.BlockSpec((1, rows, HY_W), lambda bi, j, i: (bi, i, j))] * 2,
        out_shape=[jax.ShapeDtypeStruct((b, l, 3 * HY_W), F32),
                   jax.ShapeDtypeStruct((b, l, 3 * HY_W), BF16)],
        compiler_params=_cparams("parallel", "parallel", "parallel"),
        name="hy_shortconv",
    )(p, p, p, wpad, conv_b.reshape(1, 3 * HY_W))


def _mm_kernel(a_ref, b_ref, o_ref):
    o_ref[...] = _dot(a_ref[...], b_ref[...])


def matmul_bf16(a, bm):
    m, k = a.shape
    n = bm.shape[1]
    tm = _pick(m, (512, 256, 128, 64))
    tn = _pick(n, (512, 256, 128))
    return pl.pallas_call(
        _mm_kernel,
        grid=(n // tn, m // tm),
        in_specs=[pl.BlockSpec((tm, k), lambda j, i: (i, 0)),
                  pl.BlockSpec((k, tn), lambda j, i: (0, j))],
        out_specs=pl.BlockSpec((tm, tn), lambda j, i: (i, j)),
        out_shape=jax.ShapeDtypeStruct((m, n), F32),
        compiler_params=_cparams("parallel", "parallel"),
        name="matmul_bf16",
    )(a, bm)


def _dft_fwd_kernel(c_ref, s_ref, z_ref, hr_ref, hi_ref, yr_ref, yi_ref):
    z = z_ref[0]
    cz = _dot(c_ref[...], z)
    sz = _dot(s_ref[...], z)
    hr = hr_ref[...]
    hi = hi_ref[...]
    yr_ref[0] = (cz * hr + sz * hi).astype(BF16)
    yi_ref[0] = (cz * hi - sz * hr).astype(BF16)


def dft_fwd(cmat, smat, zb, z_col, hr, hi, h_col):
    b, l, _ = zb.shape
    f = cmat.shape[0]
    tf = _pick(f, (512, 256, 128, 64))
    return pl.pallas_call(
        _dft_fwd_kernel,
        grid=(f // tf, b),
        in_specs=[pl.BlockSpec((tf, l), lambda i, bi: (i, 0)),
                  pl.BlockSpec((tf, l), lambda i, bi: (i, 0)),
                  pl.BlockSpec((1, l, HY_W), lambda i, bi: (bi, 0, z_col)),
                  pl.BlockSpec((tf, HY_W), lambda i, bi: (i, h_col)),
                  pl.BlockSpec((tf, HY_W), lambda i, bi: (i, h_col))],
        out_specs=[pl.BlockSpec((1, tf, HY_W), lambda i, bi: (bi, i, 0))] * 2,
        out_shape=[jax.ShapeDtypeStruct((b, f, HY_W), BF16)] * 2,
        compiler_params=_cparams("parallel", "parallel"),
        name="hy_dft_fwd",
    )(cmat, smat, zb, hr, hi)


def _dft_inv_kernel(ct_ref, st_ref, yr_ref, yi_ref, x_ref, zp_ref, skip_ref, o_ref, ob_ref):
    conv = _dot(ct_ref[...], yr_ref[0]) - _dot(st_ref[...], yi_ref[0])
    z = x_ref[0] * (conv + skip_ref[...] * zp_ref[0])
    o_ref[0] = z
    ob_ref[0] = z.astype(BF16)


def dft_inv(ctmat, stmat, yr, yi, x_arr, x_col, zp_arr, zp_col, skip):
    b, f, _ = yr.shape
    l = ctmat.shape[0]
    tt = _pick(l, (512, 256, 128, 64))
    return pl.pallas_call(
        _dft_inv_kernel,
        grid=(l // tt, b),
        in_specs=[pl.BlockSpec((tt, f), lambda i, bi: (i, 0)),
                  pl.BlockSpec((tt, f), lambda i, bi: (i, 0)),
                  pl.BlockSpec((1, f, HY_W), lambda i, bi: (bi, 0, 0)),
                  pl.BlockSpec((1, f, HY_W), lambda i, bi: (bi, 0, 0)),
                  pl.BlockSpec((1, tt, HY_W), lambda i, bi: (bi, i, x_col)),
                  pl.BlockSpec((1, tt, HY_W), lambda i, bi: (bi, i, zp_col)),
                  pl.BlockSpec((1, HY_W), lambda i, bi: (0, 0))],
        out_specs=[pl.BlockSpec((1, tt, HY_W), lambda i, bi: (bi, i, 0))] * 2,
        out_shape=[jax.ShapeDtypeStruct((b, l, HY_W), F32),
                   jax.ShapeDtypeStruct((b, l, HY_W), BF16)],
        compiler_params=_cparams("parallel", "parallel"),
        name="hy_dft_inv",
    )(ctmat, stmat, yr, yi, x_arr, zp_arr, skip.reshape(1, HY_W))


def dft_tables(l):
    r = _pick(l, (64, 32, 16, 8))
    period = 8 * l
    odd = 2 * jnp.arange(l, dtype=jnp.int32) + 1
    s1 = jnp.arange(l // r, dtype=jnp.int32) * (2 * r)
    s0 = 2 * jnp.arange(r, dtype=jnp.int32) + 1
    ang = lambda ph: ph.astype(F32) * (2.0 * math.pi / period)
    a = ang((odd[:, None] * s1[None, :]) % period)
    bb = ang((odd[:, None] * s0[None, :]) % period)
    ca, sa, cb, sb = jnp.cos(a), jnp.sin(a), jnp.cos(bb), jnp.sin(bb)
    cmat = (ca[:, :, None] * cb[:, None, :] - sa[:, :, None] * sb[:, None, :]).reshape(l, l).astype(BF16)
    smat = (sa[:, :, None] * cb[:, None, :] + ca[:, :, None] * sb[:, None, :]).reshape(l, l).astype(BF16)
    phi = ang(odd)[:, None]
    return cmat, smat, jnp.cos(phi), jnp.sin(phi)


def hyena_filter_taps(length, lp):
    pos = jnp.arange(length, dtype=F32)
    t = pos / (length - 1)
    bands = (HY_EMB - 1) // 2
    fr = jnp.linspace(1e-4, bands - 1, bands, dtype=F32)
    ang = (2.0 * math.pi / length) * pos[:, None] * fr[None, :]
    z = jnp.concatenate([t[:, None], jnp.cos(ang), -jnp.sin(ang)], axis=-1)
    mm = functools.partial(jnp.matmul, precision=HIGHEST)
    hdn = jnp.sin(lp['hy_f_freq'][0] * (mm(z, lp['hy_f_w1']) + lp['hy_f_b1']))
    hdn = jnp.sin(lp['hy_f_freq'][1] * (mm(hdn, lp['hy_f_w2']) + lp['hy_f_b2']))
    return mm(hdn, lp['hy_f_w3']) * jnp.exp(-t[:, None] * jnp.abs(lp['hy_decay']))


def hyena_mixer(p, lp, tables):
    b, l, _ = p.shape
    cmat, smat, cphi, sphi = tables
    uc, ucb = hy_pre(p, lp['hy_conv_w'], lp['hy_conv_b'])
    taps = hyena_filter_taps(l, lp).reshape(l, 2, 2, HY_W)
    hf = taps[:, :, 0, :]
    hbs = jnp.concatenate([taps[1:, :, 1, :], jnp.zeros((1, 2, HY_W), F32)], axis=0)
    hsum = (hf + hbs).reshape(l, 2 * HY_W).astype(BF16)
    hdiff = (hf - hbs).reshape(l, 2 * HY_W).astype(BF16)
    hr_s = matmul_bf16(cmat, hsum)
    hi_s = -matmul_bf16(smat, hdiff)
    hr = (cphi * hr_s - sphi * hi_s) * (1.0 / l)
    hi = (sphi * hr_s + cphi * hi_s) * (1.0 / l)
    yr, yi = dft_fwd(cmat, smat, ucb, 0, hr, hi, 0)
    z1, z1b = dft_inv(cmat, smat, yr, yi, uc, 1, uc, 0, lp['hy_skip'][0])
    yr, yi = dft_fwd(cmat, smat, z1b, 0, hr, hi, 1)
    z2, _ = dft_inv(cmat, smat, yr, yi, uc, 2, z1, 0, lp['hy_skip'][1])
    return z2


def _head_rms(y):
    parts = []
    for h in range(HEADS):
        yh = y[:, h * HEAD_W:(h + 1) * HEAD_W]
        parts.append(yh * lax.rsqrt(jnp.mean(yh * yh, axis=-1, keepdims=True) + EPS))
    return jnp.concatenate(parts, axis=-1)


def _outproj_kernel(hy_ref, gla_ref, glag_ref, ml_ref, mlo_ref, ret_ref, retg_ref,
                    gn_ref, w_ref, res_ref, gate_ref, o_ref):
    gn = gn_ref[...]
    y_gla = _head_rms(gla_ref[0]) * gn[0:1, :] * _silu(glag_ref[0])
    y_ml = _head_rms(_sigmoid(mlo_ref[0]) * ml_ref[0]) * gn[1:2, :]
    y_ret = _head_rms(ret_ref[0]) * gn[2:3, :] * _silu(retg_ref[0])
    acc = _dot(hy_ref[0].astype(BF16), w_ref[0:GROUP_W, :])
    acc += _dot(y_gla.astype(BF16), w_ref[GROUP_W:2 * GROUP_W, :])
    acc += _dot(y_ml.astype(BF16), w_ref[2 * GROUP_W:3 * GROUP_W, :])
    acc += _dot(y_ret.astype(BF16), w_ref[3 * GROUP_W:4 * GROUP_W, :])
    o_ref[0] = res_ref[0] + gate_ref[...] * acc


def outproj(hy, o_gla, o_ml, o_ret, p, gains, w_out, res, mod4, row_of_batch, i_gate):
    b, l, d = res.shape
    tm = _pick(l, (256, 128, 64))
    gw = GROUP_W
    ospec = pl.BlockSpec((1, tm, gw), lambda bi, i: (bi, i, 0))
    pspec = lambda off: pl.BlockSpec((1, tm, gw), lambda bi, i: (bi, i, off // gw))
    return pl.pallas_call(
        _outproj_kernel,
        grid=(b, l // tm),
        in_specs=[ospec, ospec, pspec(OFF_GLA_G), ospec, pspec(OFF_ML_O), ospec, pspec(OFF_RET_G),
                  pl.BlockSpec((8, gw), lambda bi, i: (0, 0)),
                  pl.BlockSpec((4 * gw, d), lambda bi, i: (0, 0)),
                  pl.BlockSpec((1, tm, d), lambda bi, i: (bi, i, 0)),
                  pl.BlockSpec((None, None, 1, d), lambda bi, i: (row_of_batch(bi), i_gate, 0, 0))],
        out_specs=pl.BlockSpec((1, tm, d), lambda bi, i: (bi, i, 0)),
        out_shape=jax.ShapeDtypeStruct((b, l, d), F32),
        compiler_params=_cparams("parallel", "parallel"),
        name="outproj",
    )(hy, o_gla, p, o_ml, p, o_ret, p, gains, w_out, res, mod4)


def _moe_pre_kernel(x_ref, g_ref, sc_ref, sh_ref, wr_ref, br_ref, t_ref, lg_ref):
    t = _norm_mod(x_ref[0], g_ref[...], sc_ref[...], sh_ref[...])
    t_ref[0] = t
    lg_ref[0] = _dot_hi(t, wr_ref[...]) + br_ref[...]


def moe_pre(x, g, mod4, row_of_batch, i_sc, i_sh, w_router, b_router):
    b, l, d = x.shape
    tm = _pick(l, (512, 256, 128, 64))
    nr = w_router.shape[1]
    return pl.pallas_call(
        _moe_pre_kernel,
        grid=(b, l // tm),
        in_specs=[pl.BlockSpec((1, tm, d), lambda bi, i: (bi, i, 0)),
                  pl.BlockSpec((1, d), lambda bi, i: (0, 0)),
                  pl.BlockSpec((None, None, 1, d), lambda bi, i: (row_of_batch(bi), i_sc, 0, 0)),
                  pl.BlockSpec((None, None, 1, d), lambda bi, i: (row_of_batch(bi), i_sh, 0, 0)),
                  pl.BlockSpec((d, nr), lambda bi, i: (0, 0)),
                  pl.BlockSpec((1, nr), lambda bi, i: (0, 0))],
        out_specs=[pl.BlockSpec((1, tm, d), lambda bi, i: (bi, i, 0)),
                   pl.BlockSpec((1, tm, nr), lambda bi, i: (bi, i, 0))],
        out_shape=[jax.ShapeDtypeStruct((b, l, d), F32), jax.ShapeDtypeStruct((b, l, nr), F32)],
        compiler_params=_cparams("parallel", "parallel"),
        name="moe_pre",
    )(x, g.reshape(1, d), mod4, mod4, w_router, b_router)


ROUTE_E1, ROUTE_E2, ROUTE_R1, ROUTE_R2, ROUTE_W1, ROUTE_W2 = range(6)


def _moe_route_kernel(lg_ref, route_ref, cnt_ref, run_ref, *, tr):
    i = pl.program_id(0)

    @pl.when(i == 0)
    def _():
        run_ref[...] = jnp.zeros_like(run_ref)

    lg = lg_ref[...]
    lane = lax.broadcasted_iota(jnp.int32, lg.shape, 1).astype(F32)
    no_lane = float(HEAD_W)
    neg = -jnp.inf
    first_max = lambda vals, vmax: jnp.min(jnp.where(vals == vmax, lane, no_lane), axis=-1, keepdims=True)
    gl = jnp.where(lane < MOE_GROUPS, lg, neg)
    gmax = jnp.max(gl, axis=-1, keepdims=True)
    pg = 1.0 / jnp.sum(jnp.exp(gl - gmax), axis=-1, keepdims=True)
    lo = MOE_GROUPS + first_max(gl, gmax) * MOE_EPG
    sel = jnp.where(jnp.logical_and(lane >= lo, lane < lo + MOE_EPG), lg, neg)
    v1 = jnp.max(sel, axis=-1, keepdims=True)
    i1 = first_max(sel, v1)
    sel2 = jnp.where(lane == i1, neg, sel)
    v2 = jnp.max(sel2, axis=-1, keepdims=True)
    i2 = first_max(sel2, v2)
    e21 = jnp.exp(v2 - v1)
    w1 = pg / (1.0 + e21)
    w2 = w1 * e21
    e1 = i1 - MOE_GROUPS
    e2 = i2 - MOE_GROUPS
    oh1 = (lane == e1).astype(F32)
    oh2 = (lane == e2).astype(F32)
    both = oh1 + oh2
    row = lax.broadcasted_iota(jnp.int32, (tr, tr), 0)
    col = lax.broadcasted_iota(jnp.int32, (tr, tr), 1)
    earlier = (col < row).astype(BF16)
    before = _dot(earlier, both.astype(BF16)) + run_ref[...]
    r1 = jnp.sum(before * oh1, axis=-1, keepdims=True)
    r2 = jnp.sum(before * oh2, axis=-1, keepdims=True)
    run_ref[...] += jnp.sum(both, axis=0, keepdims=True)
    rec = jnp.zeros_like(lg)
    for k, val in ((ROUTE_E1, e1), (ROUTE_E2, e2), (ROUTE_R1, r1), (ROUTE_R2, r2), (ROUTE_W1, w1), (ROUTE_W2, w2)):
        rec = jnp.where(lane == k, val, rec)
    route_ref[...] = rec

    @pl.when(i == pl.num_programs(0) - 1)
    def _():
        cnt_ref[...] = run_ref[...]


def moe_route(logits):
    t, nl = logits.shape
    tr = _pick(t, (512, 256, 128, 64))
    return pl.pallas_call(
        functools.partial(_moe_route_kernel, tr=tr),
        grid=(t // tr,),
        in_specs=[pl.BlockSpec((tr, nl), lambda i: (i, 0))],
        out_specs=[pl.BlockSpec((tr, nl), lambda i: (i, 0)), pl.BlockSpec((1, nl), lambda i: (0, 0))],
        out_shape=[jax.ShapeDtypeStruct((t, nl), F32), jax.ShapeDtypeStruct((1, nl), F32)],
        scratch_shapes=[pltpu.VMEM((1, nl), F32)],
        compiler_params=_cparams("arbitrary"),
        name="moe_route",
    )(logits)


def _row_copy_wait(src_rows, dst_rows, sem):
    pltpu.make_async_copy(src_rows, dst_rows, sem).wait()


def _moe_dispatch_kernel(pend_ref, padded_ref, pos_ref, tok_ref, xs_ref, zero_ref, sem, *, tr, tm):
    @pl.when(pl.program_id(0) == 0)
    def _():
        zero_ref[...] = jnp.zeros_like(zero_ref)

        def clear(e, carry):
            @pl.when(padded_ref[e] > 0)
            def _():
                dst = xs_ref.at[pl.ds(pl.multiple_of(pend_ref[e] - tm, tm), tm)]
                cp = pltpu.make_async_copy(zero_ref, dst, sem)
                cp.start()
                cp.wait()
            return carry

        lax.fori_loop(0, N_EXPERTS, clear, 0)

    def issue(r, carry):
        for slot in range(2):
            dst = pos_ref[0, 0, 2 * r + slot]
            pltpu.make_async_copy(tok_ref.at[pl.ds(r, 1)], xs_ref.at[pl.ds(dst, 1)], sem).start()
        return carry

    lax.fori_loop(0, tr, issue, 0)
    for _ in range(2):
        _row_copy_wait(tok_ref, xs_ref.at[pl.ds(0, tr)], sem)


def moe_dispatch(tok, pos, pend, padded, pmax, tm):
    t, d = tok.shape
    tr = pos.shape[2] // 2
    grid_spec = pltpu.PrefetchScalarGridSpec(
        num_scalar_prefetch=2,
        grid=(t // tr,),
        in_specs=[pl.BlockSpec((1, 1, 2 * tr), lambda i, pe, pa: (i, 0, 0), memory_space=pltpu.SMEM),
                  pl.BlockSpec((tr, d), lambda i, pe, pa: (i, 0))],
        out_specs=pl.BlockSpec(memory_space=pl.ANY),
        scratch_shapes=[pltpu.VMEM((tm, d), F32), pltpu.SemaphoreType.DMA],
    )
    return pl.pallas_call(
        functools.partial(_moe_dispatch_kernel, tr=tr, tm=tm),
        grid_spec=grid_spec,
        out_shape=jax.ShapeDtypeStruct((pmax, d), F32),
        compiler_params=_cparams("arbitrary"),
        name="moe_dispatch",
    )(pend, padded, pos, tok)


def _moe_expert_kernel(te_ref, nu_ref, x_ref, w1_ref, w3_ref, w2_ref, o_ref, w1b_ref, w3b_ref, w2b_ref):
    t = pl.program_id(0)

    @pl.when(t < nu_ref[0])
    def _():
        @pl.when(jnp.logical_or(t == 0, te_ref[t] != te_ref[jnp.maximum(t - 1, 0)]))
        def _():
            w1b_ref[...] = w1_ref[0].astype(BF16)
            w3b_ref[...] = w3_ref[0].astype(BF16)
            w2b_ref[...] = w2_ref[0].astype(BF16)

        x = x_ref[...].astype(BF16)
        act = _silu(_dot(x, w1b_ref[...])) * _dot(x, w3b_ref[...])
        o_ref[...] = _dot(act.astype(BF16), w2b_ref[...])


def moe_experts(x_sorted, w1, w3, w2, tile_expert, n_used, tm):
    pmax, d = x_sorted.shape
    ff = w1.shape[2]
    used = lambda t, nu: jnp.minimum(t, nu[0] - 1)
    grid_spec = pltpu.PrefetchScalarGridSpec(
        num_scalar_prefetch=2,
        grid=(pmax // tm,),
        in_specs=[pl.BlockSpec((tm, d), lambda t, te, nu: (used(t, nu), 0)),
                  pl.BlockSpec((1, d, ff), lambda t, te, nu: (te[t], 0, 0)),
                  pl.BlockSpec((1, d, ff), lambda t, te, nu: (te[t], 0, 0)),
                  pl.BlockSpec((1, ff, d), lambda t, te, nu: (te[t], 0, 0))],
        out_specs=pl.BlockSpec((tm, d), lambda t, te, nu: (used(t, nu), 0)),
        scratch_shapes=[pltpu.VMEM((d, ff), BF16), pltpu.VMEM((d, ff), BF16), pltpu.VMEM((ff, d), BF16)],
    )
    return pl.pallas_call(
        _moe_expert_kernel,
        grid_spec=grid_spec,
        out_shape=jax.ShapeDtypeStruct((pmax, d), F32),
        compiler_params=_cparams("arbitrary"),
        name="moe_experts",
    )(tile_expert, n_used, x_sorted, w1, w3, w2)


def _moe_combine_kernel(p1_ref, p2_ref, p1n_ref, p2n_ref, ys_ref, res_ref, route_ref, gate_ref, o_ref,
                        y1_ref, y2_ref, sem, *, tc):
    i = pl.program_id(0)
    slot = i % 2

    def issue(pa_ref, pb_ref, s):
        def body(r, carry):
            pltpu.make_async_copy(ys_ref.at[pl.ds(pa_ref[0, 0, r], 1)], y1_ref.at[s, pl.ds(r, 1)], sem.at[s]).start()
            pltpu.make_async_copy(ys_ref.at[pl.ds(pb_ref[0, 0, r], 1)], y2_ref.at[s, pl.ds(r, 1)], sem.at[s]).start()
            return carry
        lax.fori_loop(0, tc, body, 0)

    @pl.when(i == 0)
    def _():
        issue(p1_ref, p2_ref, 0)

    @pl.when(i + 1 < pl.num_programs(0))
    def _():
        issue(p1n_ref, p2n_ref, 1 - slot)

    _row_copy_wait(ys_ref.at[pl.ds(0, tc)], y1_ref.at[slot], sem.at[slot])
    _row_copy_wait(ys_ref.at[pl.ds(0, tc)], y2_ref.at[slot], sem.at[slot])
    rec = route_ref[...]
    w1 = rec[:, ROUTE_W1:ROUTE_W1 + 1]
    w2 = rec[:, ROUTE_W2:ROUTE_W2 + 1]
    o_ref[...] = res_ref[...] + gate_ref[...] * (w1 * y1_ref[slot] + w2 * y2_ref[slot])


def moe_combine(res, y_sorted, pos1, pos2, route, mod4, row_of_tile, i_gate):
    t, d = res.shape
    n, _, tc = pos1.shape
    cur = lambda i: (i, 0, 0)
    nxt = lambda i: (jnp.minimum(i + 1, n - 1), 0, 0)
    smem = lambda im: pl.BlockSpec((1, 1, tc), im, memory_space=pltpu.SMEM)
    return pl.pallas_call(
        functools.partial(_moe_combine_kernel, tc=tc),
        grid=(n,),
        in_specs=[smem(cur), smem(cur), smem(nxt), smem(nxt),
                  pl.BlockSpec(memory_space=pl.ANY),
                  pl.BlockSpec((tc, d), lambda i: (i, 0)),
                  pl.BlockSpec((tc, route.shape[1]), lambda i: (i, 0)),
                  pl.BlockSpec((None, None, 1, d), lambda i: (row_of_tile(i), i_gate, 0, 0))],
        out_specs=pl.BlockSpec((tc, d), lambda i: (i, 0)),
        out_shape=jax.ShapeDtypeStruct((t, d), F32),
        scratch_shapes=[pltpu.VMEM((2, tc, d), F32), pltpu.VMEM((2, tc, d), F32), pltpu.SemaphoreType.DMA((2,))],
        compiler_params=_cparams("arbitrary"),
        name="moe_combine",
    )(pos1, pos2, pos1, pos2, y_sorted, res, route, mod4)


def hier_moe(x, lp, mod4, row_of_batch):
    b, l, d = x.shape
    t = b * l
    tm = 256 if t >= 4096 else 64
    tok, logits = moe_pre(x, lp['norm2_g'], mod4, row_of_batch, 4, 3, lp['w_router'], lp['b_router'])
    tok = tok.reshape(t, d)
    route, cnt = moe_route(logits.reshape(t, -1))
    counts = cnt[0, :N_EXPERTS].astype(jnp.int32)
    padded = ((counts + tm - 1) // tm) * tm
    pend = jnp.cumsum(padded)
    pstart = pend - padded
    pmax = 2 * t + N_EXPERTS * tm
    tile_row = jnp.arange(pmax // tm, dtype=jnp.int32) * tm
    tile_expert = jnp.minimum(jnp.sum(tile_row[:, None] >= pend[None, :], axis=1), N_EXPERTS - 1).astype(jnp.int32)
    n_used = (pend[-1] // tm).astype(jnp.int32).reshape(1)
    ids = route[:, :4].astype(jnp.int32)
    pos1 = jnp.take(pstart, ids[:, ROUTE_E1], mode="clip") + ids[:, ROUTE_R1]
    pos2 = jnp.take(pstart, ids[:, ROUTE_E2], mode="clip") + ids[:, ROUTE_R2]
    tr = _pick(t, (512, 256, 128, 64))
    tc = _pick(l, (256, 128, 64))
    pos = jnp.stack([pos1, pos2], axis=-1).reshape(t // tr, 1, 2 * tr)
    x_sorted = moe_dispatch(tok, pos, pend, padded, pmax, tm)
    y_sorted = moe_experts(x_sorted, lp['moe_w1e'], lp['moe_w3e'], lp['moe_w2e'],
                           tile_expert + lp['expert_base'], n_used, tm)
    tiles_per_batch = l // tc
    out = moe_combine(x.reshape(t, d), y_sorted, pos1.reshape(t // tc, 1, tc), pos2.reshape(t // tc, 1, tc),
                      route, mod4, lambda i: row_of_batch(i // tiles_per_batch), 5)
    return out.reshape(b, l, d)


def _final_norm_kernel(x_ref, g_ref, o_ref):
    x = x_ref[0]
    o_ref[0] = x * lax.rsqrt(jnp.mean(x * x, axis=-1, keepdims=True) + EPS) * g_ref[...]


def final_norm(x, g):
    b, l, d = x.shape
    tm = _pick(l, (512, 256, 128, 64))
    spec = pl.BlockSpec((1, tm, d), lambda bi, i: (bi, i, 0))
    return pl.pallas_call(
        _final_norm_kernel,
        grid=(b, l // tm),
        in_specs=[spec, pl.BlockSpec((1, d), lambda bi, i: (0, 0))],
        out_specs=spec,
        out_shape=jax.ShapeDtypeStruct((b, l, d), F32),
        compiler_params=_cparams("parallel", "parallel"),
        name="final_norm",
    )(x, g.reshape(1, d))


def _pad_heads(w, dk):
    d = w.shape[0]
    return jnp.pad(w.reshape(d, HEADS, dk), ((0, 0), (0, 0), (0, HEAD_W - dk))).reshape(d, HEADS * HEAD_W)


def prep_w_in(w_in):
    o = 0
    take = lambda n: (w_in[:, o:o + n], o + n)
    hy, o = take(3 * HY_W)
    gq, o = take(HEADS * GLA_DK)
    gk, o = take(HEADS * GLA_DK)
    gv, o = take(GROUP_W)
    gg, o = take(GROUP_W)
    gr, o = take(2 * GLA_RANK)
    ml, o = take(4 * GROUP_W)
    mg, o = take(16)
    ret, o = take(4 * GROUP_W)
    gk_p = _pad_heads(gk, GLA_DK)
    gk_p = gk_p.at[:, SMALL_R:SMALL_R + 2 * GLA_RANK].set(gr).at[:, SMALL_MLG:SMALL_MLG + 16].set(mg)
    return jnp.concatenate([hy, _pad_heads(gq, GLA_DK), gk_p, gv, gg, ml, ret], axis=1).astype(BF16)


def prep_layer(l, prm):
    lp = {k: v[l] for k, v in prm.items()}
    lp['w_in_p'] = prep_w_in(lp['w_in'])
    lp['w_out_b'] = lp['w_out'].astype(BF16)
    gates = []
    for d in range(2):
        w = jnp.zeros((HEAD_W, HEADS * HEAD_W), F32)
        w = w.at[SMALL_R + d * GLA_RANK:SMALL_R + (d + 1) * GLA_RANK].set(_pad_heads(lp['gla_wa2'][d], GLA_DK))
        w_hi = w.astype(BF16)
        w_lo = (w - w_hi.astype(F32)).astype(BF16)
        gates.append((w_hi, w_lo, _pad_heads(lp['gla_ba'][d][None, :], GLA_DK)))
    lp['gla_gate'] = gates
    bias = jnp.zeros((1, HEAD_W), F32).at[0, SMALL_MLG:SMALL_MLG + 16].set(lp['ml_gate_b'].reshape(-1))
    lp['ml_bias'] = bias
    lp['ret_dec'] = [jnp.repeat(lp['ret_decay'][d], HEAD_W)[None, :] for d in range(2)]
    gains = jnp.zeros((8, GROUP_W), F32)
    lp['gains'] = gains.at[0].set(lp['gla_norm_g']).at[1].set(lp['ml_norm_g']).at[2].set(lp['ret_norm_g'])
    d_model = lp['moe_wg'].shape[0]
    wr = jnp.zeros((d_model, HEAD_W), F32)
    lp['w_router'] = wr.at[:, :MOE_GROUPS].set(lp['moe_wg']).at[:, MOE_GROUPS:MOE_GROUPS + N_EXPERTS].set(lp['moe_we'])
    br = jnp.zeros((1, HEAD_W), F32)
    lp['b_router'] = br.at[0, :MOE_GROUPS].set(lp['moe_bg']).at[0, MOE_GROUPS:MOE_GROUPS + N_EXPERTS].set(lp['moe_be'])
    lp['moe_w1e'] = prm['moe_w1'].reshape(-1, d_model, MOE_FF)
    lp['moe_w3e'] = prm['moe_w3'].reshape(-1, d_model, MOE_FF)
    lp['moe_w2e'] = prm['moe_w2'].reshape(-1, MOE_FF, d_model)
    lp['expert_base'] = l * N_EXPERTS
    return lp


def rotary_tables(length):
    rows = length // GRID_W
    row = jnp.repeat(jnp.arange(rows, dtype=F32), GRID_W)
    col = jnp.tile(jnp.arange(GRID_W, dtype=F32), rows)
    nf = HEAD_W // 4
    inv = ROPE_BASE ** (-jnp.arange(nf, dtype=F32) / nf)
    ang = jnp.concatenate([row[:, None] * inv, col[:, None] * inv], axis=-1)
    cos, sin = jnp.cos(ang), jnp.sin(ang)
    return jnp.concatenate([cos, cos], axis=-1), jnp.concatenate([-sin, sin], axis=-1)


def bidir_scans(p_ctx, p_lat, lp, rot):
    b = p_lat.shape[0]
    zeros_s = jnp.zeros((b, HEADS, HEAD_W, HEAD_W), F32)
    zeros_v = jnp.zeros((b, HEADS, 1, HEAD_W), F32)
    out = {}
    for name, mode, offs, scale in (("gla", "gla", (OFF_GLA_Q, OFF_GLA_K, OFF_GLA_V), GLA_DK ** -0.5),
                                    ("ret", "ret", (OFF_RET_Q, OFF_RET_K, OFF_RET_V), HEAD_W ** -0.5)):
        o_c, o_l = None, None
        for d in range(2):
            gate = lp['gla_gate'][d] if mode == "gla" else lp['ret_dec'][d]
            o_c, st = gla_scan(p_ctx, *offs, zeros_s, reverse=d == 1, mode=mode, gate=gate,
                               o_prev=o_c, q_scale=scale)
            o_l, _ = gla_scan(p_lat, *offs, st, reverse=d == 1, mode=mode, gate=gate,
                              rot=rot if mode == "ret" else None, o_prev=o_l, q_scale=scale)
        out[name] = (o_c, o_l)
    o_c, o_l = None, None
    for d in range(2):
        o_c, st = mlstm_scan(p_ctx, (zeros_s, zeros_v, zeros_v), lp['ml_bias'], reverse=d == 1, o_prev=o_c)
        o_l, _ = mlstm_scan(p_lat, st, lp['ml_bias'], reverse=d == 1, o_prev=o_l)
    out["ml"] = (o_c, o_l)
    return out


def kernel(x, c, ctx, c_ctx, ada_w, ada_b, norm1_g, norm2_g, w_in, hy_conv_w, hy_conv_b, hy_f_w1, hy_f_b1, hy_f_w2, hy_f_b2, hy_f_freq, hy_f_w3, hy_decay, hy_skip, gla_wa2, gla_ba, gla_norm_g, ml_gate_b, ml_norm_g, ret_decay, ret_norm_g, w_out, moe_wg, moe_bg, moe_we, moe_be, moe_w1, moe_w3, moe_w2, final_g):
    prm = dict(norm1_g=norm1_g, norm2_g=norm2_g, w_in=w_in, hy_conv_w=hy_conv_w, hy_conv_b=hy_conv_b,
               hy_f_w1=hy_f_w1, hy_f_b1=hy_f_b1, hy_f_w2=hy_f_w2, hy_f_b2=hy_f_b2, hy_f_freq=hy_f_freq,
               hy_f_w3=hy_f_w3, hy_decay=hy_decay, hy_skip=hy_skip, gla_wa2=gla_wa2, gla_ba=gla_ba,
               gla_norm_g=gla_norm_g, ml_gate_b=ml_gate_b, ml_norm_g=ml_norm_g, ret_decay=ret_decay,
               ret_norm_g=ret_norm_g, w_out=w_out, moe_wg=moe_wg, moe_bg=moe_bg, moe_we=moe_we,
               moe_be=moe_be, moe_w1=moe_w1, moe_w3=moe_w3, moe_w2=moe_w2)
    depth = ada_w.shape[0]
    b, seq, d = x.shape
    lc = ctx.shape[1]
    lat = x.astype(F32)
    cx = ctx.astype(F32)
    cvec = jnp.zeros((8, d), F32).at[:b].set(c.astype(F32)).at[b].set(c_ctx.astype(F32))
    mod_all = adaln(cvec, ada_w, ada_b)
    lat_row = lambda bi: bi
    ctx_row = lambda bi: b
    rot = rotary_tables(seq)
    tab_lat = dft_tables(seq)
    tab_ctx = dft_tables(lc)
    for l in range(depth):
        with_ctx = l < depth - 1
        lp = prep_layer(l, prm)
        mod4 = mod_all[l].reshape(8, 6, 1, d)
        p_lat = inproj(lat, lp['norm1_g'], mod4, lat_row, 1, 0, lp['w_in_p'])
        p_ctx = inproj(cx, lp['norm1_g'], mod4, ctx_row, 1, 0, lp['w_in_p'])
        scans = bidir_scans(p_ctx, p_lat, lp, rot)
        hy_lat = hyena_mixer(p_lat, lp, tab_lat)
        lat = outproj(hy_lat, scans["gla"][1], scans["ml"][1], scans["ret"][1], p_lat, lp['gains'],
                      lp['w_out_b'], lat, mod4, lat_row, 2)
        lat = hier_moe(lat, lp, mod4, lat_row)
        if with_ctx:
            hy_ctx = hyena_mixer(p_ctx, lp, tab_ctx)
            cx = outproj(hy_ctx, scans["gla"][0], scans["ml"][0], scans["ret"][0], p_ctx, lp['gains'],
                         lp['w_out_b'], cx, mod4, ctx_row, 2)
            cx = hier_moe(cx, lp, mod4, ctx_row)
    return final_norm(lat, final_g).astype(x.dtype)
```

```python
import functools
import math

import jax
import jax.numpy as jnp
from jax import lax
from jax.experimental import pallas as pl
from jax.experimental.pallas import tpu as pltpu

F32 = jnp.float32
BF16 = jnp.bfloat16
HIGHEST = lax.Precision.HIGHEST

EPS = 1e-6
CHUNK = 64
SCAN_BLOCK = 256
HALO = 16
GRID_W = 64
ROPE_BASE = 10000.0
HEADS = 4
HEAD_W = 128
HY_W = 512
HY_EMB = 33
GLA_DK = 64
GLA_RANK = 16
GLA_TAU = 16.0
MOE_GROUPS = 4
MOE_EPG = 8
MOE_FF = 256
N_EXPERTS = MOE_GROUPS * MOE_EPG

GROUP_W = 512
OFF_HY = 0
OFF_GLA_Q = 1536
OFF_GLA_K = OFF_GLA_Q + GROUP_W
OFF_GLA_V = OFF_GLA_K + GROUP_W
OFF_GLA_G = OFF_GLA_V + GROUP_W
OFF_ML_Q = OFF_GLA_G + GROUP_W
OFF_ML_K = OFF_ML_Q + GROUP_W
OFF_ML_V = OFF_ML_K + GROUP_W
OFF_ML_O = OFF_ML_V + GROUP_W
OFF_RET_Q = OFF_ML_O + GROUP_W
OFF_RET_K = OFF_RET_Q + GROUP_W
OFF_RET_V = OFF_RET_K + GROUP_W
OFF_RET_G = OFF_RET_V + GROUP_W
N_PROJ = OFF_RET_G + GROUP_W
SMALL_BASE = GLA_DK
SMALL_R = SMALL_BASE
SMALL_MLG = SMALL_BASE + 2 * GLA_RANK

VMEM_LIMIT_BYTES = 56 * 1024 * 1024


def _cparams(*sem):
    return pltpu.CompilerParams(dimension_semantics=sem, vmem_limit_bytes=VMEM_LIMIT_BYTES)


def _pick(n, candidates):
    for c in candidates:
        if n % c == 0:
            return c
    return n


def _silu(x):
    return x / (1.0 + jnp.exp(-x))


def _sigmoid(x):
    return 1.0 / (1.0 + jnp.exp(-x))


def _log_sigmoid(x):
    return jnp.minimum(x, 0.0) - jnp.log(1.0 + jnp.exp(-jnp.abs(x)))


def _dot(a, b):
    return jnp.dot(a, b, preferred_element_type=F32)


def _dot_hi(a, b):
    return jnp.dot(a, b, preferred_element_type=F32, precision=HIGHEST)


def _split_bf16(x, parts):
    out = []
    for _ in range(parts - 1):
        hi = x.astype(BF16)
        out.append(hi)
        x = x - hi.astype(F32)
    out.append(x.astype(BF16))
    return out


def _dot_exact_lhs(a_bf16, x):
    return sum(_dot(a_bf16, p) for p in _split_bf16(x, 3))


def _pack_pairs(x):
    h = x.shape[1] // 2
    bits = lambda a: lax.bitcast_convert_type(a.astype(BF16).astype(F32), jnp.uint32)
    return bits(x[:, :h]) | (bits(x[:, h:]) >> 16)


def _unpack_pairs(w):
    hi = lax.bitcast_convert_type(w & jnp.uint32(0xFFFF0000), F32)
    lo = lax.bitcast_convert_type(w << 16, F32)
    return jnp.concatenate([hi, lo], axis=-1)


def _dot_nt(a, b):
    return lax.dot_general(a, b, (((1,), (1,)), ((), ())), preferred_element_type=F32)


def _dot_tn(a, b):
    return lax.dot_general(a, b, (((0,), (0,)), ((), ())), preferred_element_type=F32)


def _adaln_kernel(c_ref, w_ref, b_ref, o_ref):
    o_ref[0] = _dot_hi(_silu(c_ref[...]), w_ref[0]) + b_ref[0]


def adaln(cvec, ada_w, ada_b):
    depth, d, n = ada_w.shape
    tn = _pick(n, (1024, 512, 256, 128))
    return pl.pallas_call(
        _adaln_kernel,
        grid=(depth, n // tn),
        in_specs=[pl.BlockSpec((8, d), lambda l, j: (0, 0)),
                  pl.BlockSpec((1, d, tn), lambda l, j: (l, 0, j)),
                  pl.BlockSpec((1, 1, tn), lambda l, j: (l, 0, j))],
        out_specs=pl.BlockSpec((1, 8, tn), lambda l, j: (l, 0, j)),
        out_shape=jax.ShapeDtypeStruct((depth, 8, n), F32),
        compiler_params=_cparams("parallel", "parallel"),
        name="adaln",
    )(cvec, ada_w, ada_b.reshape(depth, 1, n))


def _norm_mod(x, g, sc, sh):
    ms = jnp.mean(x * x, axis=-1, keepdims=True)
    return (x * lax.rsqrt(ms + EPS) * g) * (1.0 + sc) + sh


def _inproj_kernel(x_ref, g_ref, sc_ref, sh_ref, w_ref, o_ref, gate_ref, xn_ref, *, gate_tile, gate_off):
    j = pl.program_id(2)

    @pl.when(j == 0)
    def _():
        xn_ref[...] = _norm_mod(x_ref[0], g_ref[...], sc_ref[...], sh_ref[...]).astype(BF16)

    acc = _dot(xn_ref[...], w_ref[...])
    o_ref[0] = acc.astype(BF16)

    @pl.when(j == gate_tile)
    def _():
        gate_ref[0] = acc[:, gate_off:gate_off + HEAD_W]


def inproj(x, g, mod4, row_of_batch, i_sc, i_sh, w):
    b, l, d = x.shape
    n = w.shape[1]
    tm = _pick(l, (1024, 512, 256, 128, 64))
    tn = _pick(n, (768, 512, 256, 128))
    kern = functools.partial(_inproj_kernel, gate_tile=OFF_GLA_K // tn, gate_off=OFF_GLA_K % tn)
    return pl.pallas_call(
        kern,
        grid=(b, l // tm, n // tn),
        in_specs=[pl.BlockSpec((1, tm, d), lambda bi, i, j: (bi, i, 0)),
                  pl.BlockSpec((1, d), lambda bi, i, j: (0, 0)),
                  pl.BlockSpec((None, None, 1, d), lambda bi, i, j: (row_of_batch(bi), i_sc, 0, 0)),
                  pl.BlockSpec((None, None, 1, d), lambda bi, i, j: (row_of_batch(bi), i_sh, 0, 0)),
                  pl.BlockSpec((d, tn), lambda bi, i, j: (0, j))],
        out_specs=[pl.BlockSpec((1, tm, tn), lambda bi, i, j: (bi, i, j)),
                   pl.BlockSpec((1, tm, HEAD_W), lambda bi, i, j: (bi, i, 0))],
        out_shape=[jax.ShapeDtypeStruct((b, l, n), BF16), jax.ShapeDtypeStruct((b, l, HEAD_W), F32)],
        scratch_shapes=[pltpu.VMEM((tm, d), BF16)],
        compiler_params=_cparams("parallel", "parallel", "arbitrary"),
        name="inproj",
    )(x, g.reshape(1, d), mod4, mod4, w)


def _causal_mask(n, reverse):
    r = lax.broadcasted_iota(jnp.int32, (n, n), 0)
    c = lax.broadcasted_iota(jnp.int32, (n, n), 1)
    same_chunk = (r // CHUNK) == (c // CHUNK)
    return jnp.logical_and(same_chunk, (c >= r) if reverse else (c <= r))


def _gla_scan_kernel(*refs, reverse, mode, rotate, accumulate, n_chunks, q_scale):
    it = iter(refs)
    q_ref, k_ref, v_ref = next(it), next(it), next(it)
    if mode == "gla":
        gsrc_ref, wah_ref, wal_ref, ba_ref = next(it), next(it), next(it), next(it)
    else:
        dec_ref = next(it)
    if rotate:
        cos_ref, sin_ref = next(it), next(it)
    s0_ref = next(it)
    if accumulate:
        oprev_ref = next(it)
    o_ref, sfin_ref, st_ref = next(it), next(it), next(it)
    dterm_refs = list(it)

    i = pl.program_id(1)

    @pl.when(i == 0)
    def _():
        st_ref[...] = s0_ref[0]

    tb = n_chunks * CHUNK
    mask = _causal_mask(tb, reverse)
    maskb = mask.astype(BF16)
    order = [(n_chunks - 1 - c) if reverse else c for c in range(n_chunks)]
    rows = [slice(c * CHUNK, (c + 1) * CHUNK) for c in range(n_chunks)]

    q = q_ref[0].astype(F32)
    k = k_ref[0].astype(F32)
    vb = v_ref[0]

    def decay_terms(la):
        bc = _dot_exact_lhs(maskb, la)
        tots = [bc[r.start:r.start + 1, :] if reverse else bc[r.stop - 1:r.stop, :] for r in rows]
        tot_rows = jnp.concatenate([jnp.broadcast_to(t, (CHUNK, GROUP_W)) for t in tots], axis=0)
        tot_pad = tots + [jnp.zeros_like(tots[0])] * (-n_chunks % 8)
        return jnp.exp(bc), jnp.exp(-bc), jnp.exp(tot_rows - bc), jnp.exp(jnp.concatenate(tot_pad, axis=0))

    if mode == "gla":
        g_hi, g_lo = _split_bf16(gsrc_ref[0], 2)
        logit = (_dot(g_hi, wah_ref[...]) + _dot(g_lo, wah_ref[...]) + _dot(g_hi, wal_ref[...])) + ba_ref[...]
        e_bc, e_nbc, e_st, e_tot = decay_terms(_log_sigmoid(logit) * (1.0 / GLA_TAU))
    else:
        @pl.when(i == 0)
        def _():
            terms = decay_terms(jnp.broadcast_to(_log_sigmoid(dec_ref[...]), (tb, GROUP_W)))
            for t_ref, t in zip(dterm_refs, terms):
                t_ref[...] = t

        e_bc, e_nbc, e_st, e_tot = (t_ref[...] for t_ref in dterm_refs)
    e_tots = [e_tot[c:c + 1, :] for c in range(n_chunks)]
    if rotate:
        cos2 = cos_ref[...]
        sin2 = sin_ref[...]

    outs = []
    for h in range(HEADS):
        hs = slice(h * HEAD_W, (h + 1) * HEAD_W)
        qh, kh, vh = q[:, hs], k[:, hs], vb[:, hs]
        if rotate:
            qh = qh * cos2 + pltpu.roll(qh, HEAD_W // 2, axis=1) * sin2
            kh = kh * cos2 + pltpu.roll(kh, HEAD_W // 2, axis=1) * sin2
        q_in = (qh * q_scale * e_bc[:, hs]).astype(BF16)
        k_in = (kh * e_nbc[:, hs]).astype(BF16)
        k_st = (kh * e_st[:, hs]).astype(BF16)
        att = jnp.where(mask, _dot_nt(q_in, k_in), 0.0).astype(BF16)
        intra = _dot(att, vh)
        st = st_ref[h]
        inter = [None] * n_chunks
        for cc in order:
            r = rows[cc]
            inter[cc] = _dot_nt(q_in[r], st.astype(BF16))
            st = st * e_tots[cc][:, hs] + _dot_tn(vh[r], k_st[r])
        st_ref[h] = st
        outs.append(intra + jnp.concatenate(inter, axis=0))
    o = jnp.concatenate(outs, axis=-1)
    if accumulate:
        o = o + oprev_ref[0]
    o_ref[0] = o

    @pl.when(i == pl.num_programs(1) - 1)
    def _():
        sfin_ref[0] = st_ref[...]


def gla_scan(p, gblock, off_q, off_k, off_v, s0, *, reverse, mode, gate=None, rot=None, o_prev=None, q_scale):
    b, l, _ = p.shape
    tb = _pick(l, (SCAN_BLOCK, 128, 64))
    nblk = l // tb
    blk = (lambda i: nblk - 1 - i) if reverse else (lambda i: i)
    gspec = lambda off: pl.BlockSpec((1, tb, GROUP_W), lambda bi, i: (bi, blk(i), off // GROUP_W))
    const = lambda shape: pl.BlockSpec(shape, lambda bi, i: (0,) * len(shape))
    in_specs = [gspec(off_q), gspec(off_k), gspec(off_v)]
    args = [p, p, p]
    if mode == "gla":
        wa_hi, wa_lo, ba = gate
        in_specs += [pl.BlockSpec((1, tb, HEAD_W), lambda bi, i: (bi, blk(i), 0)),
                     const((HEAD_W, GROUP_W)), const((HEAD_W, GROUP_W)), const((1, GROUP_W))]
        args += [gblock, wa_hi, wa_lo, ba]
    else:
        in_specs += [const((1, GROUP_W))]
        args += [gate]
    if rot is not None:
        in_specs += [pl.BlockSpec((tb, HEAD_W), lambda bi, i: (blk(i), 0))] * 2
        args += [rot[0], rot[1]]
    sspec = pl.BlockSpec((1, HEADS, HEAD_W, HEAD_W), lambda bi, i: (bi, 0, 0, 0))
    in_specs += [sspec]
    args += [s0]
    o_spec = pl.BlockSpec((1, tb, GROUP_W), lambda bi, i: (bi, blk(i), 0))
    aliases = {}
    if o_prev is not None:
        in_specs += [o_spec]
        aliases = {len(args): 0}
        args += [o_prev]
    kern = functools.partial(_gla_scan_kernel, reverse=reverse, mode=mode, rotate=rot is not None,
                             accumulate=o_prev is not None, n_chunks=tb // CHUNK, q_scale=q_scale)
    return pl.pallas_call(
        kern,
        grid=(b, nblk),
        in_specs=in_specs,
        out_specs=[o_spec, sspec],
        out_shape=[jax.ShapeDtypeStruct((b, l, GROUP_W), F32),
                   jax.ShapeDtypeStruct((b, HEADS, HEAD_W, HEAD_W), F32)],
        scratch_shapes=[pltpu.VMEM((HEADS, HEAD_W, HEAD_W), F32)] + (
            [pltpu.VMEM((tb, GROUP_W), F32)] * 3 + [pltpu.VMEM((8 * pl.cdiv(tb // CHUNK, 8), GROUP_W), F32)]
            if mode == "ret" else []),
        input_output_aliases=aliases,
        compiler_params=_cparams("parallel", "arbitrary"),
        name=f"{mode}_scan_{'bwd' if reverse else 'fwd'}",
    )(*args)


def _mlstm_kernel(*refs, reverse, accumulate, n_chunks, k_scale):
    it = iter(refs)
    q_ref, k_ref, v_ref, gsrc_ref, bias_ref = (next(it) for _ in range(5))
    c0_ref, n0_ref, m0_ref = next(it), next(it), next(it)
    if accumulate:
        oprev_ref = next(it)
    o_ref, cfin_ref, nfin_ref, mfin_ref = next(it), next(it), next(it), next(it)
    ct_ref, n_ref, m_ref = next(it), next(it), next(it)

    i = pl.program_id(1)

    @pl.when(i == 0)
    def _():
        ct_ref[...] = c0_ref[0]
        n_ref[...] = n0_ref[0]
        m_ref[...] = m0_ref[0]

    tb = n_chunks * CHUNK
    mask = _causal_mask(tb, reverse)
    maskb = mask.astype(BF16)
    order = [(n_chunks - 1 - c) if reverse else c for c in range(n_chunks)]
    rows = [slice(c * CHUNK, (c + 1) * CHUNK) for c in range(n_chunks)]
    lane_i = SMALL_MLG + (8 if reverse else 0)
    lane_f = lane_i + HEADS

    q = q_ref[0].astype(F32)
    k = k_ref[0].astype(F32) * k_scale
    vb = v_ref[0]
    gates = gsrc_ref[0] + bias_ref[...]
    bc_all = _dot_exact_lhs(maskb, _log_sigmoid(gates))
    rep = lambda a, c: jnp.broadcast_to(a[:, c:c + 1], (a.shape[0], HEAD_W))
    outs = []
    for h in range(HEADS):
        hs = slice(h * HEAD_W, (h + 1) * HEAD_W)
        qh, kh, vh = q[:, hs], k[:, hs], vb[:, hs]
        bch = rep(bc_all, lane_f + h)
        wh = rep(gates, lane_i + h) - bch
        bends = [bch[r.start:r.start + 1, :] if reverse else bch[r.stop - 1:r.stop, :] for r in rows]
        w_row = jnp.broadcast_to(jnp.transpose(wh)[0:1, :], (tb, tb))
        dmat = jnp.where(mask, bch[:, 0:1] + w_row, -jnp.inf)
        rowmax = jnp.max(dmat, axis=-1, keepdims=True)
        qb = qh.astype(BF16)
        s_raw = _dot_nt(qb, kh.astype(BF16)) * jnp.exp(dmat - rowmax)
        sv = _dot(s_raw.astype(BF16), vh)
        s_sum = jnp.sum(s_raw, axis=-1, keepdims=True)
        ct, nv, m_prev = ct_ref[h], n_ref[h], m_ref[h]
        out = [None] * n_chunks
        for cc in order:
            r = rows[cc]
            bend = bends[cc]
            glh = bend + wh[r]
            gmax = jnp.max(glh, axis=0, keepdims=True)
            gk = jnp.exp(glh - gmax) * kh[r]
            inter_log = bch[r] + m_prev
            m_t = jnp.maximum(inter_log, rowmax[r])
            e_intra = jnp.exp(rowmax[r] - m_t)
            inter = jnp.exp(inter_log - m_t)
            num = inter * _dot_nt(qb[r], ct.astype(BF16)) + e_intra * sv[r]
            den = jnp.abs(inter * jnp.sum(qh[r] * nv, axis=-1, keepdims=True) + e_intra * s_sum[r])
            out[cc] = num / jnp.maximum(den, jnp.exp(-m_t))
            m_new = jnp.maximum(bend + m_prev, gmax)
            dec = jnp.exp(bend + m_prev - m_new)
            e_upd = jnp.exp(gmax - m_new)
            ct = dec * ct + e_upd * _dot_tn(vh[r], gk.astype(BF16))
            nv = dec * nv + e_upd * jnp.sum(gk, axis=0, keepdims=True)
            m_prev = m_new
        ct_ref[h] = ct
        n_ref[h] = nv
        m_ref[h] = m_prev
        outs.append(jnp.concatenate(out, axis=0))
    o = jnp.concatenate(outs, axis=-1)
    if accumulate:
        o = o + oprev_ref[0]
    o_ref[0] = o

    @pl.when(i == pl.num_programs(1) - 1)
    def _():
        cfin_ref[0] = ct_ref[...]
        nfin_ref[0] = n_ref[...]
        mfin_ref[0] = m_ref[...]


def mlstm_scan(p, gblock, state, gate_bias, *, reverse, o_prev=None):
    b, l, _ = p.shape
    tb = _pick(l, (SCAN_BLOCK, 128, 64))
    nblk = l // tb
    blk = (lambda i: nblk - 1 - i) if reverse else (lambda i: i)
    c0, n0, m0 = state
    gspec = lambda off: pl.BlockSpec((1, tb, GROUP_W), lambda bi, i: (bi, blk(i), off // GROUP_W))
    const = lambda shape: pl.BlockSpec(shape, lambda bi, i: (0,) * len(shape))
    cspec = pl.BlockSpec((1, HEADS, HEAD_W, HEAD_W), lambda bi, i: (bi, 0, 0, 0))
    vspec = pl.BlockSpec((1, HEADS, 1, HEAD_W), lambda bi, i: (bi, 0, 0, 0))
    in_specs = [gspec(OFF_ML_Q), gspec(OFF_ML_K), gspec(OFF_ML_V),
                pl.BlockSpec((1, tb, HEAD_W), lambda bi, i: (bi, blk(i), 0)),
                const((1, HEAD_W)), cspec, vspec, vspec]
    args = [p, p, p, gblock, gate_bias, c0, n0, m0]
    o_spec = pl.BlockSpec((1, tb, GROUP_W), lambda bi, i: (bi, blk(i), 0))
    aliases = {}
    if o_prev is not None:
        in_specs += [o_spec]
        aliases = {len(args): 0}
        args += [o_prev]
    kern = functools.partial(_mlstm_kernel, reverse=reverse, accumulate=o_prev is not None,
                             n_chunks=tb // CHUNK, k_scale=HEAD_W ** -0.5)
    outs = pl.pallas_call(
        kern,
        grid=(b, nblk),
        in_specs=in_specs,
        out_specs=[o_spec, cspec, vspec, vspec],
        out_shape=[jax.ShapeDtypeStruct((b, l, GROUP_W), F32),
                   jax.ShapeDtypeStruct((b, HEADS, HEAD_W, HEAD_W), F32),
                   jax.ShapeDtypeStruct((b, HEADS, 1, HEAD_W), F32),
                   jax.ShapeDtypeStruct((b, HEADS, 1, HEAD_W), F32)],
        scratch_shapes=[pltpu.VMEM((HEADS, HEAD_W, HEAD_W), F32), pltpu.VMEM((HEADS, 1, HEAD_W), F32),
                        pltpu.VMEM((HEADS, 1, HEAD_W), F32)],
        input_output_aliases=aliases,
        compiler_params=_cparams("parallel", "arbitrary"),
        name=f"mlstm_scan_{'bwd' if reverse else 'fwd'}",
    )(*args)
    return outs[0], (outs[1], outs[2], outs[3])


def _hy_pre_kernel(u_ref, up_ref, un_ref, w_ref, b_ref, o_ref, ob_ref, *, rows):
    i = pl.program_id(2)
    u = u_ref[0].astype(F32)
    prev_row = jnp.where(i == 0, 0.0, up_ref[0, HALO - 1:HALO, :].astype(F32))
    next_row = jnp.where(i == pl.num_programs(2) - 1, 0.0, un_ref[0, 0:1, :].astype(F32))
    ridx = lax.broadcasted_iota(jnp.int32, u.shape, 0)
    u_dn = jnp.where(ridx == 0, prev_row, pltpu.roll(u, 1, axis=0))
    u_up = jnp.where(ridx == rows - 1, next_row, pltpu.roll(u, rows - 1, axis=0))
    y = w_ref[0:1, :] * u_dn + w_ref[1:2, :] * u + w_ref[2:3, :] * u_up + b_ref[...]
    o_ref[0] = y
    ob_ref[0] = y.astype(BF16)


def hy_pre(p, conv_w, conv_b):
    b, l, _ = p.shape
    rows = _pick(l, (512, 256, 128, 64))
    nr = l // rows
    r8 = rows // HALO
    n8 = l // HALO
    wpad = jnp.zeros((8, 3 * HY_W), F32).at[:3].set(conv_w)
    return pl.pallas_call(
        functools.partial(_hy_pre_kernel, rows=rows),
        grid=(b, 3, nr),
        in_specs=[pl.BlockSpec((1, rows, HY_W), lambda bi, j, i: (bi, i, j)),
                  pl.BlockSpec((1, HALO, HY_W), lambda bi, j, i: (bi, jnp.maximum(i * r8 - 1, 0), j)),
                  pl.BlockSpec((1, HALO, HY_W), lambda bi, j, i: (bi, jnp.minimum((i + 1) * r8, n8 - 1), j)),
                  pl.BlockSpec((8, HY_W), lambda bi, j, i: (0, j)),
                  pl.BlockSpec((1, HY_W), lambda bi, j, i: (0, j))],
        out_specs=[pl.BlockSpec((1, rows, HY_W), lambda bi, j, i: (bi, i, j))] * 2,
        out_shape=[jax.ShapeDtypeStruct((b, l, 3 * HY_W), F32),
                   jax.ShapeDtypeStruct((b, l, 3 * HY_W), BF16)],
        compiler_params=_cparams("parallel", "parallel", "parallel"),
        name="hy_shortconv",
    )(p, p, p, wpad, conv_b.reshape(1, 3 * HY_W))


def _mm_kernel(a_ref, b_ref, o_ref):
    o_ref[...] = _dot(a_ref[...], b_ref[...])


def matmul_bf16(a, bm):
    m, k = a.shape
    n = bm.shape[1]
    tm = _pick(m, (512, 256, 128, 64))
    tn = _pick(n, (512, 256, 128))
    return pl.pallas_call(
        _mm_kernel,
        grid=(n // tn, m // tm),
        in_specs=[pl.BlockSpec((tm, k), lambda j, i: (i, 0)),
                  pl.BlockSpec((k, tn), lambda j, i: (0, j))],
        out_specs=pl.BlockSpec((tm, tn), lambda j, i: (i, j)),
        out_shape=jax.ShapeDtypeStruct((m, n), F32),
        compiler_params=_cparams("parallel", "parallel"),
        name="matmul_bf16",
    )(a, bm)


def _dft_fwd_kernel(c_ref, s_ref, z_ref, hr_ref, hi_ref, yr_ref, yi_ref):
    z = z_ref[0]
    cz = _dot(c_ref[...], z)
    sz = _dot(s_ref[...], z)
    hr = hr_ref[...]
    hi = hi_ref[...]
    yr_ref[0] = (cz * hr + sz * hi).astype(BF16)
    yi_ref[0] = (cz * hi - sz * hr).astype(BF16)


def dft_fwd(cmat, smat, zb, z_col, hr, hi, h_col):
    b, l, _ = zb.shape
    f = cmat.shape[0]
    tf = _pick(f, (512, 256, 128, 64))
    return pl.pallas_call(
        _dft_fwd_kernel,
        grid=(f // tf, b),
        in_specs=[pl.BlockSpec((tf, l), lambda i, bi: (i, 0)),
                  pl.BlockSpec((tf, l), lambda i, bi: (i, 0)),
                  pl.BlockSpec((1, l, HY_W), lambda i, bi: (bi, 0, z_col)),
                  pl.BlockSpec((tf, HY_W), lambda i, bi: (i, h_col)),
                  pl.BlockSpec((tf, HY_W), lambda i, bi: (i, h_col))],
        out_specs=[pl.BlockSpec((1, tf, HY_W), lambda i, bi: (bi, i, 0))] * 2,
        out_shape=[jax.ShapeDtypeStruct((b, f, HY_W), BF16)] * 2,
        compiler_params=_cparams("parallel", "parallel"),
        name="hy_dft_fwd",
    )(cmat, smat, zb, hr, hi)


def _dft_inv_kernel(ct_ref, st_ref, yr_ref, yi_ref, x_ref, zp_ref, skip_ref, o_ref, ob_ref):
    conv = _dot(ct_ref[...], yr_ref[0]) - _dot(st_ref[...], yi_ref[0])
    z = x_ref[0] * (conv + skip_ref[...] * zp_ref[0])
    o_ref[0] = z
    ob_ref[0] = z.astype(BF16)


def dft_inv(ctmat, stmat, yr, yi, x_arr, x_col, zp_arr, zp_col, skip):
    b, f, _ = yr.shape
    l = ctmat.shape[0]
    tt = _pick(l, (512, 256, 128, 64))
    return pl.pallas_call(
        _dft_inv_kernel,
        grid=(l // tt, b),
        in_specs=[pl.BlockSpec((tt, f), lambda i, bi: (i, 0)),
                  pl.BlockSpec((tt, f), lambda i, bi: (i, 0)),
                  pl.BlockSpec((1, f, HY_W), lambda i, bi: (bi, 0, 0)),
                  pl.BlockSpec((1, f, HY_W), lambda i, bi: (bi, 0, 0)),
                  pl.BlockSpec((1, tt, HY_W), lambda i, bi: (bi, i, x_col)),
                  pl.BlockSpec((1, tt, HY_W), lambda i, bi: (bi, i, zp_col)),
                  pl.BlockSpec((1, HY_W), lambda i, bi: (0, 0))],
        out_specs=[pl.BlockSpec((1, tt, HY_W), lambda i, bi: (bi, i, 0))] * 2,
        out_shape=[jax.ShapeDtypeStruct((b, l, HY_W), F32),
                   jax.ShapeDtypeStruct((b, l, HY_W), BF16)],
        compiler_params=_cparams("parallel", "parallel"),
        name="hy_dft_inv",
    )(ctmat, stmat, yr, yi, x_arr, zp_arr, skip.reshape(1, HY_W))


def dft_tables(l):
    r = _pick(l, (64, 32, 16, 8))
    period = 8 * l
    odd = 2 * jnp.arange(l, dtype=jnp.int32) + 1
    s1 = jnp.arange(l // r, dtype=jnp.int32) * (2 * r)
    s0 = 2 * jnp.arange(r, dtype=jnp.int32) + 1
    ang = lambda ph: ph.astype(F32) * (2.0 * math.pi / period)
    a = ang((odd[:, None] * s1[None, :]) % period)
    bb = ang((odd[:, None] * s0[None, :]) % period)
    ca, sa, cb, sb = jnp.cos(a), jnp.sin(a), jnp.cos(bb), jnp.sin(bb)
    cmat = (ca[:, :, None] * cb[:, None, :] - sa[:, :, None] * sb[:, None, :]).reshape(l, l).astype(BF16)
    smat = (sa[:, :, None] * cb[:, None, :] + ca[:, :, None] * sb[:, None, :]).reshape(l, l).astype(BF16)
    phi = ang(odd)[:, None]
    return cmat, smat, jnp.cos(phi), jnp.sin(phi)


def hyena_filter_taps(length, lp):
    pos = jnp.arange(length, dtype=F32)
    t = pos / (length - 1)
    bands = (HY_EMB - 1) // 2
    fr = jnp.linspace(1e-4, bands - 1, bands, dtype=F32)
    ang = (2.0 * math.pi / length) * pos[:, None] * fr[None, :]
    z = jnp.concatenate([t[:, None], jnp.cos(ang), -jnp.sin(ang)], axis=-1)
    mm = functools.partial(jnp.matmul, precision=HIGHEST)
    hdn = jnp.sin(lp['hy_f_freq'][0] * (mm(z, lp['hy_f_w1']) + lp['hy_f_b1']))
    hdn = jnp.sin(lp['hy_f_freq'][1] * (mm(hdn, lp['hy_f_w2']) + lp['hy_f_b2']))
    return mm(hdn, lp['hy_f_w3']) * jnp.exp(-t[:, None] * jnp.abs(lp['hy_decay']))


def hyena_mixer(p, lp, tables):
    b, l, _ = p.shape
    cmat, smat, cphi, sphi = tables
    uc, ucb = hy_pre(p, lp['hy_conv_w'], lp['hy_conv_b'])
    taps = hyena_filter_taps(l, lp).reshape(l, 2, 2, HY_W)
    hf = taps[:, :, 0, :]
    hbs = jnp.concatenate([taps[1:, :, 1, :], jnp.zeros((1, 2, HY_W), F32)], axis=0)
    hsum = (hf + hbs).reshape(l, 2 * HY_W).astype(BF16)
    hdiff = (hf - hbs).reshape(l, 2 * HY_W).astype(BF16)
    hr_s = matmul_bf16(cmat, hsum)
    hi_s = -matmul_bf16(smat, hdiff)
    hr = (cphi * hr_s - sphi * hi_s) * (1.0 / l)
    hi = (sphi * hr_s + cphi * hi_s) * (1.0 / l)
    yr, yi = dft_fwd(cmat, smat, ucb, 0, hr, hi, 0)
    z1, z1b = dft_inv(cmat, smat, yr, yi, uc, 1, uc, 0, lp['hy_skip'][0])
    yr, yi = dft_fwd(cmat, smat, z1b, 0, hr, hi, 1)
    z2, _ = dft_inv(cmat, smat, yr, yi, uc, 2, z1, 0, lp['hy_skip'][1])
    return z2


def _head_rms(y):
    parts = []
    for h in range(HEADS):
        yh = y[:, h * HEAD_W:(h + 1) * HEAD_W]
        parts.append(yh * lax.rsqrt(jnp.mean(yh * yh, axis=-1, keepdims=True) + EPS))
    return jnp.concatenate(parts, axis=-1)


def _outproj_kernel(hy_ref, gla_ref, glag_ref, ml_ref, mlo_ref, ret_ref, retg_ref,
                    gn_ref, w_ref, res_ref, gate_ref, o_ref):
    gn = gn_ref[...]
    y_gla = _head_rms(gla_ref[0]) * gn[0:1, :] * _silu(glag_ref[0].astype(F32))
    y_ml = _head_rms(_sigmoid(mlo_ref[0].astype(F32)) * ml_ref[0]) * gn[1:2, :]
    y_ret = _head_rms(ret_ref[0]) * gn[2:3, :] * _silu(retg_ref[0].astype(F32))
    acc = _dot(hy_ref[0].astype(BF16), w_ref[0:GROUP_W, :])
    acc += _dot(y_gla.astype(BF16), w_ref[GROUP_W:2 * GROUP_W, :])
    acc += _dot(y_ml.astype(BF16), w_ref[2 * GROUP_W:3 * GROUP_W, :])
    acc += _dot(y_ret.astype(BF16), w_ref[3 * GROUP_W:4 * GROUP_W, :])
    o_ref[0] = res_ref[0] + gate_ref[...] * acc


def outproj(hy, o_gla, o_ml, o_ret, p, gains, w_out, res, mod4, row_of_batch, i_gate):
    b, l, d = res.shape
    tm = _pick(l, (256, 128, 64))
    gw = GROUP_W
    ospec = pl.BlockSpec((1, tm, gw), lambda bi, i: (bi, i, 0))
    pspec = lambda off: pl.BlockSpec((1, tm, gw), lambda bi, i: (bi, i, off // gw))
    return pl.pallas_call(
        _outproj_kernel,
        grid=(b, l // tm),
        in_specs=[ospec, ospec, pspec(OFF_GLA_G), ospec, pspec(OFF_ML_O), ospec, pspec(OFF_RET_G),
                  pl.BlockSpec((8, gw), lambda bi, i: (0, 0)),
                  pl.BlockSpec((4 * gw, d), lambda bi, i: (0, 0)),
                  pl.BlockSpec((1, tm, d), lambda bi, i: (bi, i, 0)),
                  pl.BlockSpec((None, None, 1, d), lambda bi, i: (row_of_batch(bi), i_gate, 0, 0))],
        out_specs=pl.BlockSpec((1, tm, d), lambda bi, i: (bi, i, 0)),
        out_shape=jax.ShapeDtypeStruct((b, l, d), F32),
        compiler_params=_cparams("parallel", "parallel"),
        name="outproj",
    )(hy, o_gla, p, o_ml, p, o_ret, p, gains, w_out, res, mod4)


def _moe_pre_kernel(x_ref, g_ref, sc_ref, sh_ref, wr_ref, br_ref, t_ref, lg_ref):
    t = _norm_mod(x_ref[0], g_ref[...], sc_ref[...], sh_ref[...])
    t_ref[0] = _pack_pairs(t)
    lg_ref[0] = _dot_hi(t, wr_ref[...]) + br_ref[...]


def moe_pre(x, g, mod4, row_of_batch, i_sc, i_sh, w_router, b_router):
    b, l, d = x.shape
    tm = _pick(l, (512, 256, 128, 64))
    nr = w_router.shape[1]
    return pl.pallas_call(
        _moe_pre_kernel,
        grid=(b, l // tm),
        in_specs=[pl.BlockSpec((1, tm, d), lambda bi, i: (bi, i, 0)),
                  pl.BlockSpec((1, d), lambda bi, i: (0, 0)),
                  pl.BlockSpec((None, None, 1, d), lambda bi, i: (row_of_batch(bi), i_sc, 0, 0)),
                  pl.BlockSpec((None, None, 1, d), lambda bi, i: (row_of_batch(bi), i_sh, 0, 0)),
                  pl.BlockSpec((d, nr), lambda bi, i: (0, 0)),
                  pl.BlockSpec((1, nr), lambda bi, i: (0, 0))],
        out_specs=[pl.BlockSpec((1, tm, d // 2), lambda bi, i: (bi, i, 0)),
                   pl.BlockSpec((1, tm, nr), lambda bi, i: (bi, i, 0))],
        out_shape=[jax.ShapeDtypeStruct((b, l, d // 2), jnp.uint32), jax.ShapeDtypeStruct((b, l, nr), F32)],
        compiler_params=_cparams("parallel", "parallel"),
        name="moe_pre",
    )(x, g.reshape(1, d), mod4, mod4, w_router, b_router)


ROUTE_E1, ROUTE_E2, ROUTE_R1, ROUTE_R2, ROUTE_W1, ROUTE_W2 = range(6)


def _moe_route_kernel(lg_ref, route_ref, cnt_ref, run_ref, *, tr):
    i = pl.program_id(0)

    @pl.when(i == 0)
    def _():
        run_ref[...] = jnp.zeros_like(run_ref)

    lg = lg_ref[...]
    lane = lax.broadcasted_iota(jnp.int32, lg.shape, 1).astype(F32)
    no_lane = float(HEAD_W)
    neg = -jnp.inf
    first_max = lambda vals, vmax: jnp.min(jnp.where(vals == vmax, lane, no_lane), axis=-1, keepdims=True)
    gl = jnp.where(lane < MOE_GROUPS, lg, neg)
    gmax = jnp.max(gl, axis=-1, keepdims=True)
    pg = 1.0 / jnp.sum(jnp.exp(gl - gmax), axis=-1, keepdims=True)
    lo = MOE_GROUPS + first_max(gl, gmax) * MOE_EPG
    sel = jnp.where(jnp.logical_and(lane >= lo, lane < lo + MOE_EPG), lg, neg)
    v1 = jnp.max(sel, axis=-1, keepdims=True)
    i1 = first_max(sel, v1)
    sel2 = jnp.where(lane == i1, neg, sel)
    v2 = jnp.max(sel2, axis=-1, keepdims=True)
    i2 = first_max(sel2, v2)
    e21 = jnp.exp(v2 - v1)
    w1 = pg / (1.0 + e21)
    w2 = w1 * e21
    e1 = i1 - MOE_GROUPS
    e2 = i2 - MOE_GROUPS
    oh1 = (lane == e1).astype(F32)
    oh2 = (lane == e2).astype(F32)
    both = oh1 + oh2
    row = lax.broadcasted_iota(jnp.int32, (tr, tr), 0)
    col = lax.broadcasted_iota(jnp.int32, (tr, tr), 1)
    earlier = (col < row).astype(BF16)
    before = _dot(earlier, both.astype(BF16)) + run_ref[...]
    r1 = jnp.sum(before * oh1, axis=-1, keepdims=True)
    r2 = jnp.sum(before * oh2, axis=-1, keepdims=True)
    run_ref[...] += jnp.sum(both, axis=0, keepdims=True)
    rec = jnp.zeros_like(lg)
    for k, val in ((ROUTE_E1, e1), (ROUTE_E2, e2), (ROUTE_R1, r1), (ROUTE_R2, r2), (ROUTE_W1, w1), (ROUTE_W2, w2)):
        rec = jnp.where(lane == k, val, rec)
    route_ref[...] = rec

    @pl.when(i == pl.num_programs(0) - 1)
    def _():
        cnt_ref[...] = run_ref[...]


def moe_route(logits):
    t, nl = logits.shape
    tr = _pick(t, (512, 256, 128, 64))
    return pl.pallas_call(
        functools.partial(_moe_route_kernel, tr=tr),
        grid=(t // tr,),
        in_specs=[pl.BlockSpec((tr, nl), lambda i: (i, 0))],
        out_specs=[pl.BlockSpec((tr, nl), lambda i: (i, 0)), pl.BlockSpec((1, nl), lambda i: (0, 0))],
        out_shape=[jax.ShapeDtypeStruct((t, nl), F32), jax.ShapeDtypeStruct((1, nl), F32)],
        scratch_shapes=[pltpu.VMEM((1, nl), F32)],
        compiler_params=_cparams("arbitrary"),
        name="moe_route",
    )(logits)


def _moe_pos_kernel(route_ref, pstart_ref, pos_ref):
    rec = route_ref[...]
    lane = lax.broadcasted_iota(jnp.int32, rec.shape, 1).astype(F32)
    start = lambda e: jnp.sum(jnp.where(lane == e, pstart_ref[...], 0.0), axis=-1, keepdims=True)
    p1 = start(rec[:, ROUTE_E1:ROUTE_E1 + 1]) + rec[:, ROUTE_R1:ROUTE_R1 + 1]
    p2 = start(rec[:, ROUTE_E2:ROUTE_E2 + 1]) + rec[:, ROUTE_R2:ROUTE_R2 + 1]
    pos_ref[...] = jnp.where(lane == 0, p1, jnp.where(lane == 1, p2, 0.0))


def moe_pos(route, pstart_row):
    t, nl = route.shape
    tr = _pick(t, (512, 256, 128, 64))
    return pl.pallas_call(
        _moe_pos_kernel,
        grid=(t // tr,),
        in_specs=[pl.BlockSpec((tr, nl), lambda i: (i, 0)), pl.BlockSpec((1, nl), lambda i: (0, 0))],
        out_specs=pl.BlockSpec((tr, nl), lambda i: (i, 0)),
        out_shape=jax.ShapeDtypeStruct((t, nl), F32),
        compiler_params=_cparams("parallel"),
        name="moe_pos",
    )(route, pstart_row)


ISSUE_UNROLL = 8


def _row_copy_wait(src_rows, dst_rows, sem):
    pltpu.make_async_copy(src_rows, dst_rows, sem).wait()


def _moe_dispatch_kernel(pend_ref, padded_ref, pos_ref, tok_ref, xs_ref, zero_ref, sem, *, tr, tm):
    @pl.when(pl.program_id(0) == 0)
    def _():
        zero_ref[...] = jnp.zeros_like(zero_ref)

        def clear(e, carry):
            @pl.when(padded_ref[e] > 0)
            def _():
                dst = xs_ref.at[pl.ds(pl.multiple_of(pend_ref[e] - tm, tm), tm)]
                cp = pltpu.make_async_copy(zero_ref, dst, sem)
                cp.start()
                cp.wait()
            return carry

        lax.fori_loop(0, N_EXPERTS, clear, 0)

    def issue(r, carry):
        for slot in range(2):
            dst = pos_ref[0, 0, 2 * r + slot]
            pltpu.make_async_copy(tok_ref.at[pl.ds(r, 1)], xs_ref.at[pl.ds(dst, 1)], sem).start()
        return carry

    lax.fori_loop(0, tr, issue, 0, unroll=ISSUE_UNROLL)
    for _ in range(2):
        _row_copy_wait(tok_ref, xs_ref.at[pl.ds(0, tr)], sem)


def moe_dispatch(tok, pos, pend, padded, pmax, tm):
    t, d = tok.shape
    tr = pos.shape[2] // 2
    grid_spec = pltpu.PrefetchScalarGridSpec(
        num_scalar_prefetch=2,
        grid=(t // tr,),
        in_specs=[pl.BlockSpec((1, 1, 2 * tr), lambda i, pe, pa: (i, 0, 0), memory_space=pltpu.SMEM),
                  pl.BlockSpec((tr, d), lambda i, pe, pa: (i, 0))],
        out_specs=pl.BlockSpec(memory_space=pl.ANY),
        scratch_shapes=[pltpu.VMEM((tm, d), tok.dtype), pltpu.SemaphoreType.DMA],
    )
    return pl.pallas_call(
        functools.partial(_moe_dispatch_kernel, tr=tr, tm=tm),
        grid_spec=grid_spec,
        out_shape=jax.ShapeDtypeStruct((pmax, d), tok.dtype),
        compiler_params=_cparams("arbitrary"),
        name="moe_dispatch",
    )(pend, padded, pos, tok)


def _moe_expert_kernel(te_ref, nu_ref, x_ref, w1_ref, w3_ref, w2_ref, o_ref, w1b_ref, w3b_ref, w2b_ref):
    t = pl.program_id(0)

    @pl.when(t < nu_ref[0])
    def _():
        @pl.when(jnp.logical_or(t == 0, te_ref[t] != te_ref[jnp.maximum(t - 1, 0)]))
        def _():
            w1b_ref[...] = w1_ref[0].astype(BF16)
            w3b_ref[...] = w3_ref[0].astype(BF16)
            w2b_ref[...] = w2_ref[0].astype(BF16)

        x = _unpack_pairs(x_ref[...]).astype(BF16)
        act = _silu(_dot(x, w1b_ref[...])) * _dot(x, w3b_ref[...])
        o_ref[...] = _pack_pairs(_dot(act.astype(BF16), w2b_ref[...]))


def moe_experts(x_sorted, w1, w3, w2, tile_expert, n_used, tm):
    pmax, dp = x_sorted.shape
    d = 2 * dp
    ff = w1.shape[2]
    used = lambda t, nu: jnp.minimum(t, nu[0] - 1)
    grid_spec = pltpu.PrefetchScalarGridSpec(
        num_scalar_prefetch=2,
        grid=(pmax // tm,),
        in_specs=[pl.BlockSpec((tm, dp), lambda t, te, nu: (used(t, nu), 0)),
                  pl.BlockSpec((1, d, ff), lambda t, te, nu: (te[t], 0, 0)),
                  pl.BlockSpec((1, d, ff), lambda t, te, nu: (te[t], 0, 0)),
                  pl.BlockSpec((1, ff, d), lambda t, te, nu: (te[t], 0, 0))],
        out_specs=pl.BlockSpec((tm, dp), lambda t, te, nu: (used(t, nu), 0)),
        scratch_shapes=[pltpu.VMEM((d, ff), BF16), pltpu.VMEM((d, ff), BF16), pltpu.VMEM((ff, d), BF16)],
    )
    return pl.pallas_call(
        _moe_expert_kernel,
        grid_spec=grid_spec,
        out_shape=jax.ShapeDtypeStruct((pmax, dp), jnp.uint32),
        compiler_params=_cparams("arbitrary"),
        name="moe_experts",
    )(tile_expert, n_used, x_sorted, w1, w3, w2)


def _moe_combine_kernel(p1_ref, p2_ref, p1n_ref, p2n_ref, ys_ref, res_ref, route_ref, gate_ref, o_ref,
                        y1_ref, y2_ref, sem, *, tc):
    i = pl.program_id(0)
    slot = i % 2

    def issue(pa_ref, pb_ref, s):
        def body(r, carry):
            pltpu.make_async_copy(ys_ref.at[pl.ds(pa_ref[0, 0, r], 1)], y1_ref.at[s, pl.ds(r, 1)], sem.at[s]).start()
            pltpu.make_async_copy(ys_ref.at[pl.ds(pb_ref[0, 0, r], 1)], y2_ref.at[s, pl.ds(r, 1)], sem.at[s]).start()
            return carry
        lax.fori_loop(0, tc, body, 0, unroll=ISSUE_UNROLL)

    @pl.when(i == 0)
    def _():
        issue(p1_ref, p2_ref, 0)

    @pl.when(i + 1 < pl.num_programs(0))
    def _():
        issue(p1n_ref, p2n_ref, 1 - slot)

    _row_copy_wait(ys_ref.at[pl.ds(0, tc)], y1_ref.at[slot], sem.at[slot])
    _row_copy_wait(ys_ref.at[pl.ds(0, tc)], y2_ref.at[slot], sem.at[slot])
    rec = route_ref[...]
    w1 = rec[:, ROUTE_W1:ROUTE_W1 + 1]
    w2 = rec[:, ROUTE_W2:ROUTE_W2 + 1]
    y = w1 * _unpack_pairs(y1_ref[slot]) + w2 * _unpack_pairs(y2_ref[slot])
    o_ref[...] = res_ref[...] + gate_ref[...] * y


def moe_combine(res, y_sorted, pos1, pos2, route, mod4, row_of_tile, i_gate):
    t, d = res.shape
    n, _, tc = pos1.shape
    cur = lambda i: (i, 0, 0)
    nxt = lambda i: (jnp.minimum(i + 1, n - 1), 0, 0)
    smem = lambda im: pl.BlockSpec((1, 1, tc), im, memory_space=pltpu.SMEM)
    return pl.pallas_call(
        functools.partial(_moe_combine_kernel, tc=tc),
        grid=(n,),
        in_specs=[smem(cur), smem(cur), smem(nxt), smem(nxt),
                  pl.BlockSpec(memory_space=pl.ANY),
                  pl.BlockSpec((tc, d), lambda i: (i, 0)),
                  pl.BlockSpec((tc, route.shape[1]), lambda i: (i, 0)),
                  pl.BlockSpec((None, None, 1, d), lambda i: (row_of_tile(i), i_gate, 0, 0))],
        out_specs=pl.BlockSpec((tc, d), lambda i: (i, 0)),
        out_shape=jax.ShapeDtypeStruct((t, d), F32),
        scratch_shapes=[pltpu.VMEM((2, tc, d // 2), y_sorted.dtype), pltpu.VMEM((2, tc, d // 2), y_sorted.dtype),
                        pltpu.SemaphoreType.DMA((2,))],
        compiler_params=_cparams("arbitrary"),
        name="moe_combine",
    )(pos1, pos2, pos1, pos2, y_sorted, res, route, mod4)


def hier_moe(x, lp, mod4, row_of_batch):
    b, l, d = x.shape
    t = b * l
    tm = 256 if t >= 4096 else 64
    tok, logits = moe_pre(x, lp['norm2_g'], mod4, row_of_batch, 4, 3, lp['w_router'], lp['b_router'])
    tok = tok.reshape(t, d // 2)
    route, cnt = moe_route(logits.reshape(t, -1))
    counts = cnt[0, :N_EXPERTS].astype(jnp.int32)
    padded = ((counts + tm - 1) // tm) * tm
    pend = jnp.cumsum(padded)
    pstart = pend - padded
    pmax = 2 * t + N_EXPERTS * tm
    tile_row = jnp.arange(pmax // tm, dtype=jnp.int32) * tm
    tile_expert = jnp.minimum(jnp.sum(tile_row[:, None] >= pend[None, :], axis=1), N_EXPERTS - 1).astype(jnp.int32)
    n_used = (pend[-1] // tm).astype(jnp.int32).reshape(1)
    pstart_row = jnp.zeros((1, route.shape[1]), F32).at[0, :N_EXPERTS].set(pstart.astype(F32))
    pos12 = moe_pos(route, pstart_row)[:, :2].astype(jnp.int32)
    pos1, pos2 = pos12[:, 0], pos12[:, 1]
    tr = _pick(t, (512, 256, 128, 64))
    tc = _pick(l, (256, 128, 64))
    pos = pos12.reshape(t // tr, 1, 2 * tr)
    x_sorted = moe_dispatch(tok, pos, pend, padded, pmax, tm)
    y_sorted = moe_experts(x_sorted, lp['moe_w1e'], lp['moe_w3e'], lp['moe_w2e'],
                           tile_expert + lp['expert_base'], n_used, tm)
    tiles_per_batch = l // tc
    out = moe_combine(x.reshape(t, d), y_sorted, pos1.reshape(t // tc, 1, tc), pos2.reshape(t // tc, 1, tc),
                      route, mod4, lambda i: row_of_batch(i // tiles_per_batch), 5)
    return out.reshape(b, l, d)


def _final_norm_kernel(x_ref, g_ref, o_ref):
    x = x_ref[0]
    o_ref[0] = x * lax.rsqrt(jnp.mean(x * x, axis=-1, keepdims=True) + EPS) * g_ref[...]


def final_norm(x, g):
    b, l, d = x.shape
    tm = _pick(l, (512, 256, 128, 64))
    spec = pl.BlockSpec((1, tm, d), lambda bi, i: (bi, i, 0))
    return pl.pallas_call(
        _final_norm_kernel,
        grid=(b, l // tm),
        in_specs=[spec, pl.BlockSpec((1, d), lambda bi, i: (0, 0))],
        out_specs=spec,
        out_shape=jax.ShapeDtypeStruct((b, l, d), F32),
        compiler_params=_cparams("parallel", "parallel"),
        name="final_norm",
    )(x, g.reshape(1, d))


def _pad_heads(w, dk):
    d = w.shape[0]
    return jnp.pad(w.reshape(d, HEADS, dk), ((0, 0), (0, 0), (0, HEAD_W - dk))).reshape(d, HEADS * HEAD_W)


def prep_w_in(w_in):
    o = 0
    take = lambda n: (w_in[:, o:o + n], o + n)
    hy, o = take(3 * HY_W)
    gq, o = take(HEADS * GLA_DK)
    gk, o = take(HEADS * GLA_DK)
    gv, o = take(GROUP_W)
    gg, o = take(GROUP_W)
    gr, o = take(2 * GLA_RANK)
    ml, o = take(4 * GROUP_W)
    mg, o = take(16)
    ret, o = take(4 * GROUP_W)
    gk_p = _pad_heads(gk, GLA_DK)
    gk_p = gk_p.at[:, SMALL_R:SMALL_R + 2 * GLA_RANK].set(gr).at[:, SMALL_MLG:SMALL_MLG + 16].set(mg)
    return jnp.concatenate([hy, _pad_heads(gq, GLA_DK), gk_p, gv, gg, ml, ret], axis=1).astype(BF16)


def prep_layer(l, prm):
    lp = {k: v[l] for k, v in prm.items()}
    lp['w_in_p'] = prep_w_in(lp['w_in'])
    lp['w_out_b'] = lp['w_out'].astype(BF16)
    gates = []
    for d in range(2):
        w = jnp.zeros((HEAD_W, HEADS * HEAD_W), F32)
        w = w.at[SMALL_R + d * GLA_RANK:SMALL_R + (d + 1) * GLA_RANK].set(_pad_heads(lp['gla_wa2'][d], GLA_DK))
        w_hi = w.astype(BF16)
        w_lo = (w - w_hi.astype(F32)).astype(BF16)
        gates.append((w_hi, w_lo, _pad_heads(lp['gla_ba'][d][None, :], GLA_DK)))
    lp['gla_gate'] = gates
    bias = jnp.zeros((1, HEAD_W), F32).at[0, SMALL_MLG:SMALL_MLG + 16].set(lp['ml_gate_b'].reshape(-1))
    lp['ml_bias'] = bias
    lp['ret_dec'] = [jnp.repeat(lp['ret_decay'][d], HEAD_W)[None, :] for d in range(2)]
    gains = jnp.zeros((8, GROUP_W), F32)
    lp['gains'] = gains.at[0].set(lp['gla_norm_g']).at[1].set(lp['ml_norm_g']).at[2].set(lp['ret_norm_g'])
    d_model = lp['moe_wg'].shape[0]
    wr = jnp.zeros((d_model, HEAD_W), F32)
    lp['w_router'] = wr.at[:, :MOE_GROUPS].set(lp['moe_wg']).at[:, MOE_GROUPS:MOE_GROUPS + N_EXPERTS].set(lp['moe_we'])
    br = jnp.zeros((1, HEAD_W), F32)
    lp['b_router'] = br.at[0, :MOE_GROUPS].set(lp['moe_bg']).at[0, MOE_GROUPS:MOE_GROUPS + N_EXPERTS].set(lp['moe_be'])
    lp['moe_w1e'] = prm['moe_w1'].reshape(-1, d_model, MOE_FF)
    lp['moe_w3e'] = prm['moe_w3'].reshape(-1, d_model, MOE_FF)
    lp['moe_w2e'] = prm['moe_w2'].reshape(-1, MOE_FF, d_model)
    lp['expert_base'] = l * N_EXPERTS
    return lp


def rotary_tables(length):
    rows = length // GRID_W
    row = jnp.repeat(jnp.arange(rows, dtype=F32), GRID_W)
    col = jnp.tile(jnp.arange(GRID_W, dtype=F32), rows)
    nf = HEAD_W // 4
    inv = ROPE_BASE ** (-jnp.arange(nf, dtype=F32) / nf)
    ang = jnp.concatenate([row[:, None] * inv, col[:, None] * inv], axis=-1)
    cos, sin = jnp.cos(ang), jnp.sin(ang)
    return jnp.concatenate([cos, cos], axis=-1), jnp.concatenate([-sin, sin], axis=-1)


def bidir_scans(pg_ctx, pg_lat, lp, rot):
    (p_ctx, g_ctx), (p_lat, g_lat) = pg_ctx, pg_lat
    b = p_lat.shape[0]
    zeros_s = jnp.zeros((b, HEADS, HEAD_W, HEAD_W), F32)
    zeros_v = jnp.zeros((b, HEADS, 1, HEAD_W), F32)
    out = {}
    for name, mode, offs, scale in (("gla", "gla", (OFF_GLA_Q, OFF_GLA_K, OFF_GLA_V), GLA_DK ** -0.5),
                                    ("ret", "ret", (OFF_RET_Q, OFF_RET_K, OFF_RET_V), HEAD_W ** -0.5)):
        o_c, o_l = None, None
        for d in range(2):
            gate = lp['gla_gate'][d] if mode == "gla" else lp['ret_dec'][d]
            o_c, st = gla_scan(p_ctx, g_ctx, *offs, zeros_s, reverse=d == 1, mode=mode, gate=gate,
                               o_prev=o_c, q_scale=scale)
            o_l, _ = gla_scan(p_lat, g_lat, *offs, st, reverse=d == 1, mode=mode, gate=gate,
                              rot=rot if mode == "ret" else None, o_prev=o_l, q_scale=scale)
        out[name] = (o_c, o_l)
    o_c, o_l = None, None
    for d in range(2):
        o_c, st = mlstm_scan(p_ctx, g_ctx, (zeros_s, zeros_v, zeros_v), lp['ml_bias'], reverse=d == 1, o_prev=o_c)
        o_l, _ = mlstm_scan(p_lat, g_lat, st, lp['ml_bias'], reverse=d == 1, o_prev=o_l)
    out["ml"] = (o_c, o_l)
    return out


def kernel(x, c, ctx, c_ctx, ada_w, ada_b, norm1_g, norm2_g, w_in, hy_conv_w, hy_conv_b, hy_f_w1, hy_f_b1, hy_f_w2, hy_f_b2, hy_f_freq, hy_f_w3, hy_decay, hy_skip, gla_wa2, gla_ba, gla_norm_g, ml_gate_b, ml_norm_g, ret_decay, ret_norm_g, w_out, moe_wg, moe_bg, moe_we, moe_be, moe_w1, moe_w3, moe_w2, final_g):
    prm = dict(norm1_g=norm1_g, norm2_g=norm2_g, w_in=w_in, hy_conv_w=hy_conv_w, hy_conv_b=hy_conv_b,
               hy_f_w1=hy_f_w1, hy_f_b1=hy_f_b1, hy_f_w2=hy_f_w2, hy_f_b2=hy_f_b2, hy_f_freq=hy_f_freq,
               hy_f_w3=hy_f_w3, hy_decay=hy_decay, hy_skip=hy_skip, gla_wa2=gla_wa2, gla_ba=gla_ba,
               gla_norm_g=gla_norm_g, ml_gate_b=ml_gate_b, ml_norm_g=ml_norm_g, ret_decay=ret_decay,
               ret_norm_g=ret_norm_g, w_out=w_out, moe_wg=moe_wg, moe_bg=moe_bg, moe_we=moe_we,
               moe_be=moe_be, moe_w1=moe_w1, moe_w3=moe_w3, moe_w2=moe_w2)
    depth = ada_w.shape[0]
    b, seq, d = x.shape
    lc = ctx.shape[1]
    lat = x.astype(F32)
    cx = ctx.astype(F32)
    cvec = jnp.zeros((8, d), F32).at[:b].set(c.astype(F32)).at[b].set(c_ctx.astype(F32))
    mod_all = adaln(cvec, ada_w, ada_b)
    lat_row = lambda bi: bi
    ctx_row = lambda bi: b
    rot = rotary_tables(seq)
    tab_lat = dft_tables(seq)
    tab_ctx = dft_tables(lc)
    for l in range(depth):
        with_ctx = l < depth - 1
        lp = prep_layer(l, prm)
        mod4 = mod_all[l].reshape(8, 6, 1, d)
        p_lat, g_lat = inproj(lat, lp['norm1_g'], mod4, lat_row, 1, 0, lp['w_in_p'])
        p_ctx, g_ctx = inproj(cx, lp['norm1_g'], mod4, ctx_row, 1, 0, lp['w_in_p'])
        scans = bidir_scans((p_ctx, g_ctx), (p_lat, g_lat), lp, rot)
        hy_lat = hyena_mixer(p_lat, lp, tab_lat)
        lat = outproj(hy_lat, scans["gla"][1], scans["ml"][1], scans["ret"][1], p_lat, lp['gains'],
                      lp['w_out_b'], lat, mod4, lat_row, 2)
        lat = hier_moe(lat, lp, mod4, lat_row)
        if with_ctx:
            hy_ctx = hyena_mixer(p_ctx, lp, tab_ctx)
            cx = outproj(hy_ctx, scans["gla"][0], scans["ml"][0], scans["ret"][0], p_ctx, lp['gains'],
                         lp['w_out_b'], cx, mod4, ctx_row, 2)
            cx = hier_moe(cx, lp, mod4, ctx_row)
    return final_norm(lat, final_g).astype(x.dtype)
```

```python
import functools
import math

import jax
import jax.numpy as jnp
from jax import lax
from jax.experimental import pallas as pl
from jax.experimental.pallas import tpu as pltpu

F32 = jnp.float32
BF16 = jnp.bfloat16
HIGHEST = lax.Precision.HIGHEST

EPS = 1e-6
CHUNK = 64
SCAN_BLOCK = 256
NORM_ROWS = 128
HALO = 16
GRID_W = 64
ROPE_BASE = 10000.0
HEADS = 4
HEAD_W = 128
HY_W = 512
HY_EMB = 33
GLA_DK = 64
GLA_RANK = 16
GLA_TAU = 16.0
MOE_GROUPS = 4
MOE_EPG = 8
MOE_FF = 256
N_EXPERTS = MOE_GROUPS * MOE_EPG

GROUP_W = 512
OFF_HY = 0
OFF_GLA_Q = 1536
OFF_GLA_K = OFF_GLA_Q + GROUP_W
OFF_GLA_V = OFF_GLA_K + GROUP_W
OFF_GLA_G = OFF_GLA_V + GROUP_W
OFF_ML_Q = OFF_GLA_G + GROUP_W
OFF_ML_K = OFF_ML_Q + GROUP_W
OFF_ML_V = OFF_ML_K + GROUP_W
OFF_ML_O = OFF_ML_V + GROUP_W
OFF_RET_Q = OFF_ML_O + GROUP_W
OFF_RET_K = OFF_RET_Q + GROUP_W
OFF_RET_V = OFF_RET_K + GROUP_W
OFF_RET_G = OFF_RET_V + GROUP_W
N_PROJ = OFF_RET_G + GROUP_W
SMALL_BASE = GLA_DK
SMALL_R = SMALL_BASE
SMALL_MLG = SMALL_BASE + 2 * GLA_RANK

VMEM_LIMIT_BYTES = 56 * 1024 * 1024


def _cparams(*sem):
    return pltpu.CompilerParams(dimension_semantics=sem, vmem_limit_bytes=VMEM_LIMIT_BYTES)


def _pick(n, candidates):
    for c in candidates:
        if n % c == 0:
            return c
    return n


def _silu(x):
    return x / (1.0 + jnp.exp(-x))


def _sigmoid(x):
    return 1.0 / (1.0 + jnp.exp(-x))


def _log_sigmoid(x):
    return jnp.minimum(x, 0.0) - jnp.log(1.0 + jnp.exp(-jnp.abs(x)))


def _dot(a, b):
    return jnp.dot(a, b, preferred_element_type=F32)


def _dot_hi(a, b):
    return jnp.dot(a, b, preferred_element_type=F32, precision=HIGHEST)


def _split_bf16(x, parts):
    out = []
    for _ in range(parts - 1):
        hi = x.astype(BF16)
        out.append(hi)
        x = x - hi.astype(F32)
    out.append(x.astype(BF16))
    return out


def _dot_exact_lhs(a_bf16, x):
    return sum(_dot(a_bf16, p) for p in _split_bf16(x, 3))


def _pack_pairs(x):
    h = x.shape[1] // 2
    bits = lambda a: lax.bitcast_convert_type(a.astype(BF16).astype(F32), jnp.uint32)
    return bits(x[:, :h]) | (bits(x[:, h:]) >> 16)


def _unpack_pairs(w):
    hi = lax.bitcast_convert_type(w & jnp.uint32(0xFFFF0000), F32)
    lo = lax.bitcast_convert_type(w << 16, F32)
    return jnp.concatenate([hi, lo], axis=-1)


def _dot_nt(a, b):
    return lax.dot_general(a, b, (((1,), (1,)), ((), ())), preferred_element_type=F32)


def _dot_tn(a, b):
    return lax.dot_general(a, b, (((0,), (0,)), ((), ())), preferred_element_type=F32)


def _adaln_kernel(c_ref, w_ref, b_ref, o_ref):
    o_ref[0] = _dot_hi(_silu(c_ref[...]), w_ref[0]) + b_ref[0]


def adaln(cvec, ada_w, ada_b):
    depth, d, n = ada_w.shape
    tn = _pick(n, (1024, 512, 256, 128))
    return pl.pallas_call(
        _adaln_kernel,
        grid=(depth, n // tn),
        in_specs=[pl.BlockSpec((8, d), lambda l, j: (0, 0)),
                  pl.BlockSpec((1, d, tn), lambda l, j: (l, 0, j)),
                  pl.BlockSpec((1, 1, tn), lambda l, j: (l, 0, j))],
        out_specs=pl.BlockSpec((1, 8, tn), lambda l, j: (l, 0, j)),
        out_shape=jax.ShapeDtypeStruct((depth, 8, n), F32),
        compiler_params=_cparams("parallel", "parallel"),
        name="adaln",
    )(cvec, ada_w, ada_b.reshape(depth, 1, n))


def _norm_mod(x, g, sc, sh):
    ms = jnp.mean(x * x, axis=-1, keepdims=True)
    return (x * lax.rsqrt(ms + EPS) * g) * (1.0 + sc) + sh


def _inproj_kernel(x_ref, g_ref, sc_ref, sh_ref, w_ref, o_ref, gate_ref, xn_ref, *, gate_tile, gate_off):
    j = pl.program_id(2)

    @pl.when(j == 0)
    def _():
        tm = xn_ref.shape[0]
        rc = min(tm, NORM_ROWS)
        for r in range(0, tm, rc):
            xn = _norm_mod(x_ref[0, r:r + rc, :], g_ref[...], sc_ref[...], sh_ref[...]).astype(BF16)
            xn_ref[r:r + rc, :] = xn
            o_ref[0, r:r + rc, :] = _dot(xn, w_ref[...]).astype(BF16)

    @pl.when(j != 0)
    def _():
        acc = _dot(xn_ref[...], w_ref[...])
        o_ref[0] = acc.astype(BF16)

        @pl.when(j == gate_tile)
        def _():
            gate_ref[0] = acc[:, gate_off:gate_off + HEAD_W]


def inproj(x, g, mod4, row_of_batch, i_sc, i_sh, w):
    b, l, d = x.shape
    n = w.shape[1]
    tm = _pick(l, (1024, 512, 256, 128, 64))
    tn = _pick(n, (768, 512, 256, 128))
    assert OFF_GLA_K // tn > 0 and OFF_GLA_K % tn + HEAD_W <= tn
    kern = functools.partial(_inproj_kernel, gate_tile=OFF_GLA_K // tn, gate_off=OFF_GLA_K % tn)
    return pl.pallas_call(
        kern,
        grid=(b, l // tm, n // tn),
        in_specs=[pl.BlockSpec((1, tm, d), lambda bi, i, j: (bi, i, 0)),
                  pl.BlockSpec((1, d), lambda bi, i, j: (0, 0)),
                  pl.BlockSpec((None, None, 1, d), lambda bi, i, j: (row_of_batch(bi), i_sc, 0, 0)),
                  pl.BlockSpec((None, None, 1, d), lambda bi, i, j: (row_of_batch(bi), i_sh, 0, 0)),
                  pl.BlockSpec((d, tn), lambda bi, i, j: (0, j))],
        out_specs=[pl.BlockSpec((1, tm, tn), lambda bi, i, j: (bi, i, j)),
                   pl.BlockSpec((1, tm, HEAD_W), lambda bi, i, j: (bi, i, 0))],
        out_shape=[jax.ShapeDtypeStruct((b, l, n), BF16), jax.ShapeDtypeStruct((b, l, HEAD_W), F32)],
        scratch_shapes=[pltpu.VMEM((tm, d), BF16)],
        compiler_params=_cparams("parallel", "parallel", "arbitrary"),
        name="inproj",
    )(x, g.reshape(1, d), mod4, mod4, w)


def _causal_mask(n, reverse):
    r = lax.broadcasted_iota(jnp.int32, (n, n), 0)
    c = lax.broadcasted_iota(jnp.int32, (n, n), 1)
    same_chunk = (r // CHUNK) == (c // CHUNK)
    return jnp.logical_and(same_chunk, (c >= r) if reverse else (c <= r))


def _gla_scan_kernel(*refs, reverse, mode, rotate, accumulate, n_chunks, q_scale):
    it = iter(refs)
    q_ref, k_ref, v_ref = next(it), next(it), next(it)
    if mode == "gla":
        gsrc_ref, wah_ref, wal_ref, ba_ref = next(it), next(it), next(it), next(it)
    else:
        dec_ref = next(it)
    if rotate:
        cos_ref, sin_ref = next(it), next(it)
    s0_ref = next(it)
    if accumulate:
        oprev_ref = next(it)
    o_ref, sfin_ref, st_ref = next(it), next(it), next(it)
    dterm_refs = list(it)

    i = pl.program_id(1)

    @pl.when(i == 0)
    def _():
        st_ref[...] = s0_ref[0]

    tb = n_chunks * CHUNK
    mask = _causal_mask(tb, reverse)
    maskb = mask.astype(BF16)
    order = [(n_chunks - 1 - c) if reverse else c for c in range(n_chunks)]
    rows = [slice(c * CHUNK, (c + 1) * CHUNK) for c in range(n_chunks)]

    q = q_ref[0].astype(F32)
    k = k_ref[0].astype(F32)
    vb = v_ref[0]

    def decay_terms(la):
        bc = _dot_exact_lhs(maskb, la)
        tots = [bc[r.start:r.start + 1, :] if reverse else bc[r.stop - 1:r.stop, :] for r in rows]
        tot_rows = jnp.concatenate([jnp.broadcast_to(t, (CHUNK, GROUP_W)) for t in tots], axis=0)
        tot_pad = tots + [jnp.zeros_like(tots[0])] * (-n_chunks % 8)
        return jnp.exp(bc), jnp.exp(-bc), jnp.exp(tot_rows - bc), jnp.exp(jnp.concatenate(tot_pad, axis=0))

    if mode == "gla":
        g_hi, g_lo = _split_bf16(gsrc_ref[0], 2)
        logit = (_dot(g_hi, wah_ref[...]) + _dot(g_lo, wah_ref[...]) + _dot(g_hi, wal_ref[...])) + ba_ref[...]
        e_bc, e_nbc, e_st, e_tot = decay_terms(_log_sigmoid(logit) * (1.0 / GLA_TAU))
    else:
        @pl.when(i == 0)
        def _():
            terms = decay_terms(jnp.broadcast_to(_log_sigmoid(dec_ref[...]), (tb, GROUP_W)))
            for t_ref, t in zip(dterm_refs, terms):
                t_ref[...] = t

        e_bc, e_nbc, e_st, e_tot = (t_ref[...] for t_ref in dterm_refs)
    e_tots = [e_tot[c:c + 1, :] for c in range(n_chunks)]
    if rotate:
        cos2 = cos_ref[...]
        sin2 = sin_ref[...]

    outs = []
    for h in range(HEADS):
        hs = slice(h * HEAD_W, (h + 1) * HEAD_W)
        qh, kh, vh = q[:, hs], k[:, hs], vb[:, hs]
        if rotate:
            qh = qh * cos2 + pltpu.roll(qh, HEAD_W // 2, axis=1) * sin2
            kh = kh * cos2 + pltpu.roll(kh, HEAD_W // 2, axis=1) * sin2
        q_in = (qh * q_scale * e_bc[:, hs]).astype(BF16)
        k_in = (kh * e_nbc[:, hs]).astype(BF16)
        k_st = (kh * e_st[:, hs]).astype(BF16)
        att = jnp.where(mask, _dot_nt(q_in, k_in), 0.0).astype(BF16)
        intra = _dot(att, vh)
        st = st_ref[h]
        inter = [None] * n_chunks
        for cc in order:
            r = rows[cc]
            inter[cc] = _dot_nt(q_in[r], st.astype(BF16))
            st = st * e_tots[cc][:, hs] + _dot_tn(vh[r], k_st[r])
        st_ref[h] = st
        outs.append(intra + jnp.concatenate(inter, axis=0))
    o = jnp.concatenate(outs, axis=-1)
    if accumulate:
        o = o + oprev_ref[0]
    o_ref[0] = o

    @pl.when(i == pl.num_programs(1) - 1)
    def _():
        sfin_ref[0] = st_ref[...]


def gla_scan(p, gblock, off_q, off_k, off_v, s0, *, reverse, mode, gate=None, rot=None, o_prev=None, q_scale):
    b, l, _ = p.shape
    tb = _pick(l, (SCAN_BLOCK, 128, 64))
    nblk = l // tb
    blk = (lambda i: nblk - 1 - i) if reverse else (lambda i: i)
    gspec = lambda off: pl.BlockSpec((1, tb, GROUP_W), lambda bi, i: (bi, blk(i), off // GROUP_W))
    const = lambda shape: pl.BlockSpec(shape, lambda bi, i: (0,) * len(shape))
    in_specs = [gspec(off_q), gspec(off_k), gspec(off_v)]
    args = [p, p, p]
    if mode == "gla":
        wa_hi, wa_lo, ba = gate
        in_specs += [pl.BlockSpec((1, tb, HEAD_W), lambda bi, i: (bi, blk(i), 0)),
                     const((HEAD_W, GROUP_W)), const((HEAD_W, GROUP_W)), const((1, GROUP_W))]
        args += [gblock, wa_hi, wa_lo, ba]
    else:
        in_specs += [const((1, GROUP_W))]
        args += [gate]
    if rot is not None:
        in_specs += [pl.BlockSpec((tb, HEAD_W), lambda bi, i: (blk(i), 0))] * 2
        args += [rot[0], rot[1]]
    sspec = pl.BlockSpec((1, HEADS, HEAD_W, HEAD_W), lambda bi, i: (bi, 0, 0, 0))
    in_specs += [sspec]
    args += [s0]
    o_spec = pl.BlockSpec((1, tb, GROUP_W), lambda bi, i: (bi, blk(i), 0))
    aliases = {}
    if o_prev is not None:
        in_specs += [o_spec]
        aliases = {len(args): 0}
        args += [o_prev]
    kern = functools.partial(_gla_scan_kernel, reverse=reverse, mode=mode, rotate=rot is not None,
                             accumulate=o_prev is not None, n_chunks=tb // CHUNK, q_scale=q_scale)
    return pl.pallas_call(
        kern,
        grid=(b, nblk),
        in_specs=in_specs,
        out_specs=[o_spec, sspec],
        out_shape=[jax.ShapeDtypeStruct((b, l, GROUP_W), F32),
                   jax.ShapeDtypeStruct((b, HEADS, HEAD_W, HEAD_W), F32)],
        scratch_shapes=[pltpu.VMEM((HEADS, HEAD_W, HEAD_W), F32)] + (
            [pltpu.VMEM((tb, GROUP_W), F32)] * 3 + [pltpu.VMEM((8 * pl.cdiv(tb // CHUNK, 8), GROUP_W), F32)]
            if mode == "ret" else []),
        input_output_aliases=aliases,
        compiler_params=_cparams("parallel", "arbitrary"),
        name=f"{mode}_scan_{'bwd' if reverse else 'fwd'}",
    )(*args)


def _mlstm_kernel(*refs, reverse, accumulate, n_chunks, k_scale):
    it = iter(refs)
    q_ref, k_ref, v_ref, gsrc_ref, bias_ref = (next(it) for _ in range(5))
    c0_ref, n0_ref, m0_ref = next(it), next(it), next(it)
    if accumulate:
        oprev_ref = next(it)
    o_ref, cfin_ref, nfin_ref, mfin_ref = next(it), next(it), next(it), next(it)
    ct_ref, n_ref, m_ref = next(it), next(it), next(it)

    i = pl.program_id(1)

    @pl.when(i == 0)
    def _():
        ct_ref[...] = c0_ref[0]
        n_ref[...] = n0_ref[0]
        m_ref[...] = m0_ref[0]

    tb = n_chunks * CHUNK
    mask = _causal_mask(tb, reverse)
    maskb = mask.astype(BF16)
    order = [(n_chunks - 1 - c) if reverse else c for c in range(n_chunks)]
    rows = [slice(c * CHUNK, (c + 1) * CHUNK) for c in range(n_chunks)]
    lane_i = SMALL_MLG + (8 if reverse else 0)
    lane_f = lane_i + HEADS

    q = q_ref[0].astype(F32)
    k = k_ref[0].astype(F32) * k_scale
    vb = v_ref[0]
    gates = gsrc_ref[0] + bias_ref[...]
    bc_all = _dot_exact_lhs(maskb, _log_sigmoid(gates))
    rep = lambda a, c: jnp.broadcast_to(a[:, c:c + 1], (a.shape[0], HEAD_W))
    outs = []
    for h in range(HEADS):
        hs = slice(h * HEAD_W, (h + 1) * HEAD_W)
        qh, kh, vh = q[:, hs], k[:, hs], vb[:, hs]
        bch = rep(bc_all, lane_f + h)
        wh = rep(gates, lane_i + h) - bch
        bends = [bch[r.start:r.start + 1, :] if reverse else bch[r.stop - 1:r.stop, :] for r in rows]
        w_row = jnp.broadcast_to(jnp.transpose(wh)[0:1, :], (tb, tb))
        dmat = jnp.where(mask, bch[:, 0:1] + w_row, -jnp.inf)
        rowmax = jnp.max(dmat, axis=-1, keepdims=True)
        qb = qh.astype(BF16)
        s_raw = _dot_nt(qb, kh.astype(BF16)) * jnp.exp(dmat - rowmax)
        sv = _dot(s_raw.astype(BF16), vh)
        s_sum = jnp.sum(s_raw, axis=-1, keepdims=True)
        ct, nv, m_prev = ct_ref[h], n_ref[h], m_ref[h]
        out = [None] * n_chunks
        for cc in order:
            r = rows[cc]
            bend = bends[cc]
            glh = bend + wh[r]
            gmax = jnp.max(glh, axis=0, keepdims=True)
            gk = jnp.exp(glh - gmax) * kh[r]
            inter_log = bch[r] + m_prev
            m_t = jnp.maximum(inter_log, rowmax[r])
            e_intra = jnp.exp(rowmax[r] - m_t)
            inter = jnp.exp(inter_log - m_t)
            num = inter * _dot_nt(qb[r], ct.astype(BF16)) + e_intra * sv[r]
            den = jnp.abs(inter * jnp.sum(qh[r] * nv, axis=-1, keepdims=True) + e_intra * s_sum[r])
            out[cc] = num / jnp.maximum(den, jnp.exp(-m_t))
            m_new = jnp.maximum(bend + m_prev, gmax)
            dec = jnp.exp(bend + m_prev - m_new)
            e_upd = jnp.exp(gmax - m_new)
            ct = dec * ct + e_upd * _dot_tn(vh[r], gk.astype(BF16))
            nv = dec * nv + e_upd * jnp.sum(gk, axis=0, keepdims=True)
            m_prev = m_new
        ct_ref[h] = ct
        n_ref[h] = nv
        m_ref[h] = m_prev
        outs.append(jnp.concatenate(out, axis=0))
    o = jnp.concatenate(outs, axis=-1)
    if accumulate:
        o = o + oprev_ref[0]
    o_ref[0] = o

    @pl.when(i == pl.num_programs(1) - 1)
    def _():
        cfin_ref[0] = ct_ref[...]
        nfin_ref[0] = n_ref[...]
        mfin_ref[0] = m_ref[...]


def mlstm_scan(p, gblock, state, gate_bias, *, reverse, o_prev=None):
    b, l, _ = p.shape
    tb = _pick(l, (SCAN_BLOCK, 128, 64))
    nblk = l // tb
    blk = (lambda i: nblk - 1 - i) if reverse else (lambda i: i)
    c0, n0, m0 = state
    gspec = lambda off: pl.BlockSpec((1, tb, GROUP_W), lambda bi, i: (bi, blk(i), off // GROUP_W))
    const = lambda shape: pl.BlockSpec(shape, lambda bi, i: (0,) * len(shape))
    cspec = pl.BlockSpec((1, HEADS, HEAD_W, HEAD_W), lambda bi, i: (bi, 0, 0, 0))
    vspec = pl.BlockSpec((1, HEADS, 1, HEAD_W), lambda bi, i: (bi, 0, 0, 0))
    in_specs = [gspec(OFF_ML_Q), gspec(OFF_ML_K), gspec(OFF_ML_V),
                pl.BlockSpec((1, tb, HEAD_W), lambda bi, i: (bi, blk(i), 0)),
                const((1, HEAD_W)), cspec, vspec, vspec]
    args = [p, p, p, gblock, gate_bias, c0, n0, m0]
    o_spec = pl.BlockSpec((1, tb, GROUP_W), lambda bi, i: (bi, blk(i), 0))
    aliases = {}
    if o_prev is not None:
        in_specs += [o_spec]
        aliases = {len(args): 0}
        args += [o_prev]
    kern = functools.partial(_mlstm_kernel, reverse=reverse, accumulate=o_prev is not None,
                             n_chunks=tb // CHUNK, k_scale=HEAD_W ** -0.5)
    outs = pl.pallas_call(
        kern,
        grid=(b, nblk),
        in_specs=in_specs,
        out_specs=[o_spec, cspec, vspec, vspec],
        out_shape=[jax.ShapeDtypeStruct((b, l, GROUP_W), F32),
                   jax.ShapeDtypeStruct((b, HEADS, HEAD_W, HEAD_W), F32),
                   jax.ShapeDtypeStruct((b, HEADS, 1, HEAD_W), F32),
                   jax.ShapeDtypeStruct((b, HEADS, 1, HEAD_W), F32)],
        scratch_shapes=[pltpu.VMEM((HEADS, HEAD_W, HEAD_W), F32), pltpu.VMEM((HEADS, 1, HEAD_W), F32),
                        pltpu.VMEM((HEADS, 1, HEAD_W), F32)],
        input_output_aliases=aliases,
        compiler_params=_cparams("parallel", "arbitrary"),
        name=f"mlstm_scan_{'bwd' if reverse else 'fwd'}",
    )(*args)
    return outs[0], (outs[1], outs[2], outs[3])


def _hy_pre_kernel(u_ref, up_ref, un_ref, w_ref, b_ref, o_ref, ob_ref, y_ref, *, rows):
    i = pl.program_id(2)
    u = u_ref[0].astype(F32)
    prev_row = jnp.where(i == 0, 0.0, up_ref[0, HALO - 1:HALO, :].astype(F32))
    next_row = jnp.where(i == pl.num_programs(2) - 1, 0.0, un_ref[0, 0:1, :].astype(F32))
    ridx = lax.broadcasted_iota(jnp.int32, u.shape, 0)
    u_dn = jnp.where(ridx == 0, prev_row, pltpu.roll(u, 1, axis=0))
    u_up = jnp.where(ridx == rows - 1, next_row, pltpu.roll(u, rows - 1, axis=0))
    y = w_ref[0:1, :] * u_dn + w_ref[1:2, :] * u + w_ref[2:3, :] * u_up + b_ref[...]
    lane_tiles = HY_W // HEAD_W
    for j in range(lane_tiles):
        y_ref[j] = y[:, j * HEAD_W:(j + 1) * HEAD_W]
    for par in range(2):
        plane = jnp.concatenate([y_ref[j, pl.ds(par, rows // 2, stride=2), :] for j in range(lane_tiles)], axis=-1)
        o_ref[0, par] = plane
        ob_ref[0, par] = plane.astype(BF16)


def hy_pre(p, conv_w, conv_b):
    b, l, _ = p.shape
    rows = _pick(l, (512, 256, 128, 64))
    nr = l // rows
    r8 = rows // HALO
    n8 = l // HALO
    wpad = jnp.zeros((8, 3 * HY_W), F32).at[:3].set(conv_w)
    return pl.pallas_call(
        functools.partial(_hy_pre_kernel, rows=rows),
        grid=(b, 3, nr),
        in_specs=[pl.BlockSpec((1, rows, HY_W), lambda bi, j, i: (bi, i, j)),
                  pl.BlockSpec((1, HALO, HY_W), lambda bi, j, i: (bi, jnp.maximum(i * r8 - 1, 0), j)),
                  pl.BlockSpec((1, HALO, HY_W), lambda bi, j, i: (bi, jnp.minimum((i + 1) * r8, n8 - 1), j)),
                  pl.BlockSpec((8, HY_W), lambda bi, j, i: (0, j)),
                  pl.BlockSpec((1, HY_W), lambda bi, j, i: (0, j))],
        out_specs=[pl.BlockSpec((1, 2, rows // 2, HY_W), lambda bi, j, i: (bi, 0, i, j))] * 2,
        out_shape=[jax.ShapeDtypeStruct((b, 2, l // 2, 3 * HY_W), F32),
                   jax.ShapeDtypeStruct((b, 2, l // 2, 3 * HY_W), BF16)],
        scratch_shapes=[pltpu.VMEM((HY_W // HEAD_W, rows, HEAD_W), F32)],
        compiler_params=_cparams("parallel", "parallel", "parallel"),
        name="hy_shortconv",
    )(p, p, p, wpad, conv_b.reshape(1, 3 * HY_W))


def _mm_kernel(a_ref, b_ref, o_ref):
    o_ref[...] = _dot(a_ref[...], b_ref[...])


def matmul_bf16(a, bm):
    m, k = a.shape
    n = bm.shape[1]
    tm = _pick(m, (512, 256, 128, 64))
    tn = _pick(n, (512, 256, 128))
    return pl.pallas_call(
        _mm_kernel,
        grid=(n // tn, m // tm),
        in_specs=[pl.BlockSpec((tm, k), lambda j, i: (i, 0)),
                  pl.BlockSpec((k, tn), lambda j, i: (0, j))],
        out_specs=pl.BlockSpec((tm, tn), lambda j, i: (i, j)),
        out_shape=jax.ShapeDtypeStruct((m, n), F32),
        compiler_params=_cparams("parallel", "parallel"),
        name="matmul_bf16",
    )(a, bm)


def _dft_fwd_kernel(c_ref, s_ref, z_ref, cp_ref, sp_ref, hr_ref, hi_ref, hbr_ref, hbi_ref, ur_ref, ui_ref):
    c = c_ref[...]
    s = s_ref[...]
    z0 = z_ref[0, 0]
    z1 = z_ref[0, 1]
    e0r, e0i = _dot(c, z0), -_dot(s, z0)
    e1r, e1i = _dot(c, z1), -_dot(s, z1)
    cp = cp_ref[...]
    sp = sp_ref[...]
    t1r = cp * e1r + sp * e1i
    t1i = cp * e1i - sp * e1r
    ar, ai, br, bi = e0r + t1r, e0i + t1i, e0r - t1r, e0i - t1i
    hr, hi, hbr, hbi = hr_ref[...], hi_ref[...], hbr_ref[...], hbi_ref[...]
    pr = ar * hr - ai * hi
    pim = ar * hi + ai * hr
    qr = br * hbr - bi * hbi
    qim = br * hbi + bi * hbr
    wr, wim = pr - qr, pim - qim
    ur_ref[0, 0] = (pr + qr).astype(BF16)
    ui_ref[0, 0] = (pim + qim).astype(BF16)
    ur_ref[0, 1] = (cp * wr - sp * wim).astype(BF16)
    ui_ref[0, 1] = (sp * wr + cp * wim).astype(BF16)


def dft_fwd(cmat, smat, zb, z_col, cpsi, spsi, h4, h_col):
    b, _, m, _ = zb.shape
    tf = _pick(m, (512, 256, 128, 64))
    hspec = pl.BlockSpec((tf, HY_W), lambda i, bi: (i, h_col))
    vspec = pl.BlockSpec((tf, 1), lambda i, bi: (i, 0))
    ospec = pl.BlockSpec((1, 2, tf, HY_W), lambda i, bi: (bi, 0, i, 0))
    return pl.pallas_call(
        _dft_fwd_kernel,
        grid=(m // tf, b),
        in_specs=[pl.BlockSpec((tf, m), lambda i, bi: (i, 0)),
                  pl.BlockSpec((tf, m), lambda i, bi: (i, 0)),
                  pl.BlockSpec((1, 2, m, HY_W), lambda i, bi: (bi, 0, 0, z_col)),
                  vspec, vspec, hspec, hspec, hspec, hspec],
        out_specs=[ospec, ospec],
        out_shape=[jax.ShapeDtypeStruct((b, 2, m, HY_W), BF16)] * 2,
        compiler_params=_cparams("parallel", "parallel"),
        name="hy_dft_fwd",
    )(cmat, smat, zb, cpsi, spsi, *h4)


def _dft_inv_kernel(c_ref, s_ref, ur_ref, ui_ref, x_ref, zp_ref, skip_ref, *o_refs, interleave):
    c = c_ref[...]
    s = s_ref[...]
    tt = c.shape[0]
    lane_tiles = HY_W // HEAD_W
    for par in range(2):
        conv = _dot(c, ur_ref[0, par]) - _dot(s, ui_ref[0, par])
        z = x_ref[0, par] * (conv + skip_ref[...] * zp_ref[0, par])
        if interleave:
            for j in range(lane_tiles):
                o_refs[1][j, pl.ds(par, tt, stride=2), :] = z[:, j * HEAD_W:(j + 1) * HEAD_W]
        else:
            o_refs[0][0, par] = z
            o_refs[1][0, par] = z.astype(BF16)
    if interleave:
        o_refs[0][0] = jnp.concatenate([o_refs[1][j] for j in range(lane_tiles)], axis=-1)


def dft_inv(cmat, smat, ur, ui, x_arr, x_col, zp_arr, zp_col, skip, interleave):
    b, _, m, _ = ur.shape
    tt = _pick(m, (512, 256, 128, 64))
    uspec = pl.BlockSpec((1, 2, m, HY_W), lambda i, bi: (bi, 0, 0, 0))
    pspec = lambda col: pl.BlockSpec((1, 2, tt, HY_W), lambda i, bi: (bi, 0, i, col))
    if interleave:
        out_specs = [pl.BlockSpec((1, 2 * tt, HY_W), lambda i, bi: (bi, i, 0))]
        out_shape = [jax.ShapeDtypeStruct((b, 2 * m, HY_W), F32)]
    else:
        out_specs = [pspec(0), pspec(0)]
        out_shape = [jax.ShapeDtypeStruct((b, 2, m, HY_W), F32), jax.ShapeDtypeStruct((b, 2, m, HY_W), BF16)]
    return pl.pallas_call(
        functools.partial(_dft_inv_kernel, interleave=interleave),
        grid=(m // tt, b),
        in_specs=[pl.BlockSpec((tt, m), lambda i, bi: (i, 0)),
                  pl.BlockSpec((tt, m), lambda i, bi: (i, 0)),
                  uspec, uspec, pspec(x_col), pspec(zp_col),
                  pl.BlockSpec((1, HY_W), lambda i, bi: (0, 0))],
        out_specs=out_specs,
        out_shape=out_shape,
        scratch_shapes=[pltpu.VMEM((HY_W // HEAD_W, 2 * tt, HEAD_W), F32)] if interleave else [],
        compiler_params=_cparams("parallel", "parallel"),
        name="hy_dft_inv",
    )(cmat, smat, ur, ui, x_arr, zp_arr, skip.reshape(1, HY_W))


def dft_tables(l):
    r = _pick(l, (64, 32, 16, 8))
    period = 8 * l
    odd = 2 * jnp.arange(l, dtype=jnp.int32) + 1
    s1 = jnp.arange(l // r, dtype=jnp.int32) * (2 * r)
    s0 = 2 * jnp.arange(r, dtype=jnp.int32) + 1
    ang = lambda ph: ph.astype(F32) * (2.0 * math.pi / period)
    a = ang((odd[:, None] * s1[None, :]) % period)
    bb = ang((odd[:, None] * s0[None, :]) % period)
    ca, sa, cb, sb = jnp.cos(a), jnp.sin(a), jnp.cos(bb), jnp.sin(bb)
    cmat = (ca[:, :, None] * cb[:, None, :] - sa[:, :, None] * sb[:, None, :]).reshape(l, l).astype(BF16)
    smat = (sa[:, :, None] * cb[:, None, :] + ca[:, :, None] * sb[:, None, :]).reshape(l, l).astype(BF16)
    phi = ang(odd)[:, None]
    return cmat, smat, jnp.cos(phi), jnp.sin(phi)


def hyena_filter_taps(length, lp):
    pos = jnp.arange(length, dtype=F32)
    t = pos / (length - 1)
    bands = (HY_EMB - 1) // 2
    fr = jnp.linspace(1e-4, bands - 1, bands, dtype=F32)
    ang = (2.0 * math.pi / length) * pos[:, None] * fr[None, :]
    z = jnp.concatenate([t[:, None], jnp.cos(ang), -jnp.sin(ang)], axis=-1)
    mm = functools.partial(jnp.matmul, precision=HIGHEST)
    hdn = jnp.sin(lp['hy_f_freq'][0] * (mm(z, lp['hy_f_w1']) + lp['hy_f_b1']))
    hdn = jnp.sin(lp['hy_f_freq'][1] * (mm(hdn, lp['hy_f_w2']) + lp['hy_f_b2']))
    return mm(hdn, lp['hy_f_w3']) * jnp.exp(-t[:, None] * jnp.abs(lp['hy_decay']))


def hyena_mixer(p, lp, tables):
    b, l, _ = p.shape
    cmat, smat, cpsi, spsi = tables
    uc, ucb = hy_pre(p, lp['hy_conv_w'], lp['hy_conv_b'])
    w2 = 2 * HY_W
    taps = hyena_filter_taps(l, lp).reshape(l, 2, 2, HY_W)
    hf = taps[:, :, 0, :].reshape(l, w2)
    hb = jnp.concatenate([taps[:, :, 1, :].reshape(l, w2), jnp.zeros((1, w2), F32)], axis=0)
    gf0, gf1 = hf[0::2], hf[1::2]
    gb0, gb1 = hb[2::2], hb[1:l:2]
    gr_s = matmul_bf16(cmat, jnp.concatenate([gf0 + gb0, gf1 + gb1], axis=1).astype(BF16))
    gi_s = -matmul_bf16(smat, jnp.concatenate([gf0 - gb0, gf1 - gb1], axis=1).astype(BF16))
    gr = cpsi * gr_s - spsi * gi_s
    gi = spsi * gr_s + cpsi * gi_s
    g0r, g1r, g0i, g1i = gr[:, :w2], gr[:, w2:], gi[:, :w2], gi[:, w2:]
    tr = cpsi * g1r + spsi * g1i
    ti = cpsi * g1i - spsi * g1r
    scale = 1.0 / l
    h4 = ((g0r + tr) * scale, (g0i + ti) * scale,
          (g0r - tr) * scale, (g0i - ti) * scale)
    ur, ui = dft_fwd(cmat, smat, ucb, 0, cpsi, spsi, h4, 0)
    z1, z1b = dft_inv(cmat, smat, ur, ui, uc, 1, uc, 0, lp['hy_skip'][0], interleave=False)
    ur, ui = dft_fwd(cmat, smat, z1b, 0, cpsi, spsi, h4, 1)
    (z2,) = dft_inv(cmat, smat, ur, ui, uc, 2, z1, 0, lp['hy_skip'][1], interleave=True)
    return z2


def _head_rms(y):
    parts = []
    for h in range(HEADS):
        yh = y[:, h * HEAD_W:(h + 1) * HEAD_W]
        parts.append(yh * lax.rsqrt(jnp.mean(yh * yh, axis=-1, keepdims=True) + EPS))
    return jnp.concatenate(parts, axis=-1)


def _outproj_kernel(hy_ref, gla_ref, glag_ref, ml_ref, mlo_ref, ret_ref, retg_ref,
                    gn_ref, w_ref, res_ref, gate_ref, o_ref):
    gn = gn_ref[...]
    tm = o_ref.shape[1]
    rc = min(tm, NORM_ROWS)
    for r0 in range(0, tm, rc):
        r = slice(r0, r0 + rc)
        y_gla = _head_rms(gla_ref[0, r, :]) * gn[0:1, :] * _silu(glag_ref[0, r, :].astype(F32))
        y_ml = _head_rms(_sigmoid(mlo_ref[0, r, :].astype(F32)) * ml_ref[0, r, :]) * gn[1:2, :]
        y_ret = _head_rms(ret_ref[0, r, :]) * gn[2:3, :] * _silu(retg_ref[0, r, :].astype(F32))
        acc = _dot(hy_ref[0, r, :].astype(BF16), w_ref[0:GROUP_W, :])
        acc += _dot(y_gla.astype(BF16), w_ref[GROUP_W:2 * GROUP_W, :])
        acc += _dot(y_ml.astype(BF16), w_ref[2 * GROUP_W:3 * GROUP_W, :])
        acc += _dot(y_ret.astype(BF16), w_ref[3 * GROUP_W:4 * GROUP_W, :])
        o_ref[0, r, :] = res_ref[0, r, :] + gate_ref[...] * acc


def outproj(hy, o_gla, o_ml, o_ret, p, gains, w_out, res, mod4, row_of_batch, i_gate):
    b, l, d = res.shape
    tm = _pick(l, (256, 128, 64))
    gw = GROUP_W
    ospec = pl.BlockSpec((1, tm, gw), lambda bi, i: (bi, i, 0))
    pspec = lambda off: pl.BlockSpec((1, tm, gw), lambda bi, i: (bi, i, off // gw))
    return pl.pallas_call(
        _outproj_kernel,
        grid=(b, l // tm),
        in_specs=[ospec, ospec, pspec(OFF_GLA_G), ospec, pspec(OFF_ML_O), ospec, pspec(OFF_RET_G),
                  pl.BlockSpec((8, gw), lambda bi, i: (0, 0)),
                  pl.BlockSpec((4 * gw, d), lambda bi, i: (0, 0)),
                  pl.BlockSpec((1, tm, d), lambda bi, i: (bi, i, 0)),
                  pl.BlockSpec((None, None, 1, d), lambda bi, i: (row_of_batch(bi), i_gate, 0, 0))],
        out_specs=pl.BlockSpec((1, tm, d), lambda bi, i: (bi, i, 0)),
        out_shape=jax.ShapeDtypeStruct((b, l, d), F32),
        compiler_params=_cparams("parallel", "parallel"),
        name="outproj",
    )(hy, o_gla, p, o_ml, p, o_ret, p, gains, w_out, res, mod4)


def _moe_pre_kernel(x_ref, g_ref, sc_ref, sh_ref, wr_ref, br_ref, t_ref, lg_ref):
    tm = x_ref.shape[1]
    rc = min(tm, NORM_ROWS)
    for r0 in range(0, tm, rc):
        r = slice(r0, r0 + rc)
        t = _norm_mod(x_ref[0, r, :], g_ref[...], sc_ref[...], sh_ref[...])
        t_ref[0, r, :] = _pack_pairs(t)
        lg_ref[0, r, :] = _dot_hi(t, wr_ref[...]) + br_ref[...]


def moe_pre(x, g, mod4, row_of_batch, i_sc, i_sh, w_router, b_router):
    b, l, d = x.shape
    tm = _pick(l, (512, 256, 128, 64))
    nr = w_router.shape[1]
    return pl.pallas_call(
        _moe_pre_kernel,
        grid=(b, l // tm),
        in_specs=[pl.BlockSpec((1, tm, d), lambda bi, i: (bi, i, 0)),
                  pl.BlockSpec((1, d), lambda bi, i: (0, 0)),
                  pl.BlockSpec((None, None, 1, d), lambda bi, i: (row_of_batch(bi), i_sc, 0, 0)),
                  pl.BlockSpec((None, None, 1, d), lambda bi, i: (row_of_batch(bi), i_sh, 0, 0)),
                  pl.BlockSpec((d, nr), lambda bi, i: (0, 0)),
                  pl.BlockSpec((1, nr), lambda bi, i: (0, 0))],
        out_specs=[pl.BlockSpec((1, tm, d // 2), lambda bi, i: (bi, i, 0)),
                   pl.BlockSpec((1, tm, nr), lambda bi, i: (bi, i, 0))],
        out_shape=[jax.ShapeDtypeStruct((b, l, d // 2), jnp.uint32), jax.ShapeDtypeStruct((b, l, nr), F32)],
        compiler_params=_cparams("parallel", "parallel"),
        name="moe_pre",
    )(x, g.reshape(1, d), mod4, mod4, w_router, b_router)


ROUTE_E1, ROUTE_E2, ROUTE_R1, ROUTE_R2, ROUTE_W1, ROUTE_W2 = range(6)


def _moe_route_kernel(lg_ref, route_ref, cnt_ref, run_ref, *, tr):
    i = pl.program_id(0)

    @pl.when(i == 0)
    def _():
        run_ref[...] = jnp.zeros_like(run_ref)

    lg = lg_ref[...]
    lane = lax.broadcasted_iota(jnp.int32, lg.shape, 1).astype(F32)
    no_lane = float(HEAD_W)
    neg = -jnp.inf
    first_max = lambda vals, vmax: jnp.min(jnp.where(vals == vmax, lane, no_lane), axis=-1, keepdims=True)
    gl = jnp.where(lane < MOE_GROUPS, lg, neg)
    gmax = jnp.max(gl, axis=-1, keepdims=True)
    pg = 1.0 / jnp.sum(jnp.exp(gl - gmax), axis=-1, keepdims=True)
    lo = MOE_GROUPS + first_max(gl, gmax) * MOE_EPG
    sel = jnp.where(jnp.logical_and(lane >= lo, lane < lo + MOE_EPG), lg, neg)
    v1 = jnp.max(sel, axis=-1, keepdims=True)
    i1 = first_max(sel, v1)
    sel2 = jnp.where(lane == i1, neg, sel)
    v2 = jnp.max(sel2, axis=-1, keepdims=True)
    i2 = first_max(sel2, v2)
    e21 = jnp.exp(v2 - v1)
    w1 = pg / (1.0 + e21)
    w2 = w1 * e21
    e1 = i1 - MOE_GROUPS
    e2 = i2 - MOE_GROUPS
    oh1 = (lane == e1).astype(F32)
    oh2 = (lane == e2).astype(F32)
    both = oh1 + oh2
    row = lax.broadcasted_iota(jnp.int32, (tr, tr), 0)
    col = lax.broadcasted_iota(jnp.int32, (tr, tr), 1)
    earlier = (col < row).astype(BF16)
    before = _dot(earlier, both.astype(BF16)) + run_ref[...]
    r1 = jnp.sum(before * oh1, axis=-1, keepdims=True)
    r2 = jnp.sum(before * oh2, axis=-1, keepdims=True)
    run_ref[...] += jnp.sum(both, axis=0, keepdims=True)
    rec = jnp.zeros_like(lg)
    for k, val in ((ROUTE_E1, e1), (ROUTE_E2, e2), (ROUTE_R1, r1), (ROUTE_R2, r2), (ROUTE_W1, w1), (ROUTE_W2, w2)):
        rec = jnp.where(lane == k, val, rec)
    route_ref[...] = rec

    @pl.when(i == pl.num_programs(0) - 1)
    def _():
        cnt_ref[...] = run_ref[...]


def moe_route(logits):
    t, nl = logits.shape
    tr = _pick(t, (512, 256, 128, 64))
    return pl.pallas_call(
        functools.partial(_moe_route_kernel, tr=tr),
        grid=(t // tr,),
        in_specs=[pl.BlockSpec((tr, nl), lambda i: (i, 0))],
        out_specs=[pl.BlockSpec((tr, nl), lambda i: (i, 0)), pl.BlockSpec((1, nl), lambda i: (0, 0))],
        out_shape=[jax.ShapeDtypeStruct((t, nl), F32), jax.ShapeDtypeStruct((1, nl), F32)],
        scratch_shapes=[pltpu.VMEM((1, nl), F32)],
        compiler_params=_cparams("arbitrary"),
        name="moe_route",
    )(logits)


def _moe_pos_kernel(route_ref, pstart_ref, pos_ref):
    rec = route_ref[...]
    lane = lax.broadcasted_iota(jnp.int32, rec.shape, 1).astype(F32)
    start = lambda e: jnp.sum(jnp.where(lane == e, pstart_ref[...], 0.0), axis=-1, keepdims=True)
    p1 = start(rec[:, ROUTE_E1:ROUTE_E1 + 1]) + rec[:, ROUTE_R1:ROUTE_R1 + 1]
    p2 = start(rec[:, ROUTE_E2:ROUTE_E2 + 1]) + rec[:, ROUTE_R2:ROUTE_R2 + 1]
    pos_ref[...] = jnp.where(lane == 0, p1, jnp.where(lane == 1, p2, 0.0))


def moe_pos(route, pstart_row):
    t, nl = route.shape
    tr = _pick(t, (512, 256, 128, 64))
    return pl.pallas_call(
        _moe_pos_kernel,
        grid=(t // tr,),
        in_specs=[pl.BlockSpec((tr, nl), lambda i: (i, 0)), pl.BlockSpec((1, nl), lambda i: (0, 0))],
        out_specs=pl.BlockSpec((tr, nl), lambda i: (i, 0)),
        out_shape=jax.ShapeDtypeStruct((t, nl), F32),
        compiler_params=_cparams("parallel"),
        name="moe_pos",
    )(route, pstart_row)


ISSUE_UNROLL = 8


def _row_copy_wait(src_rows, dst_rows, sem):
    pltpu.make_async_copy(src_rows, dst_rows, sem).wait()


def _moe_dispatch_kernel(pend_ref, padded_ref, pos_ref, tok_ref, xs_ref, zero_ref, sem, *, tr, tm):
    @pl.when(pl.program_id(0) == 0)
    def _():
        zero_ref[...] = jnp.zeros_like(zero_ref)

        def clear(e, carry):
            @pl.when(padded_ref[e] > 0)
            def _():
                dst = xs_ref.at[pl.ds(pl.multiple_of(pend_ref[e] - tm, tm), tm)]
                cp = pltpu.make_async_copy(zero_ref, dst, sem)
                cp.start()
                cp.wait()
            return carry

        lax.fori_loop(0, N_EXPERTS, clear, 0)

    def issue(r, carry):
        for slot in range(2):
            dst = pos_ref[0, 0, 2 * r + slot]
            pltpu.make_async_copy(tok_ref.at[pl.ds(r, 1)], xs_ref.at[pl.ds(dst, 1)], sem).start()
        return carry

    lax.fori_loop(0, tr, issue, 0, unroll=ISSUE_UNROLL)
    for _ in range(2):
        _row_copy_wait(tok_ref, xs_ref.at[pl.ds(0, tr)], sem)


def moe_dispatch(tok, pos, pend, padded, pmax, tm):
    t, d = tok.shape
    tr = pos.shape[2] // 2
    grid_spec = pltpu.PrefetchScalarGridSpec(
        num_scalar_prefetch=2,
        grid=(t // tr,),
        in_specs=[pl.BlockSpec((1, 1, 2 * tr), lambda i, pe, pa: (i, 0, 0), memory_space=pltpu.SMEM),
                  pl.BlockSpec((tr, d), lambda i, pe, pa: (i, 0))],
        out_specs=pl.BlockSpec(memory_space=pl.ANY),
        scratch_shapes=[pltpu.VMEM((tm, d), tok.dtype), pltpu.SemaphoreType.DMA],
    )
    return pl.pallas_call(
        functools.partial(_moe_dispatch_kernel, tr=tr, tm=tm),
        grid_spec=grid_spec,
        out_shape=jax.ShapeDtypeStruct((pmax, d), tok.dtype),
        compiler_params=_cparams("arbitrary"),
        name="moe_dispatch",
    )(pend, padded, pos, tok)


def _moe_expert_kernel(te_ref, nu_ref, x_ref, w1_ref, w3_ref, w2_ref, o_ref, w1b_ref, w3b_ref, w2b_ref):
    t = pl.program_id(0)

    @pl.when(t < nu_ref[0])
    def _():
        @pl.when(jnp.logical_or(t == 0, te_ref[t] != te_ref[jnp.maximum(t - 1, 0)]))
        def _():
            w1b_ref[...] = w1_ref[0].astype(BF16)
            w3b_ref[...] = w3_ref[0].astype(BF16)
            w2b_ref[...] = w2_ref[0].astype(BF16)

        x = _unpack_pairs(x_ref[...]).astype(BF16)
        act = _silu(_dot(x, w1b_ref[...])) * _dot(x, w3b_ref[...])
        o_ref[...] = _pack_pairs(_dot(act.astype(BF16), w2b_ref[...]))


def moe_experts(x_sorted, w1, w3, w2, tile_expert, n_used, tm):
    pmax, dp = x_sorted.shape
    d = 2 * dp
    ff = w1.shape[2]
    used = lambda t, nu: jnp.minimum(t, nu[0] - 1)
    grid_spec = pltpu.PrefetchScalarGridSpec(
        num_scalar_prefetch=2,
        grid=(pmax // tm,),
        in_specs=[pl.BlockSpec((tm, dp), lambda t, te, nu: (used(t, nu), 0)),
                  pl.BlockSpec((1, d, ff), lambda t, te, nu: (te[t], 0, 0)),
                  pl.BlockSpec((1, d, ff), lambda t, te, nu: (te[t], 0, 0)),
                  pl.BlockSpec((1, ff, d), lambda t, te, nu: (te[t], 0, 0))],
        out_specs=pl.BlockSpec((tm, dp), lambda t, te, nu: (used(t, nu), 0)),
        scratch_shapes=[pltpu.VMEM((d, ff), BF16), pltpu.VMEM((d, ff), BF16), pltpu.VMEM((ff, d), BF16)],
    )
    return pl.pallas_call(
        _moe_expert_kernel,
        grid_spec=grid_spec,
        out_shape=jax.ShapeDtypeStruct((pmax, dp), jnp.uint32),
        compiler_params=_cparams("arbitrary"),
        name="moe_experts",
    )(tile_expert, n_used, x_sorted, w1, w3, w2)


def _moe_combine_kernel(p1_ref, p2_ref, p1n_ref, p2n_ref, ys_ref, res_ref, route_ref, gate_ref, o_ref,
                        y1_ref, y2_ref, sem, *, tc):
    i = pl.program_id(0)
    slot = i % 2

    def issue(pa_ref, pb_ref, s):
        def body(r, carry):
            pltpu.make_async_copy(ys_ref.at[pl.ds(pa_ref[0, 0, r], 1)], y1_ref.at[s, pl.ds(r, 1)], sem.at[s]).start()
            pltpu.make_async_copy(ys_ref.at[pl.ds(pb_ref[0, 0, r], 1)], y2_ref.at[s, pl.ds(r, 1)], sem.at[s]).start()
            return carry
        lax.fori_loop(0, tc, body, 0, unroll=ISSUE_UNROLL)

    @pl.when(i == 0)
    def _():
        issue(p1_ref, p2_ref, 0)

    @pl.when(i + 1 < pl.num_programs(0))
    def _():
        issue(p1n_ref, p2n_ref, 1 - slot)

    _row_copy_wait(ys_ref.at[pl.ds(0, tc)], y1_ref.at[slot], sem.at[slot])
    _row_copy_wait(ys_ref.at[pl.ds(0, tc)], y2_ref.at[slot], sem.at[slot])
    rec = route_ref[...]
    w1 = rec[:, ROUTE_W1:ROUTE_W1 + 1]
    w2 = rec[:, ROUTE_W2:ROUTE_W2 + 1]
    y = w1 * _unpack_pairs(y1_ref[slot]) + w2 * _unpack_pairs(y2_ref[slot])
    o_ref[...] = res_ref[...] + gate_ref[...] * y


def moe_combine(res, y_sorted, pos1, pos2, route, mod4, row_of_tile, i_gate):
    t, d = res.shape
    n, _, tc = pos1.shape
    cur = lambda i: (i, 0, 0)
    nxt = lambda i: (jnp.minimum(i + 1, n - 1), 0, 0)
    smem = lambda im: pl.BlockSpec((1, 1, tc), im, memory_space=pltpu.SMEM)
    return pl.pallas_call(
        functools.partial(_moe_combine_kernel, tc=tc),
        grid=(n,),
        in_specs=[smem(cur), smem(cur), smem(nxt), smem(nxt),
                  pl.BlockSpec(memory_space=pl.ANY),
                  pl.BlockSpec((tc, d), lambda i: (i, 0)),
                  pl.BlockSpec((tc, route.shape[1]), lambda i: (i, 0)),
                  pl.BlockSpec((None, None, 1, d), lambda i: (row_of_tile(i), i_gate, 0, 0))],
        out_specs=pl.BlockSpec((tc, d), lambda i: (i, 0)),
        out_shape=jax.ShapeDtypeStruct((t, d), F32),
        scratch_shapes=[pltpu.VMEM((2, tc, d // 2), y_sorted.dtype), pltpu.VMEM((2, tc, d // 2), y_sorted.dtype),
                        pltpu.SemaphoreType.DMA((2,))],
        compiler_params=_cparams("arbitrary"),
        name="moe_combine",
    )(pos1, pos2, pos1, pos2, y_sorted, res, route, mod4)


def hier_moe(x, lp, mod4, row_of_batch):
    b, l, d = x.shape
    t = b * l
    tm = 256 if t >= 4096 else 64
    tok, logits = moe_pre(x, lp['norm2_g'], mod4, row_of_batch, 4, 3, lp['w_router'], lp['b_router'])
    tok = tok.reshape(t, d // 2)
    route, cnt = moe_route(logits.reshape(t, -1))
    counts = cnt[0, :N_EXPERTS].astype(jnp.int32)
    padded = ((counts + tm - 1) // tm) * tm
    pend = jnp.cumsum(padded)
    pstart = pend - padded
    pmax = 2 * t + N_EXPERTS * tm
    tile_row = jnp.arange(pmax // tm, dtype=jnp.int32) * tm
    tile_expert = jnp.minimum(jnp.sum(tile_row[:, None] >= pend[None, :], axis=1), N_EXPERTS - 1).astype(jnp.int32)
    n_used = (pend[-1] // tm).astype(jnp.int32).reshape(1)
    pstart_row = jnp.zeros((1, route.shape[1]), F32).at[0, :N_EXPERTS].set(pstart.astype(F32))
    pos12 = moe_pos(route, pstart_row)[:, :2].astype(jnp.int32)
    pos1, pos2 = pos12[:, 0], pos12[:, 1]
    tr = _pick(t, (512, 256, 128, 64))
    tc = _pick(l, (256, 128, 64))
    pos = pos12.reshape(t // tr, 1, 2 * tr)
    x_sorted = moe_dispatch(tok, pos, pend, padded, pmax, tm)
    y_sorted = moe_experts(x_sorted, lp['moe_w1e'], lp['moe_w3e'], lp['moe_w2e'],
                           tile_expert + lp['expert_base'], n_used, tm)
    tiles_per_batch = l // tc
    out = moe_combine(x.reshape(t, d), y_sorted, pos1.reshape(t // tc, 1, tc), pos2.reshape(t // tc, 1, tc),
                      route, mod4, lambda i: row_of_batch(i // tiles_per_batch), 5)
    return out.reshape(b, l, d)


def _final_norm_kernel(x_ref, g_ref, o_ref):
    x = x_ref[0]
    o_ref[0] = x * lax.rsqrt(jnp.mean(x * x, axis=-1, keepdims=True) + EPS) * g_ref[...]


def final_norm(x, g):
    b, l, d = x.shape
    tm = _pick(l, (512, 256, 128, 64))
    spec = pl.BlockSpec((1, tm, d), lambda bi, i: (bi, i, 0))
    return pl.pallas_call(
        _final_norm_kernel,
        grid=(b, l // tm),
        in_specs=[spec, pl.BlockSpec((1, d), lambda bi, i: (0, 0))],
        out_specs=spec,
        out_shape=jax.ShapeDtypeStruct((b, l, d), F32),
        compiler_params=_cparams("parallel", "parallel"),
        name="final_norm",
    )(x, g.reshape(1, d))


def _pad_heads(w, dk):
    d = w.shape[0]
    return jnp.pad(w.reshape(d, HEADS, dk), ((0, 0), (0, 0), (0, HEAD_W - dk))).reshape(d, HEADS * HEAD_W)


def prep_w_in(w_in):
    o = 0
    take = lambda n: (w_in[:, o:o + n], o + n)
    hy, o = take(3 * HY_W)
    gq, o = take(HEADS * GLA_DK)
    gk, o = take(HEADS * GLA_DK)
    gv, o = take(GROUP_W)
    gg, o = take(GROUP_W)
    gr, o = take(2 * GLA_RANK)
    ml, o = take(4 * GROUP_W)
    mg, o = take(16)
    ret, o = take(4 * GROUP_W)
    gk_p = _pad_heads(gk, GLA_DK)
    gk_p = gk_p.at[:, SMALL_R:SMALL_R + 2 * GLA_RANK].set(gr).at[:, SMALL_MLG:SMALL_MLG + 16].set(mg)
    return jnp.concatenate([hy, _pad_heads(gq, GLA_DK), gk_p, gv, gg, ml, ret], axis=1).astype(BF16)


def prep_layer(l, prm):
    lp = {k: v[l] for k, v in prm.items()}
    lp['w_in_p'] = prep_w_in(lp['w_in'])
    lp['w_out_b'] = lp['w_out'].astype(BF16)
    gates = []
    for d in range(2):
        w = jnp.zeros((HEAD_W, HEADS * HEAD_W), F32)
        w = w.at[SMALL_R + d * GLA_RANK:SMALL_R + (d + 1) * GLA_RANK].set(_pad_heads(lp['gla_wa2'][d], GLA_DK))
        w_hi = w.astype(BF16)
        w_lo = (w - w_hi.astype(F32)).astype(BF16)
        gates.append((w_hi, w_lo, _pad_heads(lp['gla_ba'][d][None, :], GLA_DK)))
    lp['gla_gate'] = gates
    bias = jnp.zeros((1, HEAD_W), F32).at[0, SMALL_MLG:SMALL_MLG + 16].set(lp['ml_gate_b'].reshape(-1))
    lp['ml_bias'] = bias
    lp['ret_dec'] = [jnp.repeat(lp['ret_decay'][d], HEAD_W)[None, :] for d in range(2)]
    gains = jnp.zeros((8, GROUP_W), F32)
    lp['gains'] = gains.at[0].set(lp['gla_norm_g']).at[1].set(lp['ml_norm_g']).at[2].set(lp['ret_norm_g'])
    d_model = lp['moe_wg'].shape[0]
    wr = jnp.zeros((d_model, HEAD_W), F32)
    lp['w_router'] = wr.at[:, :MOE_GROUPS].set(lp['moe_wg']).at[:, MOE_GROUPS:MOE_GROUPS + N_EXPERTS].set(lp['moe_we'])
    br = jnp.zeros((1, HEAD_W), F32)
    lp['b_router'] = br.at[0, :MOE_GROUPS].set(lp['moe_bg']).at[0, MOE_GROUPS:MOE_GROUPS + N_EXPERTS].set(lp['moe_be'])
    lp['moe_w1e'] = prm['moe_w1'].reshape(-1, d_model, MOE_FF)
    lp['moe_w3e'] = prm['moe_w3'].reshape(-1, d_model, MOE_FF)
    lp['moe_w2e'] = prm['moe_w2'].reshape(-1, MOE_FF, d_model)
    lp['expert_base'] = l * N_EXPERTS
    return lp


def rotary_tables(length):
    rows = length // GRID_W
    row = jnp.repeat(jnp.arange(rows, dtype=F32), GRID_W)
    col = jnp.tile(jnp.arange(GRID_W, dtype=F32), rows)
    nf = HEAD_W // 4
    inv = ROPE_BASE ** (-jnp.arange(nf, dtype=F32) / nf)
    ang = jnp.concatenate([row[:, None] * inv, col[:, None] * inv], axis=-1)
    cos, sin = jnp.cos(ang), jnp.sin(ang)
    return jnp.concatenate([cos, cos], axis=-1), jnp.concatenate([-sin, sin], axis=-1)


def bidir_scans(pg_ctx, pg_lat, lp, rot):
    (p_ctx, g_ctx), (p_lat, g_lat) = pg_ctx, pg_lat
    b = p_lat.shape[0]
    zeros_s = jnp.zeros((b, HEADS, HEAD_W, HEAD_W), F32)
    zeros_v = jnp.zeros((b, HEADS, 1, HEAD_W), F32)
    out = {}
    for name, mode, offs, scale in (("gla", "gla", (OFF_GLA_Q, OFF_GLA_K, OFF_GLA_V), GLA_DK ** -0.5),
                                    ("ret", "ret", (OFF_RET_Q, OFF_RET_K, OFF_RET_V), HEAD_W ** -0.5)):
        o_c, o_l = None, None
        for d in range(2):
            gate = lp['gla_gate'][d] if mode == "gla" else lp['ret_dec'][d]
            o_c, st = gla_scan(p_ctx, g_ctx, *offs, zeros_s, reverse=d == 1, mode=mode, gate=gate,
                               o_prev=o_c, q_scale=scale)
            o_l, _ = gla_scan(p_lat, g_lat, *offs, st, reverse=d == 1, mode=mode, gate=gate,
                              rot=rot if mode == "ret" else None, o_prev=o_l, q_scale=scale)
        out[name] = (o_c, o_l)
    o_c, o_l = None, None
    for d in range(2):
        o_c, st = mlstm_scan(p_ctx, g_ctx, (zeros_s, zeros_v, zeros_v), lp['ml_bias'], reverse=d == 1, o_prev=o_c)
        o_l, _ = mlstm_scan(p_lat, g_lat, st, lp['ml_bias'], reverse=d == 1, o_prev=o_l)
    out["ml"] = (o_c, o_l)
    return out


def kernel(x, c, ctx, c_ctx, ada_w, ada_b, norm1_g, norm2_g, w_in, hy_conv_w, hy_conv_b, hy_f_w1, hy_f_b1, hy_f_w2, hy_f_b2, hy_f_freq, hy_f_w3, hy_decay, hy_skip, gla_wa2, gla_ba, gla_norm_g, ml_gate_b, ml_norm_g, ret_decay, ret_norm_g, w_out, moe_wg, moe_bg, moe_we, moe_be, moe_w1, moe_w3, moe_w2, final_g):
    prm = dict(norm1_g=norm1_g, norm2_g=norm2_g, w_in=w_in, hy_conv_w=hy_conv_w, hy_conv_b=hy_conv_b,
               hy_f_w1=hy_f_w1, hy_f_b1=hy_f_b1, hy_f_w2=hy_f_w2, hy_f_b2=hy_f_b2, hy_f_freq=hy_f_freq,
               hy_f_w3=hy_f_w3, hy_decay=hy_decay, hy_skip=hy_skip, gla_wa2=gla_wa2, gla_ba=gla_ba,
               gla_norm_g=gla_norm_g, ml_gate_b=ml_gate_b, ml_norm_g=ml_norm_g, ret_decay=ret_decay,
               ret_norm_g=ret_norm_g, w_out=w_out, moe_wg=moe_wg, moe_bg=moe_bg, moe_we=moe_we,
               moe_be=moe_be, moe_w1=moe_w1, moe_w3=moe_w3, moe_w2=moe_w2)
    depth = ada_w.shape[0]
    b, seq, d = x.shape
    lc = ctx.shape[1]
    lat = x.astype(F32)
    cx = ctx.astype(F32)
    cvec = jnp.zeros((8, d), F32).at[:b].set(c.astype(F32)).at[b].set(c_ctx.astype(F32))
    mod_all = adaln(cvec, ada_w, ada_b)
    lat_row = lambda bi: bi
    ctx_row = lambda bi: b
    rot = rotary_tables(seq)
    tab_lat = dft_tables(seq // 2)
    tab_ctx = dft_tables(lc // 2)
    for l in range(depth):
        with_ctx = l < depth - 1
        lp = prep_layer(l, prm)
        mod4 = mod_all[l].reshape(8, 6, 1, d)
        p_lat, g_lat = inproj(lat, lp['norm1_g'], mod4, lat_row, 1, 0, lp['w_in_p'])
        p_ctx, g_ctx = inproj(cx, lp['norm1_g'], mod4, ctx_row, 1, 0, lp['w_in_p'])
        scans = bidir_scans((p_ctx, g_ctx), (p_lat, g_lat), lp, rot)
        hy_lat = hyena_mixer(p_lat, lp, tab_lat)
        lat = outproj(hy_lat, scans["gla"][1], scans["ml"][1], scans["ret"][1], p_lat, lp['gains'],
                      lp['w_out_b'], lat, mod4, lat_row, 2)
        lat = hier_moe(lat, lp, mod4, lat_row)
        if with_ctx:
            hy_ctx = hyena_mixer(p_ctx, lp, tab_ctx)
            cx = outproj(hy_ctx, scans["gla"][0], scans["ml"][0], scans["ret"][0], p_ctx, lp['gains'],
                         lp['w_out_b'], cx, mod4, ctx_row, 2)
            cx = hier_moe(cx, lp, mod4, ctx_row)
    return final_norm(lat, final_g).astype(x.dtype)
```

```python
import functools
import math

import jax
import jax.numpy as jnp
from jax import lax
from jax.experimental import pallas as pl
from jax.experimental.pallas import tpu as pltpu

F32 = jnp.float32
BF16 = jnp.bfloat16
HIGHEST = lax.Precision.HIGHEST

EPS = 1e-6
CHUNK = 64
SCAN_BLOCK = 256
NORM_ROWS = 128
HALO = 16
GRID_W = 64
ROPE_BASE = 10000.0
HEADS = 4
HEAD_W = 128
HY_W = 512
HY_EMB = 33
GLA_DK = 64
GLA_RANK = 16
GLA_TAU = 16.0
MOE_GROUPS = 4
MOE_EPG = 8
MOE_FF = 256
N_EXPERTS = MOE_GROUPS * MOE_EPG

GROUP_W = 512
OFF_HY = 0
OFF_GLA_Q = 1536
OFF_GLA_K = OFF_GLA_Q + GROUP_W
OFF_GLA_V = OFF_GLA_K + GROUP_W
OFF_GLA_G = OFF_GLA_V + GROUP_W
OFF_ML_Q = OFF_GLA_G + GROUP_W
OFF_ML_K = OFF_ML_Q + GROUP_W
OFF_ML_V = OFF_ML_K + GROUP_W
OFF_ML_O = OFF_ML_V + GROUP_W
OFF_RET_Q = OFF_ML_O + GROUP_W
OFF_RET_K = OFF_RET_Q + GROUP_W
OFF_RET_V = OFF_RET_K + GROUP_W
OFF_RET_G = OFF_RET_V + GROUP_W
N_PROJ = OFF_RET_G + GROUP_W
SMALL_BASE = GLA_DK
SMALL_R = SMALL_BASE
SMALL_MLG = SMALL_BASE + 2 * GLA_RANK

VMEM_LIMIT_BYTES = 56 * 1024 * 1024


def _cparams(*sem):
    return pltpu.CompilerParams(dimension_semantics=sem, vmem_limit_bytes=VMEM_LIMIT_BYTES)


def _pick(n, candidates):
    for c in candidates:
        if n % c == 0:
            return c
    return n


def _silu(x):
    return x / (1.0 + jnp.exp(-x))


def _sigmoid(x):
    return 1.0 / (1.0 + jnp.exp(-x))


def _log_sigmoid(x):
    return jnp.minimum(x, 0.0) - jnp.log(1.0 + jnp.exp(-jnp.abs(x)))


def _dot(a, b):
    return jnp.dot(a, b, preferred_element_type=F32)


def _dot_hi(a, b):
    return jnp.dot(a, b, preferred_element_type=F32, precision=HIGHEST)


def _split_bf16(x, parts):
    out = []
    for _ in range(parts - 1):
        hi = x.astype(BF16)
        out.append(hi)
        x = x - hi.astype(F32)
    out.append(x.astype(BF16))
    return out


def _dot_exact_lhs(a_bf16, x):
    return sum(_dot(a_bf16, p) for p in _split_bf16(x, 3))


def _pack_pairs(x):
    h = x.shape[1] // 2
    bits = lambda a: lax.bitcast_convert_type(a.astype(BF16).astype(F32), jnp.uint32)
    return bits(x[:, :h]) | (bits(x[:, h:]) >> 16)


def _unpack_pairs(w):
    hi = lax.bitcast_convert_type(w & jnp.uint32(0xFFFF0000), F32)
    lo = lax.bitcast_convert_type(w << 16, F32)
    return jnp.concatenate([hi, lo], axis=-1)


def _dot_nt(a, b):
    return lax.dot_general(a, b, (((1,), (1,)), ((), ())), preferred_element_type=F32)


def _dot_tn(a, b):
    return lax.dot_general(a, b, (((0,), (0,)), ((), ())), preferred_element_type=F32)


def _adaln_kernel(c_ref, w_ref, b_ref, o_ref):
    o_ref[0] = _dot_hi(_silu(c_ref[...]), w_ref[0]) + b_ref[0]


def adaln(cvec, ada_w, ada_b):
    depth, d, n = ada_w.shape
    tn = _pick(n, (1024, 512, 256, 128))
    return pl.pallas_call(
        _adaln_kernel,
        grid=(depth, n // tn),
        in_specs=[pl.BlockSpec((8, d), lambda l, j: (0, 0)),
                  pl.BlockSpec((1, d, tn), lambda l, j: (l, 0, j)),
                  pl.BlockSpec((1, 1, tn), lambda l, j: (l, 0, j))],
        out_specs=pl.BlockSpec((1, 8, tn), lambda l, j: (l, 0, j)),
        out_shape=jax.ShapeDtypeStruct((depth, 8, n), F32),
        compiler_params=_cparams("parallel", "parallel"),
        name="adaln",
    )(cvec, ada_w, ada_b.reshape(depth, 1, n))


def _norm_mod(x, g, sc, sh):
    ms = jnp.mean(x * x, axis=-1, keepdims=True)
    return (x * lax.rsqrt(ms + EPS) * g) * (1.0 + sc) + sh


def _inproj_kernel(x_ref, g_ref, sc_ref, sh_ref, w_ref, o_ref, gate_ref, xn_ref, *, gate_tile, gate_off):
    j = pl.program_id(2)

    @pl.when(j == 0)
    def _():
        tm = xn_ref.shape[0]
        rc = min(tm, NORM_ROWS)
        for r in range(0, tm, rc):
            xn = _norm_mod(x_ref[0, r:r + rc, :], g_ref[...], sc_ref[...], sh_ref[...]).astype(BF16)
            xn_ref[r:r + rc, :] = xn
            o_ref[0, r:r + rc, :] = _dot(xn, w_ref[...]).astype(BF16)

    @pl.when(j != 0)
    def _():
        acc = _dot(xn_ref[...], w_ref[...])
        o_ref[0] = acc.astype(BF16)

        @pl.when(j == gate_tile)
        def _():
            gate_ref[0] = acc[:, gate_off:gate_off + HEAD_W]


def inproj(x, g, mod4, row_of_batch, i_sc, i_sh, w):
    b, l, d = x.shape
    n = w.shape[1]
    tm = _pick(l, (1024, 512, 256, 128, 64))
    tn = _pick(n, (768, 512, 256, 128))
    assert OFF_GLA_K // tn > 0 and OFF_GLA_K % tn + HEAD_W <= tn
    kern = functools.partial(_inproj_kernel, gate_tile=OFF_GLA_K // tn, gate_off=OFF_GLA_K % tn)
    return pl.pallas_call(
        kern,
        grid=(b, l // tm, n // tn),
        in_specs=[pl.BlockSpec((1, tm, d), lambda bi, i, j: (bi, i, 0)),
                  pl.BlockSpec((1, d), lambda bi, i, j: (0, 0)),
                  pl.BlockSpec((None, None, 1, d), lambda bi, i, j: (row_of_batch(bi), i_sc, 0, 0)),
                  pl.BlockSpec((None, None, 1, d), lambda bi, i, j: (row_of_batch(bi), i_sh, 0, 0)),
                  pl.BlockSpec((d, tn), lambda bi, i, j: (0, j))],
        out_specs=[pl.BlockSpec((1, tm, tn), lambda bi, i, j: (bi, i, j)),
                   pl.BlockSpec((1, tm, HEAD_W), lambda bi, i, j: (bi, i, 0))],
        out_shape=[jax.ShapeDtypeStruct((b, l, n), BF16), jax.ShapeDtypeStruct((b, l, HEAD_W), F32)],
        scratch_shapes=[pltpu.VMEM((tm, d), BF16)],
        compiler_params=_cparams("parallel", "parallel", "arbitrary"),
        name="inproj",
    )(x, g.reshape(1, d), mod4, mod4, w)


def _causal_mask(n, reverse):
    r = lax.broadcasted_iota(jnp.int32, (n, n), 0)
    c = lax.broadcasted_iota(jnp.int32, (n, n), 1)
    same_chunk = (r // CHUNK) == (c // CHUNK)
    return jnp.logical_and(same_chunk, (c >= r) if reverse else (c <= r))


def _gla_scan_kernel(*refs, reverse, mode, rotate, accumulate, n_chunks, q_scale):
    it = iter(refs)
    q_ref, k_ref, v_ref = next(it), next(it), next(it)
    if mode == "gla":
        gsrc_ref, wah_ref, wal_ref, ba_ref = next(it), next(it), next(it), next(it)
    else:
        dec_ref = next(it)
    if rotate:
        cos_ref, sin_ref = next(it), next(it)
    s0_ref = next(it)
    if accumulate:
        oprev_ref = next(it)
    o_ref, sfin_ref, st_ref = next(it), next(it), next(it)
    dterm_refs = list(it)

    i = pl.program_id(1)

    @pl.when(i == 0)
    def _():
        st_ref[...] = s0_ref[0]

    tb = n_chunks * CHUNK
    mask = _causal_mask(tb, reverse)
    maskb = mask.astype(BF16)
    order = [(n_chunks - 1 - c) if reverse else c for c in range(n_chunks)]
    rows = [slice(c * CHUNK, (c + 1) * CHUNK) for c in range(n_chunks)]

    q = q_ref[0].astype(F32)
    k = k_ref[0].astype(F32)
    vb = v_ref[0]

    def decay_terms(la):
        bc = _dot_exact_lhs(maskb, la)
        tots = [bc[r.start:r.start + 1, :] if reverse else bc[r.stop - 1:r.stop, :] for r in rows]
        tot_rows = jnp.concatenate([jnp.broadcast_to(t, (CHUNK, GROUP_W)) for t in tots], axis=0)
        tot_pad = tots + [jnp.zeros_like(tots[0])] * (-n_chunks % 8)
        return jnp.exp(bc), jnp.exp(-bc), jnp.exp(tot_rows - bc), jnp.exp(jnp.concatenate(tot_pad, axis=0))

    if mode == "gla":
        g_hi, g_lo = _split_bf16(gsrc_ref[0], 2)
        logit = (_dot(g_hi, wah_ref[...]) + _dot(g_lo, wah_ref[...]) + _dot(g_hi, wal_ref[...])) + ba_ref[...]
        e_bc, e_nbc, e_st, e_tot = decay_terms(_log_sigmoid(logit) * (1.0 / GLA_TAU))
    else:
        @pl.when(i == 0)
        def _():
            terms = decay_terms(jnp.broadcast_to(_log_sigmoid(dec_ref[...]), (tb, GROUP_W)))
            for t_ref, t in zip(dterm_refs, terms):
                t_ref[...] = t

        e_bc, e_nbc, e_st, e_tot = (t_ref[...] for t_ref in dterm_refs)
    e_tots = [e_tot[c:c + 1, :] for c in range(n_chunks)]
    if rotate:
        cos2 = cos_ref[...]
        sin2 = sin_ref[...]

    outs = []
    for h in range(HEADS):
        hs = slice(h * HEAD_W, (h + 1) * HEAD_W)
        qh, kh, vh = q[:, hs], k[:, hs], vb[:, hs]
        if rotate:
            qh = qh * cos2 + pltpu.roll(qh, HEAD_W // 2, axis=1) * sin2
            kh = kh * cos2 + pltpu.roll(kh, HEAD_W // 2, axis=1) * sin2
        q_in = (qh * q_scale * e_bc[:, hs]).astype(BF16)
        k_in = (kh * e_nbc[:, hs]).astype(BF16)
        k_st = (kh * e_st[:, hs]).astype(BF16)
        att = jnp.where(mask, _dot_nt(q_in, k_in), 0.0).astype(BF16)
        intra = _dot(att, vh)
        st = st_ref[h]
        inter = [None] * n_chunks
        for cc in order:
            r = rows[cc]
            inter[cc] = _dot_nt(q_in[r], st.astype(BF16))
            st = st * e_tots[cc][:, hs] + _dot_tn(vh[r], k_st[r])
        st_ref[h] = st
        outs.append(intra + jnp.concatenate(inter, axis=0))
    o = jnp.concatenate(outs, axis=-1)
    if accumulate:
        o = o + oprev_ref[0]
    o_ref[0] = o

    @pl.when(i == pl.num_programs(1) - 1)
    def _():
        sfin_ref[0] = st_ref[...]


def gla_scan(p, gblock, off_q, off_k, off_v, s0, *, reverse, mode, gate=None, rot=None, o_prev=None, q_scale):
    b, l, _ = p.shape
    tb = _pick(l, (SCAN_BLOCK, 128, 64))
    nblk = l // tb
    blk = (lambda i: nblk - 1 - i) if reverse else (lambda i: i)
    gspec = lambda off: pl.BlockSpec((1, tb, GROUP_W), lambda bi, i: (bi, blk(i), off // GROUP_W))
    const = lambda shape: pl.BlockSpec(shape, lambda bi, i: (0,) * len(shape))
    in_specs = [gspec(off_q), gspec(off_k), gspec(off_v)]
    args = [p, p, p]
    if mode == "gla":
        wa_hi, wa_lo, ba = gate
        in_specs += [pl.BlockSpec((1, tb, HEAD_W), lambda bi, i: (bi, blk(i), 0)),
                     const((HEAD_W, GROUP_W)), const((HEAD_W, GROUP_W)), const((1, GROUP_W))]
        args += [gblock, wa_hi, wa_lo, ba]
    else:
        in_specs += [const((1, GROUP_W))]
        args += [gate]
    if rot is not None:
        in_specs += [pl.BlockSpec((tb, HEAD_W), lambda bi, i: (blk(i), 0))] * 2
        args += [rot[0], rot[1]]
    sspec = pl.BlockSpec((1, HEADS, HEAD_W, HEAD_W), lambda bi, i: (bi, 0, 0, 0))
    in_specs += [sspec]
    args += [s0]
    o_spec = pl.BlockSpec((1, tb, GROUP_W), lambda bi, i: (bi, blk(i), 0))
    aliases = {}
    if o_prev is not None:
        in_specs += [o_spec]
        aliases = {len(args): 0}
        args += [o_prev]
    kern = functools.partial(_gla_scan_kernel, reverse=reverse, mode=mode, rotate=rot is not None,
                             accumulate=o_prev is not None, n_chunks=tb // CHUNK, q_scale=q_scale)
    return pl.pallas_call(
        kern,
        grid=(b, nblk),
        in_specs=in_specs,
        out_specs=[o_spec, sspec],
        out_shape=[jax.ShapeDtypeStruct((b, l, GROUP_W), F32),
                   jax.ShapeDtypeStruct((b, HEADS, HEAD_W, HEAD_W), F32)],
        scratch_shapes=[pltpu.VMEM((HEADS, HEAD_W, HEAD_W), F32)] + (
            [pltpu.VMEM((tb, GROUP_W), F32)] * 3 + [pltpu.VMEM((8 * pl.cdiv(tb // CHUNK, 8), GROUP_W), F32)]
            if mode == "ret" else []),
        input_output_aliases=aliases,
        compiler_params=_cparams("parallel", "arbitrary"),
        name=f"{mode}_scan_{'bwd' if reverse else 'fwd'}",
    )(*args)


def _mlstm_kernel(*refs, reverse, accumulate, n_chunks, k_scale):
    it = iter(refs)
    q_ref, k_ref, v_ref, gsrc_ref, bias_ref = (next(it) for _ in range(5))
    c0_ref, n0_ref, m0_ref = next(it), next(it), next(it)
    if accumulate:
        oprev_ref = next(it)
    o_ref, cfin_ref, nfin_ref, mfin_ref = next(it), next(it), next(it), next(it)
    ct_ref, n_ref, m_ref = next(it), next(it), next(it)

    i = pl.program_id(1)

    @pl.when(i == 0)
    def _():
        ct_ref[...] = c0_ref[0]
        n_ref[...] = n0_ref[0]
        m_ref[...] = m0_ref[0]

    tb = n_chunks * CHUNK
    mask = _causal_mask(tb, reverse)
    maskb = mask.astype(BF16)
    order = [(n_chunks - 1 - c) if reverse else c for c in range(n_chunks)]
    rows = [slice(c * CHUNK, (c + 1) * CHUNK) for c in range(n_chunks)]
    lane_i = SMALL_MLG + (8 if reverse else 0)
    lane_f = lane_i + HEADS

    q = q_ref[0].astype(F32)
    k = k_ref[0].astype(F32) * k_scale
    vb = v_ref[0]
    gates = gsrc_ref[0] + bias_ref[...]
    bc_all = _dot_exact_lhs(maskb, _log_sigmoid(gates))
    rep = lambda a, c: jnp.broadcast_to(a[:, c:c + 1], (a.shape[0], HEAD_W))
    outs = []
    for h in range(HEADS):
        hs = slice(h * HEAD_W, (h + 1) * HEAD_W)
        qh, kh, vh = q[:, hs], k[:, hs], vb[:, hs]
        bch = rep(bc_all, lane_f + h)
        wh = rep(gates, lane_i + h) - bch
        bends = [bch[r.start:r.start + 1, :] if reverse else bch[r.stop - 1:r.stop, :] for r in rows]
        w_row = jnp.broadcast_to(jnp.transpose(wh)[0:1, :], (tb, tb))
        dmat = jnp.where(mask, bch[:, 0:1] + w_row, -jnp.inf)
        rowmax = jnp.max(dmat, axis=-1, keepdims=True)
        qb = qh.astype(BF16)
        s_raw = _dot_nt(qb, kh.astype(BF16)) * jnp.exp(dmat - rowmax)
        sv = _dot(s_raw.astype(BF16), vh)
        s_sum = jnp.sum(s_raw, axis=-1, keepdims=True)
        ct, nv, m_prev = ct_ref[h], n_ref[h], m_ref[h]
        out = [None] * n_chunks
        for cc in order:
            r = rows[cc]
            bend = bends[cc]
            glh = bend + wh[r]
            gmax = jnp.max(glh, axis=0, keepdims=True)
            gk = jnp.exp(glh - gmax) * kh[r]
            inter_log = bch[r] + m_prev
            m_t = jnp.maximum(inter_log, rowmax[r])
            e_intra = jnp.exp(rowmax[r] - m_t)
            inter = jnp.exp(inter_log - m_t)
            num = inter * _dot_nt(qb[r], ct.astype(BF16)) + e_intra * sv[r]
            den = jnp.abs(inter * jnp.sum(qh[r] * nv, axis=-1, keepdims=True) + e_intra * s_sum[r])
            out[cc] = num / jnp.maximum(den, jnp.exp(-m_t))
            m_new = jnp.maximum(bend + m_prev, gmax)
            dec = jnp.exp(bend + m_prev - m_new)
            e_upd = jnp.exp(gmax - m_new)
            ct = dec * ct + e_upd * _dot_tn(vh[r], gk.astype(BF16))
            nv = dec * nv + e_upd * jnp.sum(gk, axis=0, keepdims=True)
            m_prev = m_new
        ct_ref[h] = ct
        n_ref[h] = nv
        m_ref[h] = m_prev
        outs.append(jnp.concatenate(out, axis=0))
    o = jnp.concatenate(outs, axis=-1)
    if accumulate:
        o = o + oprev_ref[0]
    o_ref[0] = o

    @pl.when(i == pl.num_programs(1) - 1)
    def _():
        cfin_ref[0] = ct_ref[...]
        nfin_ref[0] = n_ref[...]
        mfin_ref[0] = m_ref[...]


def mlstm_scan(p, gblock, state, gate_bias, *, reverse, o_prev=None):
    b, l, _ = p.shape
    tb = _pick(l, (SCAN_BLOCK, 128, 64))
    nblk = l // tb
    blk = (lambda i: nblk - 1 - i) if reverse else (lambda i: i)
    c0, n0, m0 = state
    gspec = lambda off: pl.BlockSpec((1, tb, GROUP_W), lambda bi, i: (bi, blk(i), off // GROUP_W))
    const = lambda shape: pl.BlockSpec(shape, lambda bi, i: (0,) * len(shape))
    cspec = pl.BlockSpec((1, HEADS, HEAD_W, HEAD_W), lambda bi, i: (bi, 0, 0, 0))
    vspec = pl.BlockSpec((1, HEADS, 1, HEAD_W), lambda bi, i: (bi, 0, 0, 0))
    in_specs = [gspec(OFF_ML_Q), gspec(OFF_ML_K), gspec(OFF_ML_V),
                pl.BlockSpec((1, tb, HEAD_W), lambda bi, i: (bi, blk(i), 0)),
                const((1, HEAD_W)), cspec, vspec, vspec]
    args = [p, p, p, gblock, gate_bias, c0, n0, m0]
    o_spec = pl.BlockSpec((1, tb, GROUP_W), lambda bi, i: (bi, blk(i), 0))
    aliases = {}
    if o_prev is not None:
        in_specs += [o_spec]
        aliases = {len(args): 0}
        args += [o_prev]
    kern = functools.partial(_mlstm_kernel, reverse=reverse, accumulate=o_prev is not None,
                             n_chunks=tb // CHUNK, k_scale=HEAD_W ** -0.5)
    outs = pl.pallas_call(
        kern,
        grid=(b, nblk),
        in_specs=in_specs,
        out_specs=[o_spec, cspec, vspec, vspec],
        out_shape=[jax.ShapeDtypeStruct((b, l, GROUP_W), F32),
                   jax.ShapeDtypeStruct((b, HEADS, HEAD_W, HEAD_W), F32),
                   jax.ShapeDtypeStruct((b, HEADS, 1, HEAD_W), F32),
                   jax.ShapeDtypeStruct((b, HEADS, 1, HEAD_W), F32)],
        scratch_shapes=[pltpu.VMEM((HEADS, HEAD_W, HEAD_W), F32), pltpu.VMEM((HEADS, 1, HEAD_W), F32),
                        pltpu.VMEM((HEADS, 1, HEAD_W), F32)],
        input_output_aliases=aliases,
        compiler_params=_cparams("parallel", "arbitrary"),
        name=f"mlstm_scan_{'bwd' if reverse else 'fwd'}",
    )(*args)
    return outs[0], (outs[1], outs[2], outs[3])


def _hy_pre_kernel(u_ref, up_ref, un_ref, w_ref, b_ref, o_ref, ob_ref, y_ref, *, rows):
    i = pl.program_id(2)
    u = u_ref[0].astype(F32)
    prev_row = jnp.where(i == 0, 0.0, up_ref[0, HALO - 1:HALO, :].astype(F32))
    next_row = jnp.where(i == pl.num_programs(2) - 1, 0.0, un_ref[0, 0:1, :].astype(F32))
    ridx = lax.broadcasted_iota(jnp.int32, u.shape, 0)
    u_dn = jnp.where(ridx == 0, prev_row, pltpu.roll(u, 1, axis=0))
    u_up = jnp.where(ridx == rows - 1, next_row, pltpu.roll(u, rows - 1, axis=0))
    y = w_ref[0:1, :] * u_dn + w_ref[1:2, :] * u + w_ref[2:3, :] * u_up + b_ref[...]
    lane_tiles = HY_W // HEAD_W
    for j in range(lane_tiles):
        y_ref[j] = y[:, j * HEAD_W:(j + 1) * HEAD_W]
    for par in range(2):
        plane = jnp.concatenate([y_ref[j, pl.ds(par, rows // 2, stride=2), :] for j in range(lane_tiles)], axis=-1)
        o_ref[0, par] = plane
        ob_ref[0, par] = plane.astype(BF16)


def hy_pre(p, conv_w, conv_b):
    b, l, _ = p.shape
    rows = _pick(l, (512, 256, 128, 64))
    nr = l // rows
    r8 = rows // HALO
    n8 = l // HALO
    wpad = jnp.zeros((8, 3 * HY_W), F32).at[:3].set(conv_w)
    return pl.pallas_call(
        functools.partial(_hy_pre_kernel, rows=rows),
        grid=(b, 3, nr),
        in_specs=[pl.BlockSpec((1, rows, HY_W), lambda bi, j, i: (bi, i, j)),
                  pl.BlockSpec((1, HALO, HY_W), lambda bi, j, i: (bi, jnp.maximum(i * r8 - 1, 0), j)),
                  pl.BlockSpec((1, HALO, HY_W), lambda bi, j, i: (bi, jnp.minimum((i + 1) * r8, n8 - 1), j)),
                  pl.BlockSpec((8, HY_W), lambda bi, j, i: (0, j)),
                  pl.BlockSpec((1, HY_W), lambda bi, j, i: (0, j))],
        out_specs=[pl.BlockSpec((1, 2, rows // 2, HY_W), lambda bi, j, i: (bi, 0, i, j))] * 2,
        out_shape=[jax.ShapeDtypeStruct((b, 2, l // 2, 3 * HY_W), F32),
                   jax.ShapeDtypeStruct((b, 2, l // 2, 3 * HY_W), BF16)],
        scratch_shapes=[pltpu.VMEM((HY_W // HEAD_W, rows, HEAD_W), F32)],
        compiler_params=_cparams("parallel", "parallel", "parallel"),
        name="hy_shortconv",
    )(p, p, p, wpad, conv_b.reshape(1, 3 * HY_W))


def _mm_kernel(a_ref, b_ref, o_ref):
    o_ref[...] = _dot(a_ref[...], b_ref[...])


def matmul_bf16(a, bm):
    m, k = a.shape
    n = bm.shape[1]
    tm = _pick(m, (512, 256, 128, 64))
    tn = _pick(n, (512, 256, 128))
    return pl.pallas_call(
        _mm_kernel,
        grid=(n // tn, m // tm),
        in_specs=[pl.BlockSpec((tm, k), lambda j, i: (i, 0)),
                  pl.BlockSpec((k, tn), lambda j, i: (0, j))],
        out_specs=pl.BlockSpec((tm, tn), lambda j, i: (i, j)),
        out_shape=jax.ShapeDtypeStruct((m, n), F32),
        compiler_params=_cparams("parallel", "parallel"),
        name="matmul_bf16",
    )(a, bm)


def _dft_fwd_kernel(c_ref, s_ref, z_ref, cp_ref, sp_ref, hr_ref, hi_ref, hbr_ref, hbi_ref, ur_ref, ui_ref):
    c = c_ref[...]
    s = s_ref[...]
    z0 = z_ref[0, 0]
    z1 = z_ref[0, 1]
    e0r, e0i = _dot(c, z0), -_dot(s, z0)
    e1r, e1i = _dot(c, z1), -_dot(s, z1)
    cp = cp_ref[...]
    sp = sp_ref[...]
    t1r = cp * e1r + sp * e1i
    t1i = cp * e1i - sp * e1r
    ar, ai, br, bi = e0r + t1r, e0i + t1i, e0r - t1r, e0i - t1i
    hr, hi, hbr, hbi = hr_ref[...], hi_ref[...], hbr_ref[...], hbi_ref[...]
    pr = ar * hr - ai * hi
    pim = ar * hi + ai * hr
    qr = br * hbr - bi * hbi
    qim = br * hbi + bi * hbr
    wr, wim = pr - qr, pim - qim
    ur_ref[0, 0] = (pr + qr).astype(BF16)
    ui_ref[0, 0] = (pim + qim).astype(BF16)
    ur_ref[0, 1] = (cp * wr - sp * wim).astype(BF16)
    ui_ref[0, 1] = (sp * wr + cp * wim).astype(BF16)


def dft_fwd(cmat, smat, zb, z_col, cpsi, spsi, h4, h_col):
    b, _, m, _ = zb.shape
    tf = _pick(m, (512, 256, 128, 64))
    hspec = pl.BlockSpec((tf, HY_W), lambda i, bi: (i, h_col))
    vspec = pl.BlockSpec((tf, 1), lambda i, bi: (i, 0))
    ospec = pl.BlockSpec((1, 2, tf, HY_W), lambda i, bi: (bi, 0, i, 0))
    return pl.pallas_call(
        _dft_fwd_kernel,
        grid=(m // tf, b),
        in_specs=[pl.BlockSpec((tf, m), lambda i, bi: (i, 0)),
                  pl.BlockSpec((tf, m), lambda i, bi: (i, 0)),
                  pl.BlockSpec((1, 2, m, HY_W), lambda i, bi: (bi, 0, 0, z_col)),
                  vspec, vspec, hspec, hspec, hspec, hspec],
        out_specs=[ospec, ospec],
        out_shape=[jax.ShapeDtypeStruct((b, 2, m, HY_W), BF16)] * 2,
        compiler_params=_cparams("parallel", "parallel"),
        name="hy_dft_fwd",
    )(cmat, smat, zb, cpsi, spsi, *h4)


def _dft_inv_kernel(c_ref, s_ref, ur_ref, ui_ref, x_ref, zp_ref, skip_ref, *o_refs, interleave):
    c = c_ref[...]
    s = s_ref[...]
    tt = c.shape[0]
    lane_tiles = HY_W // HEAD_W
    for par in range(2):
        conv = _dot(c, ur_ref[0, par]) - _dot(s, ui_ref[0, par])
        z = x_ref[0, par] * (conv + skip_ref[...] * zp_ref[0, par])
        if interleave:
            for j in range(lane_tiles):
                o_refs[1][j, pl.ds(par, tt, stride=2), :] = z[:, j * HEAD_W:(j + 1) * HEAD_W]
        else:
            o_refs[0][0, par] = z
            o_refs[1][0, par] = z.astype(BF16)
    if interleave:
        o_refs[0][0] = jnp.concatenate([o_refs[1][j] for j in range(lane_tiles)], axis=-1)


def dft_inv(cmat, smat, ur, ui, x_arr, x_col, zp_arr, zp_col, skip, interleave):
    b, _, m, _ = ur.shape
    tt = _pick(m, (512, 256, 128, 64))
    uspec = pl.BlockSpec((1, 2, m, HY_W), lambda i, bi: (bi, 0, 0, 0))
    pspec = lambda col: pl.BlockSpec((1, 2, tt, HY_W), lambda i, bi: (bi, 0, i, col))
    if interleave:
        out_specs = [pl.BlockSpec((1, 2 * tt, HY_W), lambda i, bi: (bi, i, 0))]
        out_shape = [jax.ShapeDtypeStruct((b, 2 * m, HY_W), F32)]
    else:
        out_specs = [pspec(0), pspec(0)]
        out_shape = [jax.ShapeDtypeStruct((b, 2, m, HY_W), F32), jax.ShapeDtypeStruct((b, 2, m, HY_W), BF16)]
    return pl.pallas_call(
        functools.partial(_dft_inv_kernel, interleave=interleave),
        grid=(m // tt, b),
        in_specs=[pl.BlockSpec((tt, m), lambda i, bi: (i, 0)),
                  pl.BlockSpec((tt, m), lambda i, bi: (i, 0)),
                  uspec, uspec, pspec(x_col), pspec(zp_col),
                  pl.BlockSpec((1, HY_W), lambda i, bi: (0, 0))],
        out_specs=out_specs,
        out_shape=out_shape,
        scratch_shapes=[pltpu.VMEM((HY_W // HEAD_W, 2 * tt, HEAD_W), F32)] if interleave else [],
        compiler_params=_cparams("parallel", "parallel"),
        name="hy_dft_inv",
    )(cmat, smat, ur, ui, x_arr, zp_arr, skip.reshape(1, HY_W))


def dft_tables(l):
    r = _pick(l, (64, 32, 16, 8))
    period = 8 * l
    odd = 2 * jnp.arange(l, dtype=jnp.int32) + 1
    s1 = jnp.arange(l // r, dtype=jnp.int32) * (2 * r)
    s0 = 2 * jnp.arange(r, dtype=jnp.int32) + 1
    ang = lambda ph: ph.astype(F32) * (2.0 * math.pi / period)
    a = ang((odd[:, None] * s1[None, :]) % period)
    bb = ang((odd[:, None] * s0[None, :]) % period)
    ca, sa, cb, sb = jnp.cos(a), jnp.sin(a), jnp.cos(bb), jnp.sin(bb)
    cmat = (ca[:, :, None] * cb[:, None, :] - sa[:, :, None] * sb[:, None, :]).reshape(l, l).astype(BF16)
    smat = (sa[:, :, None] * cb[:, None, :] + ca[:, :, None] * sb[:, None, :]).reshape(l, l).astype(BF16)
    phi = ang(odd)[:, None]
    return cmat, smat, jnp.cos(phi), jnp.sin(phi)


def hyena_filter_taps(length, lp, parity):
    pos = 2.0 * jnp.arange(length // 2, dtype=F32) + parity
    t = pos / (length - 1)
    bands = (HY_EMB - 1) // 2
    fr = jnp.linspace(1e-4, bands - 1, bands, dtype=F32)
    ang = (2.0 * math.pi / length) * pos[:, None] * fr[None, :]
    z = jnp.concatenate([t[:, None], jnp.cos(ang), -jnp.sin(ang)], axis=-1)
    mm = functools.partial(jnp.matmul, precision=HIGHEST)
    hdn = jnp.sin(lp['hy_f_freq'][0] * (mm(z, lp['hy_f_w1']) + lp['hy_f_b1']))
    hdn = jnp.sin(lp['hy_f_freq'][1] * (mm(hdn, lp['hy_f_w2']) + lp['hy_f_b2']))
    return mm(hdn, lp['hy_f_w3']) * jnp.exp(-t[:, None] * jnp.abs(lp['hy_decay']))


def hyena_mixer(p, lp, tables):
    b, l, _ = p.shape
    cmat, smat, cpsi, spsi = tables
    uc, ucb = hy_pre(p, lp['hy_conv_w'], lp['hy_conv_b'])
    w2 = 2 * HY_W
    m = l // 2
    taps = [hyena_filter_taps(l, lp, par).reshape(m, 2, 2, HY_W) for par in range(2)]
    gf0, gf1 = (tp[:, :, 0, :].reshape(m, w2) for tp in taps)
    hb_even, gb1 = (tp[:, :, 1, :].reshape(m, w2) for tp in taps)
    gb0 = jnp.concatenate([hb_even[1:], jnp.zeros((1, w2), F32)], axis=0)
    gr_s = matmul_bf16(cmat, jnp.concatenate([gf0 + gb0, gf1 + gb1], axis=1).astype(BF16))
    gi_s = -matmul_bf16(smat, jnp.concatenate([gf0 - gb0, gf1 - gb1], axis=1).astype(BF16))
    gr = cpsi * gr_s - spsi * gi_s
    gi = spsi * gr_s + cpsi * gi_s
    g0r, g1r, g0i, g1i = gr[:, :w2], gr[:, w2:], gi[:, :w2], gi[:, w2:]
    tr = cpsi * g1r + spsi * g1i
    ti = cpsi * g1i - spsi * g1r
    scale = 1.0 / l
    h4 = ((g0r + tr) * scale, (g0i + ti) * scale,
          (g0r - tr) * scale, (g0i - ti) * scale)
    ur, ui = dft_fwd(cmat, smat, ucb, 0, cpsi, spsi, h4, 0)
    z1, z1b = dft_inv(cmat, smat, ur, ui, uc, 1, uc, 0, lp['hy_skip'][0], interleave=False)
    ur, ui = dft_fwd(cmat, smat, z1b, 0, cpsi, spsi, h4, 1)
    (z2,) = dft_inv(cmat, smat, ur, ui, uc, 2, z1, 0, lp['hy_skip'][1], interleave=True)
    return z2


def _head_rms(y):
    parts = []
    for h in range(HEADS):
        yh = y[:, h * HEAD_W:(h + 1) * HEAD_W]
        parts.append(yh * lax.rsqrt(jnp.mean(yh * yh, axis=-1, keepdims=True) + EPS))
    return jnp.concatenate(parts, axis=-1)


def _outproj_kernel(hy_ref, gla_ref, glag_ref, ml_ref, mlo_ref, ret_ref, retg_ref,
                    gn_ref, w_ref, res_ref, gate_ref, o_ref):
    gn = gn_ref[...]
    y_gla = _head_rms(gla_ref[0]) * gn[0:1, :] * _silu(glag_ref[0].astype(F32))
    y_ml = _head_rms(_sigmoid(mlo_ref[0].astype(F32)) * ml_ref[0]) * gn[1:2, :]
    y_ret = _head_rms(ret_ref[0]) * gn[2:3, :] * _silu(retg_ref[0].astype(F32))
    acc = _dot(hy_ref[0].astype(BF16), w_ref[0:GROUP_W, :])
    acc += _dot(y_gla.astype(BF16), w_ref[GROUP_W:2 * GROUP_W, :])
    acc += _dot(y_ml.astype(BF16), w_ref[2 * GROUP_W:3 * GROUP_W, :])
    acc += _dot(y_ret.astype(BF16), w_ref[3 * GROUP_W:4 * GROUP_W, :])
    o_ref[0] = res_ref[0] + gate_ref[...] * acc


def outproj(hy, o_gla, o_ml, o_ret, p, gains, w_out, res, mod4, row_of_batch, i_gate):
    b, l, d = res.shape
    tm = _pick(l, (256, 128, 64))
    gw = GROUP_W
    ospec = pl.BlockSpec((1, tm, gw), lambda bi, i: (bi, i, 0))
    pspec = lambda off: pl.BlockSpec((1, tm, gw), lambda bi, i: (bi, i, off // gw))
    return pl.pallas_call(
        _outproj_kernel,
        grid=(b, l // tm),
        in_specs=[ospec, ospec, pspec(OFF_GLA_G), ospec, pspec(OFF_ML_O), ospec, pspec(OFF_RET_G),
                  pl.BlockSpec((8, gw), lambda bi, i: (0, 0)),
                  pl.BlockSpec((4 * gw, d), lambda bi, i: (0, 0)),
                  pl.BlockSpec((1, tm, d), lambda bi, i: (bi, i, 0)),
                  pl.BlockSpec((None, None, 1, d), lambda bi, i: (row_of_batch(bi), i_gate, 0, 0))],
        out_specs=pl.BlockSpec((1, tm, d), lambda bi, i: (bi, i, 0)),
        out_shape=jax.ShapeDtypeStruct((b, l, d), F32),
        compiler_params=_cparams("parallel", "parallel"),
        name="outproj",
    )(hy, o_gla, p, o_ml, p, o_ret, p, gains, w_out, res, mod4)


def _moe_pre_kernel(x_ref, g_ref, sc_ref, sh_ref, wr_ref, br_ref, t_ref, lg_ref):
    t = _norm_mod(x_ref[0], g_ref[...], sc_ref[...], sh_ref[...])
    t_ref[0] = _pack_pairs(t)
    lg_ref[0] = _dot_hi(t, wr_ref[...]) + br_ref[...]


def moe_pre(x, g, mod4, row_of_batch, i_sc, i_sh, w_router, b_router):
    b, l, d = x.shape
    tm = _pick(l, (512, 256, 128, 64))
    nr = w_router.shape[1]
    return pl.pallas_call(
        _moe_pre_kernel,
        grid=(b, l // tm),
        in_specs=[pl.BlockSpec((1, tm, d), lambda bi, i: (bi, i, 0)),
                  pl.BlockSpec((1, d), lambda bi, i: (0, 0)),
                  pl.BlockSpec((None, None, 1, d), lambda bi, i: (row_of_batch(bi), i_sc, 0, 0)),
                  pl.BlockSpec((None, None, 1, d), lambda bi, i: (row_of_batch(bi), i_sh, 0, 0)),
                  pl.BlockSpec((d, nr), lambda bi, i: (0, 0)),
                  pl.BlockSpec((1, nr), lambda bi, i: (0, 0))],
        out_specs=[pl.BlockSpec((1, tm, d // 2), lambda bi, i: (bi, i, 0)),
                   pl.BlockSpec((1, tm, nr), lambda bi, i: (bi, i, 0))],
        out_shape=[jax.ShapeDtypeStruct((b, l, d // 2), jnp.uint32), jax.ShapeDtypeStruct((b, l, nr), F32)],
        compiler_params=_cparams("parallel", "parallel"),
        name="moe_pre",
    )(x, g.reshape(1, d), mod4, mod4, w_router, b_router)


ROUTE_E1, ROUTE_E2, ROUTE_R1, ROUTE_R2, ROUTE_W1, ROUTE_W2 = range(6)


def _moe_route_kernel(lg_ref, route_ref, cnt_ref, run_ref, *, tr):
    i = pl.program_id(0)

    @pl.when(i == 0)
    def _():
        run_ref[...] = jnp.zeros_like(run_ref)

    lg = lg_ref[...]
    lane = lax.broadcasted_iota(jnp.int32, lg.shape, 1).astype(F32)
    no_lane = float(HEAD_W)
    neg = -jnp.inf
    first_max = lambda vals, vmax: jnp.min(jnp.where(vals == vmax, lane, no_lane), axis=-1, keepdims=True)
    gl = jnp.where(lane < MOE_GROUPS, lg, neg)
    gmax = jnp.max(gl, axis=-1, keepdims=True)
    pg = 1.0 / jnp.sum(jnp.exp(gl - gmax), axis=-1, keepdims=True)
    lo = MOE_GROUPS + first_max(gl, gmax) * MOE_EPG
    sel = jnp.where(jnp.logical_and(lane >= lo, lane < lo + MOE_EPG), lg, neg)
    v1 = jnp.max(sel, axis=-1, keepdims=True)
    i1 = first_max(sel, v1)
    sel2 = jnp.where(lane == i1, neg, sel)
    v2 = jnp.max(sel2, axis=-1, keepdims=True)
    i2 = first_max(sel2, v2)
    e21 = jnp.exp(v2 - v1)
    w1 = pg / (1.0 + e21)
    w2 = w1 * e21
    e1 = i1 - MOE_GROUPS
    e2 = i2 - MOE_GROUPS
    oh1 = (lane == e1).astype(F32)
    oh2 = (lane == e2).astype(F32)
    both = oh1 + oh2
    row = lax.broadcasted_iota(jnp.int32, (tr, tr), 0)
    col = lax.broadcasted_iota(jnp.int32, (tr, tr), 1)
    earlier = (col < row).astype(BF16)
    before = _dot(earlier, both.astype(BF16)) + run_ref[...]
    r1 = jnp.sum(before * oh1, axis=-1, keepdims=True)
    r2 = jnp.sum(before * oh2, axis=-1, keepdims=True)
    run_ref[...] += jnp.sum(both, axis=0, keepdims=True)
    rec = jnp.zeros_like(lg)
    for k, val in ((ROUTE_E1, e1), (ROUTE_E2, e2), (ROUTE_R1, r1), (ROUTE_R2, r2), (ROUTE_W1, w1), (ROUTE_W2, w2)):
        rec = jnp.where(lane == k, val, rec)
    route_ref[...] = rec

    @pl.when(i == pl.num_programs(0) - 1)
    def _():
        cnt_ref[...] = run_ref[...]


def moe_route(logits):
    t, nl = logits.shape
    tr = _pick(t, (512, 256, 128, 64))
    return pl.pallas_call(
        functools.partial(_moe_route_kernel, tr=tr),
        grid=(t // tr,),
        in_specs=[pl.BlockSpec((tr, nl), lambda i: (i, 0))],
        out_specs=[pl.BlockSpec((tr, nl), lambda i: (i, 0)), pl.BlockSpec((1, nl), lambda i: (0, 0))],
        out_shape=[jax.ShapeDtypeStruct((t, nl), F32), jax.ShapeDtypeStruct((1, nl), F32)],
        scratch_shapes=[pltpu.VMEM((1, nl), F32)],
        compiler_params=_cparams("arbitrary"),
        name="moe_route",
    )(logits)


def _moe_pos_kernel(route_ref, pstart_ref, pos_ref):
    rec = route_ref[...]
    lane = lax.broadcasted_iota(jnp.int32, rec.shape, 1).astype(F32)
    start = lambda e: jnp.sum(jnp.where(lane == e, pstart_ref[...], 0.0), axis=-1, keepdims=True)
    p1 = start(rec[:, ROUTE_E1:ROUTE_E1 + 1]) + rec[:, ROUTE_R1:ROUTE_R1 + 1]
    p2 = start(rec[:, ROUTE_E2:ROUTE_E2 + 1]) + rec[:, ROUTE_R2:ROUTE_R2 + 1]
    pos_ref[...] = jnp.where(lane == 0, p1, jnp.where(lane == 1, p2, 0.0))


def moe_pos(route, pstart_row):
    t, nl = route.shape
    tr = _pick(t, (512, 256, 128, 64))
    return pl.pallas_call(
        _moe_pos_kernel,
        grid=(t // tr,),
        in_specs=[pl.BlockSpec((tr, nl), lambda i: (i, 0)), pl.BlockSpec((1, nl), lambda i: (0, 0))],
        out_specs=pl.BlockSpec((tr, nl), lambda i: (i, 0)),
        out_shape=jax.ShapeDtypeStruct((t, nl), F32),
        compiler_params=_cparams("parallel"),
        name="moe_pos",
    )(route, pstart_row)


ISSUE_UNROLL = 8


def _row_copy_wait(src_rows, dst_rows, sem):
    pltpu.make_async_copy(src_rows, dst_rows, sem).wait()


def _moe_dispatch_kernel(pend_ref, padded_ref, pos_ref, tok_ref, xs_ref, zero_ref, sem, *, tr, tm):
    @pl.when(pl.program_id(0) == 0)
    def _():
        zero_ref[...] = jnp.zeros_like(zero_ref)

        def clear(e, carry):
            @pl.when(padded_ref[e] > 0)
            def _():
                dst = xs_ref.at[pl.ds(pl.multiple_of(pend_ref[e] - tm, tm), tm)]
                cp = pltpu.make_async_copy(zero_ref, dst, sem)
                cp.start()
                cp.wait()
            return carry

        lax.fori_loop(0, N_EXPERTS, clear, 0)

    def issue(r, carry):
        for slot in range(2):
            dst = pos_ref[0, 0, 2 * r + slot]
            pltpu.make_async_copy(tok_ref.at[pl.ds(r, 1)], xs_ref.at[pl.ds(dst, 1)], sem).start()
        return carry

    lax.fori_loop(0, tr, issue, 0, unroll=ISSUE_UNROLL)
    for _ in range(2):
        _row_copy_wait(tok_ref, xs_ref.at[pl.ds(0, tr)], sem)


def moe_dispatch(tok, pos, pend, padded, pmax, tm):
    t, d = tok.shape
    tr = pos.shape[2] // 2
    grid_spec = pltpu.PrefetchScalarGridSpec(
        num_scalar_prefetch=2,
        grid=(t // tr,),
        in_specs=[pl.BlockSpec((1, 1, 2 * tr), lambda i, pe, pa: (i, 0, 0), memory_space=pltpu.SMEM),
                  pl.BlockSpec((tr, d), lambda i, pe, pa: (i, 0))],
        out_specs=pl.BlockSpec(memory_space=pl.ANY),
        scratch_shapes=[pltpu.VMEM((tm, d), tok.dtype), pltpu.SemaphoreType.DMA],
    )
    return pl.pallas_call(
        functools.partial(_moe_dispatch_kernel, tr=tr, tm=tm),
        grid_spec=grid_spec,
        out_shape=jax.ShapeDtypeStruct((pmax, d), tok.dtype),
        compiler_params=_cparams("arbitrary"),
        name="moe_dispatch",
    )(pend, padded, pos, tok)


def _moe_expert_kernel(te_ref, nu_ref, x_ref, w1_ref, w3_ref, w2_ref, o_ref, w1b_ref, w3b_ref, w2b_ref):
    t = pl.program_id(0)

    @pl.when(t < nu_ref[0])
    def _():
        @pl.when(jnp.logical_or(t == 0, te_ref[t] != te_ref[jnp.maximum(t - 1, 0)]))
        def _():
            w1b_ref[...] = w1_ref[0].astype(BF16)
            w3b_ref[...] = w3_ref[0].astype(BF16)
            w2b_ref[...] = w2_ref[0].astype(BF16)

        x = _unpack_pairs(x_ref[...]).astype(BF16)
        act = _silu(_dot(x, w1b_ref[...])) * _dot(x, w3b_ref[...])
        o_ref[...] = _pack_pairs(_dot(act.astype(BF16), w2b_ref[...]))


def moe_experts(x_sorted, w1, w3, w2, tile_expert, n_used, tm):
    pmax, dp = x_sorted.shape
    d = 2 * dp
    ff = w1.shape[2]
    used = lambda t, nu: jnp.minimum(t, nu[0] - 1)
    grid_spec = pltpu.PrefetchScalarGridSpec(
        num_scalar_prefetch=2,
        grid=(pmax // tm,),
        in_specs=[pl.BlockSpec((tm, dp), lambda t, te, nu: (used(t, nu), 0)),
                  pl.BlockSpec((1, d, ff), lambda t, te, nu: (te[t], 0, 0)),
                  pl.BlockSpec((1, d, ff), lambda t, te, nu: (te[t], 0, 0)),
                  pl.BlockSpec((1, ff, d), lambda t, te, nu: (te[t], 0, 0))],
        out_specs=pl.BlockSpec((tm, dp), lambda t, te, nu: (used(t, nu), 0)),
        scratch_shapes=[pltpu.VMEM((d, ff), BF16), pltpu.VMEM((d, ff), BF16), pltpu.VMEM((ff, d), BF16)],
    )
    return pl.pallas_call(
        _moe_expert_kernel,
        grid_spec=grid_spec,
        out_shape=jax.ShapeDtypeStruct((pmax, dp), jnp.uint32),
        compiler_params=_cparams("arbitrary"),
        name="moe_experts",
    )(tile_expert, n_used, x_sorted, w1, w3, w2)


def _moe_combine_kernel(p1_ref, p2_ref, p1n_ref, p2n_ref, ys_ref, res_ref, route_ref, gate_ref, o_ref,
                        y1_ref, y2_ref, sem, *, tc):
    i = pl.program_id(0)
    slot = i % 2

    def issue(pa_ref, pb_ref, s):
        def body(r, carry):
            pltpu.make_async_copy(ys_ref.at[pl.ds(pa_ref[0, 0, r], 1)], y1_ref.at[s, pl.ds(r, 1)], sem.at[s]).start()
            pltpu.make_async_copy(ys_ref.at[pl.ds(pb_ref[0, 0, r], 1)], y2_ref.at[s, pl.ds(r, 1)], sem.at[s]).start()
            return carry
        lax.fori_loop(0, tc, body, 0, unroll=ISSUE_UNROLL)

    @pl.when(i == 0)
    def _():
        issue(p1_ref, p2_ref, 0)

    @pl.when(i + 1 < pl.num_programs(0))
    def _():
        issue(p1n_ref, p2n_ref, 1 - slot)

    _row_copy_wait(ys_ref.at[pl.ds(0, tc)], y1_ref.at[slot], sem.at[slot])
    _row_copy_wait(ys_ref.at[pl.ds(0, tc)], y2_ref.at[slot], sem.at[slot])
    rec = route_ref[...]
    w1 = rec[:, ROUTE_W1:ROUTE_W1 + 1]
    w2 = rec[:, ROUTE_W2:ROUTE_W2 + 1]
    y = w1 * _unpack_pairs(y1_ref[slot]) + w2 * _unpack_pairs(y2_ref[slot])
    o_ref[...] = res_ref[...] + gate_ref[...] * y


def moe_combine(res, y_sorted, pos1, pos2, route, mod4, row_of_tile, i_gate):
    t, d = res.shape
    n, _, tc = pos1.shape
    cur = lambda i: (i, 0, 0)
    nxt = lambda i: (jnp.minimum(i + 1, n - 1), 0, 0)
    smem = lambda im: pl.BlockSpec((1, 1, tc), im, memory_space=pltpu.SMEM)
    return pl.pallas_call(
        functools.partial(_moe_combine_kernel, tc=tc),
        grid=(n,),
        in_specs=[smem(cur), smem(cur), smem(nxt), smem(nxt),
                  pl.BlockSpec(memory_space=pl.ANY),
                  pl.BlockSpec((tc, d), lambda i: (i, 0)),
                  pl.BlockSpec((tc, route.shape[1]), lambda i: (i, 0)),
                  pl.BlockSpec((None, None, 1, d), lambda i: (row_of_tile(i), i_gate, 0, 0))],
        out_specs=pl.BlockSpec((tc, d), lambda i: (i, 0)),
        out_shape=jax.ShapeDtypeStruct((t, d), F32),
        scratch_shapes=[pltpu.VMEM((2, tc, d // 2), y_sorted.dtype), pltpu.VMEM((2, tc, d // 2), y_sorted.dtype),
                        pltpu.SemaphoreType.DMA((2,))],
        compiler_params=_cparams("arbitrary"),
        name="moe_combine",
    )(pos1, pos2, pos1, pos2, y_sorted, res, route, mod4)


def hier_moe(x, lp, mod4, row_of_batch):
    b, l, d = x.shape
    t = b * l
    tm = 256 if t >= 4096 else 64
    tok, logits = moe_pre(x, lp['norm2_g'], mod4, row_of_batch, 4, 3, lp['w_router'], lp['b_router'])
    tok = tok.reshape(t, d // 2)
    route, cnt = moe_route(logits.reshape(t, -1))
    counts = cnt[0, :N_EXPERTS].astype(jnp.int32)
    padded = ((counts + tm - 1) // tm) * tm
    pend = jnp.cumsum(padded)
    pstart = pend - padded
    pmax = 2 * t + N_EXPERTS * tm
    tile_row = jnp.arange(pmax // tm, dtype=jnp.int32) * tm
    tile_expert = jnp.minimum(jnp.sum(tile_row[:, None] >= pend[None, :], axis=1), N_EXPERTS - 1).astype(jnp.int32)
    n_used = (pend[-1] // tm).astype(jnp.int32).reshape(1)
    pstart_row = jnp.zeros((1, route.shape[1]), F32).at[0, :N_EXPERTS].set(pstart.astype(F32))
    pos12 = moe_pos(route, pstart_row)[:, :2].astype(jnp.int32)
    pos1, pos2 = pos12[:, 0], pos12[:, 1]
    tr = _pick(t, (512, 256, 128, 64))
    tc = _pick(l, (256, 128, 64))
    pos = pos12.reshape(t // tr, 1, 2 * tr)
    x_sorted = moe_dispatch(tok, pos, pend, padded, pmax, tm)
    y_sorted = moe_experts(x_sorted, lp['moe_w1e'], lp['moe_w3e'], lp['moe_w2e'],
                           tile_expert + lp['expert_base'], n_used, tm)
    tiles_per_batch = l // tc
    out = moe_combine(x.reshape(t, d), y_sorted, pos1.reshape(t // tc, 1, tc), pos2.reshape(t // tc, 1, tc),
                      route, mod4, lambda i: row_of_batch(i // tiles_per_batch), 5)
    return out.reshape(b, l, d)


def _final_norm_kernel(x_ref, g_ref, o_ref):
    x = x_ref[0]
    o_ref[0] = x * lax.rsqrt(jnp.mean(x * x, axis=-1, keepdims=True) + EPS) * g_ref[...]


def final_norm(x, g):
    b, l, d = x.shape
    tm = _pick(l, (512, 256, 128, 64))
    spec = pl.BlockSpec((1, tm, d), lambda bi, i: (bi, i, 0))
    return pl.pallas_call(
        _final_norm_kernel,
        grid=(b, l // tm),
        in_specs=[spec, pl.BlockSpec((1, d), lambda bi, i: (0, 0))],
        out_specs=spec,
        out_shape=jax.ShapeDtypeStruct((b, l, d), F32),
        compiler_params=_cparams("parallel", "parallel"),
        name="final_norm",
    )(x, g.reshape(1, d))


def _pad_heads(w, dk):
    d = w.shape[0]
    return jnp.pad(w.reshape(d, HEADS, dk), ((0, 0), (0, 0), (0, HEAD_W - dk))).reshape(d, HEADS * HEAD_W)


def prep_w_in(w_in):
    o = 0
    take = lambda n: (w_in[:, o:o + n], o + n)
    hy, o = take(3 * HY_W)
    gq, o = take(HEADS * GLA_DK)
    gk, o = take(HEADS * GLA_DK)
    gv, o = take(GROUP_W)
    gg, o = take(GROUP_W)
    gr, o = take(2 * GLA_RANK)
    ml, o = take(4 * GROUP_W)
    mg, o = take(16)
    ret, o = take(4 * GROUP_W)
    gk_p = _pad_heads(gk, GLA_DK)
    gk_p = gk_p.at[:, SMALL_R:SMALL_R + 2 * GLA_RANK].set(gr).at[:, SMALL_MLG:SMALL_MLG + 16].set(mg)
    return jnp.concatenate([hy, _pad_heads(gq, GLA_DK), gk_p, gv, gg, ml, ret], axis=1).astype(BF16)


def prep_layer(l, prm):
    lp = {k: v[l] for k, v in prm.items()}
    lp['w_in_p'] = prep_w_in(lp['w_in'])
    lp['w_out_b'] = lp['w_out'].astype(BF16)
    gates = []
    for d in range(2):
        w = jnp.zeros((HEAD_W, HEADS * HEAD_W), F32)
        w = w.at[SMALL_R + d * GLA_RANK:SMALL_R + (d + 1) * GLA_RANK].set(_pad_heads(lp['gla_wa2'][d], GLA_DK))
        w_hi = w.astype(BF16)
        w_lo = (w - w_hi.astype(F32)).astype(BF16)
        gates.append((w_hi, w_lo, _pad_heads(lp['gla_ba'][d][None, :], GLA_DK)))
    lp['gla_gate'] = gates
    bias = jnp.zeros((1, HEAD_W), F32).at[0, SMALL_MLG:SMALL_MLG + 16].set(lp['ml_gate_b'].reshape(-1))
    lp['ml_bias'] = bias
    lp['ret_dec'] = [jnp.repeat(lp['ret_decay'][d], HEAD_W)[None, :] for d in range(2)]
    gains = jnp.zeros((8, GROUP_W), F32)
    lp['gains'] = gains.at[0].set(lp['gla_norm_g']).at[1].set(lp['ml_norm_g']).at[2].set(lp['ret_norm_g'])
    d_model = lp['moe_wg'].shape[0]
    wr = jnp.zeros((d_model, HEAD_W), F32)
    lp['w_router'] = wr.at[:, :MOE_GROUPS].set(lp['moe_wg']).at[:, MOE_GROUPS:MOE_GROUPS + N_EXPERTS].set(lp['moe_we'])
    br = jnp.zeros((1, HEAD_W), F32)
    lp['b_router'] = br.at[0, :MOE_GROUPS].set(lp['moe_bg']).at[0, MOE_GROUPS:MOE_GROUPS + N_EXPERTS].set(lp['moe_be'])
    lp['moe_w1e'] = prm['moe_w1'].reshape(-1, d_model, MOE_FF)
    lp['moe_w3e'] = prm['moe_w3'].reshape(-1, d_model, MOE_FF)
    lp['moe_w2e'] = prm['moe_w2'].reshape(-1, MOE_FF, d_model)
    lp['expert_base'] = l * N_EXPERTS
    return lp


def rotary_tables(length):
    rows = length // GRID_W
    row = jnp.repeat(jnp.arange(rows, dtype=F32), GRID_W)
    col = jnp.tile(jnp.arange(GRID_W, dtype=F32), rows)
    nf = HEAD_W // 4
    inv = ROPE_BASE ** (-jnp.arange(nf, dtype=F32) / nf)
    ang = jnp.concatenate([row[:, None] * inv, col[:, None] * inv], axis=-1)
    cos, sin = jnp.cos(ang), jnp.sin(ang)
    return jnp.concatenate([cos, cos], axis=-1), jnp.concatenate([-sin, sin], axis=-1)


def bidir_scans(pg_ctx, pg_lat, lp, rot):
    (p_ctx, g_ctx), (p_lat, g_lat) = pg_ctx, pg_lat
    b = p_lat.shape[0]
    zeros_s = jnp.zeros((b, HEADS, HEAD_W, HEAD_W), F32)
    zeros_v = jnp.zeros((b, HEADS, 1, HEAD_W), F32)
    out = {}
    for name, mode, offs, scale in (("gla", "gla", (OFF_GLA_Q, OFF_GLA_K, OFF_GLA_V), GLA_DK ** -0.5),
                                    ("ret", "ret", (OFF_RET_Q, OFF_RET_K, OFF_RET_V), HEAD_W ** -0.5)):
        o_c, o_l = None, None
        for d in range(2):
            gate = lp['gla_gate'][d] if mode == "gla" else lp['ret_dec'][d]
            o_c, st = gla_scan(p_ctx, g_ctx, *offs, zeros_s, reverse=d == 1, mode=mode, gate=gate,
                               o_prev=o_c, q_scale=scale)
            o_l, _ = gla_scan(p_lat, g_lat, *offs, st, reverse=d == 1, mode=mode, gate=gate,
                              rot=rot if mode == "ret" else None, o_prev=o_l, q_scale=scale)
        out[name] = (o_c, o_l)
    o_c, o_l = None, None
    for d in range(2):
        o_c, st = mlstm_scan(p_ctx, g_ctx, (zeros_s, zeros_v, zeros_v), lp['ml_bias'], reverse=d == 1, o_prev=o_c)
        o_l, _ = mlstm_scan(p_lat, g_lat, st, lp['ml_bias'], reverse=d == 1, o_prev=o_l)
    out["ml"] = (o_c, o_l)
    return out


def kernel(x, c, ctx, c_ctx, ada_w, ada_b, norm1_g, norm2_g, w_in, hy_conv_w, hy_conv_b, hy_f_w1, hy_f_b1, hy_f_w2, hy_f_b2, hy_f_freq, hy_f_w3, hy_decay, hy_skip, gla_wa2, gla_ba, gla_norm_g, ml_gate_b, ml_norm_g, ret_decay, ret_norm_g, w_out, moe_wg, moe_bg, moe_we, moe_be, moe_w1, moe_w3, moe_w2, final_g):
    prm = dict(norm1_g=norm1_g, norm2_g=norm2_g, w_in=w_in, hy_conv_w=hy_conv_w, hy_conv_b=hy_conv_b,
               hy_f_w1=hy_f_w1, hy_f_b1=hy_f_b1, hy_f_w2=hy_f_w2, hy_f_b2=hy_f_b2, hy_f_freq=hy_f_freq,
               hy_f_w3=hy_f_w3, hy_decay=hy_decay, hy_skip=hy_skip, gla_wa2=gla_wa2, gla_ba=gla_ba,
               gla_norm_g=gla_norm_g, ml_gate_b=ml_gate_b, ml_norm_g=ml_norm_g, ret_decay=ret_decay,
               ret_norm_g=ret_norm_g, w_out=w_out, moe_wg=moe_wg, moe_bg=moe_bg, moe_we=moe_we,
               moe_be=moe_be, moe_w1=moe_w1, moe_w3=moe_w3, moe_w2=moe_w2)
    depth = ada_w.shape[0]
    b, seq, d = x.shape
    lc = ctx.shape[1]
    lat = x.astype(F32)
    cx = ctx.astype(F32)
    cvec = jnp.zeros((8, d), F32).at[:b].set(c.astype(F32)).at[b].set(c_ctx.astype(F32))
    mod_all = adaln(cvec, ada_w, ada_b)
    lat_row = lambda bi: bi
    ctx_row = lambda bi: b
    rot = rotary_tables(seq)
    tab_lat = dft_tables(seq // 2)
    tab_ctx = dft_tables(lc // 2)
    for l in range(depth):
        with_ctx = l < depth - 1
        lp = prep_layer(l, prm)
        mod4 = mod_all[l].reshape(8, 6, 1, d)
        p_lat, g_lat = inproj(lat, lp['norm1_g'], mod4, lat_row, 1, 0, lp['w_in_p'])
        p_ctx, g_ctx = inproj(cx, lp['norm1_g'], mod4, ctx_row, 1, 0, lp['w_in_p'])
        scans = bidir_scans((p_ctx, g_ctx), (p_lat, g_lat), lp, rot)
        hy_lat = hyena_mixer(p_lat, lp, tab_lat)
        lat = outproj(hy_lat, scans["gla"][1], scans["ml"][1], scans["ret"][1], p_lat, lp['gains'],
                      lp['w_out_b'], lat, mod4, lat_row, 2)
        lat = hier_moe(lat, lp, mod4, lat_row)
        if with_ctx:
            hy_ctx = hyena_mixer(p_ctx, lp, tab_ctx)
            cx = outproj(hy_ctx, scans["gla"][0], scans["ml"][0], scans["ret"][0], p_ctx, lp['gains'],
                         lp['w_out_b'], cx, mod4, ctx_row, 2)
            cx = hier_moe(cx, lp, mod4, ctx_row)
    return final_norm(lat, final_g).astype(x.dtype)
```

```python
import functools
import math

import jax
import jax.numpy as jnp
from jax import lax
from jax.experimental import pallas as pl
from jax.experimental.pallas import tpu as pltpu

F32 = jnp.float32
BF16 = jnp.bfloat16
HIGHEST = lax.Precision.HIGHEST

EPS = 1e-6
CHUNK = 64
SCAN_BLOCK = 256
NORM_ROWS = 128
HALO = 16
GRID_W = 64
ROPE_BASE = 10000.0
HEADS = 4
HEAD_W = 128
HY_W = 512
HY_EMB = 33
GLA_DK = 64
GLA_RANK = 16
GLA_TAU = 16.0
MOE_GROUPS = 4
MOE_EPG = 8
MOE_FF = 256
N_EXPERTS = MOE_GROUPS * MOE_EPG

GROUP_W = 512
OFF_HY = 0
OFF_GLA_Q = 1536
OFF_GLA_K = OFF_GLA_Q + GROUP_W
OFF_GLA_V = OFF_GLA_K + GROUP_W
OFF_GLA_G = OFF_GLA_V + GROUP_W
OFF_ML_Q = OFF_GLA_G + GROUP_W
OFF_ML_K = OFF_ML_Q + GROUP_W
OFF_ML_V = OFF_ML_K + GROUP_W
OFF_ML_O = OFF_ML_V + GROUP_W
OFF_RET_Q = OFF_ML_O + GROUP_W
OFF_RET_K = OFF_RET_Q + GROUP_W
OFF_RET_V = OFF_RET_K + GROUP_W
OFF_RET_G = OFF_RET_V + GROUP_W
N_PROJ = OFF_RET_G + GROUP_W
SMALL_BASE = GLA_DK
SMALL_R = SMALL_BASE
SMALL_MLG = SMALL_BASE + 2 * GLA_RANK

VMEM_LIMIT_BYTES = 56 * 1024 * 1024


def _cparams(*sem):
    return pltpu.CompilerParams(dimension_semantics=sem, vmem_limit_bytes=VMEM_LIMIT_BYTES)


def _pick(n, candidates):
    for c in candidates:
        if n % c == 0:
            return c
    return n


def _silu(x):
    return x / (1.0 + jnp.exp(-x))


def _sigmoid(x):
    return 1.0 / (1.0 + jnp.exp(-x))


def _log_sigmoid(x):
    return jnp.minimum(x, 0.0) - jnp.log(1.0 + jnp.exp(-jnp.abs(x)))


def _dot(a, b):
    return jnp.dot(a, b, preferred_element_type=F32)


def _dot_hi(a, b):
    return jnp.dot(a, b, preferred_element_type=F32, precision=HIGHEST)


def _split_bf16(x, parts):
    out = []
    for _ in range(parts - 1):
        hi = x.astype(BF16)
        out.append(hi)
        x = x - hi.astype(F32)
    out.append(x.astype(BF16))
    return out


def _dot_exact_lhs(a_bf16, x):
    return sum(_dot(a_bf16, p) for p in _split_bf16(x, 3))


def _pack_pairs(x):
    h = x.shape[1] // 2
    bits = lambda a: lax.bitcast_convert_type(a.astype(BF16).astype(F32), jnp.uint32)
    return bits(x[:, :h]) | (bits(x[:, h:]) >> 16)


def _unpack_pairs(w):
    hi = lax.bitcast_convert_type(w & jnp.uint32(0xFFFF0000), F32)
    lo = lax.bitcast_convert_type(w << 16, F32)
    return jnp.concatenate([hi, lo], axis=-1)


def _dot_nt(a, b):
    return lax.dot_general(a, b, (((1,), (1,)), ((), ())), preferred_element_type=F32)


def _dot_tn(a, b):
    return lax.dot_general(a, b, (((0,), (0,)), ((), ())), preferred_element_type=F32)


def _adaln_kernel(c_ref, w_ref, b_ref, o_ref):
    o_ref[0] = _dot_hi(_silu(c_ref[...]), w_ref[0]) + b_ref[0]


def adaln(cvec, ada_w, ada_b):
    depth, d, n = ada_w.shape
    tn = _pick(n, (1024, 512, 256, 128))
    return pl.pallas_call(
        _adaln_kernel,
        grid=(depth, n // tn),
        in_specs=[pl.BlockSpec((8, d), lambda l, j: (0, 0)),
                  pl.BlockSpec((1, d, tn), lambda l, j: (l, 0, j)),
                  pl.BlockSpec((1, 1, tn), lambda l, j: (l, 0, j))],
        out_specs=pl.BlockSpec((1, 8, tn), lambda l, j: (l, 0, j)),
        out_shape=jax.ShapeDtypeStruct((depth, 8, n), F32),
        compiler_params=_cparams("parallel", "parallel"),
        name="adaln",
    )(cvec, ada_w, ada_b.reshape(depth, 1, n))


def _norm_mod(x, g, sc, sh):
    ms = jnp.mean(x * x, axis=-1, keepdims=True)
    return (x * lax.rsqrt(ms + EPS) * g) * (1.0 + sc) + sh


def _inproj_kernel(x_ref, g_ref, sc_ref, sh_ref, w_ref, o_ref, gate_ref, xn_ref, *, gate_tile, gate_off):
    j = pl.program_id(2)

    @pl.when(j == 0)
    def _():
        tm = xn_ref.shape[0]
        rc = min(tm, NORM_ROWS)
        for r in range(0, tm, rc):
            xn = _norm_mod(x_ref[0, r:r + rc, :], g_ref[...], sc_ref[...], sh_ref[...]).astype(BF16)
            xn_ref[r:r + rc, :] = xn
            o_ref[0, r:r + rc, :] = _dot(xn, w_ref[...]).astype(BF16)

    @pl.when(j != 0)
    def _():
        acc = _dot(xn_ref[...], w_ref[...])
        o_ref[0] = acc.astype(BF16)

        @pl.when(j == gate_tile)
        def _():
            gate_ref[0] = acc[:, gate_off:gate_off + HEAD_W]


def inproj(x, g, mod4, row_of_batch, i_sc, i_sh, w):
    b, l, d = x.shape
    n = w.shape[1]
    tm = _pick(l, (1024, 512, 256, 128, 64))
    tn = _pick(n, (768, 512, 256, 128))
    assert OFF_GLA_K // tn > 0 and OFF_GLA_K % tn + HEAD_W <= tn
    kern = functools.partial(_inproj_kernel, gate_tile=OFF_GLA_K // tn, gate_off=OFF_GLA_K % tn)
    return pl.pallas_call(
        kern,
        grid=(b, l // tm, n // tn),
        in_specs=[pl.BlockSpec((1, tm, d), lambda bi, i, j: (bi, i, 0)),
                  pl.BlockSpec((1, d), lambda bi, i, j: (0, 0)),
                  pl.BlockSpec((None, None, 1, d), lambda bi, i, j: (row_of_batch(bi), i_sc, 0, 0)),
                  pl.BlockSpec((None, None, 1, d), lambda bi, i, j: (row_of_batch(bi), i_sh, 0, 0)),
                  pl.BlockSpec((d, tn), lambda bi, i, j: (0, j))],
        out_specs=[pl.BlockSpec((1, tm, tn), lambda bi, i, j: (bi, i, j)),
                   pl.BlockSpec((1, tm, HEAD_W), lambda bi, i, j: (bi, i, 0))],
        out_shape=[jax.ShapeDtypeStruct((b, l, n), BF16), jax.ShapeDtypeStruct((b, l, HEAD_W), F32)],
        scratch_shapes=[pltpu.VMEM((tm, d), BF16)],
        compiler_params=_cparams("parallel", "parallel", "arbitrary"),
        name="inproj",
    )(x, g.reshape(1, d), mod4, mod4, w)


def _causal_mask(n, reverse):
    r = lax.broadcasted_iota(jnp.int32, (n, n), 0)
    c = lax.broadcasted_iota(jnp.int32, (n, n), 1)
    same_chunk = (r // CHUNK) == (c // CHUNK)
    return jnp.logical_and(same_chunk, (c >= r) if reverse else (c <= r))


def _gla_scan_kernel(*refs, reverse, mode, rotate, accumulate, n_chunks, q_scale):
    it = iter(refs)
    q_ref, k_ref, v_ref = next(it), next(it), next(it)
    if mode == "gla":
        gsrc_ref, wah_ref, wal_ref, ba_ref = next(it), next(it), next(it), next(it)
    else:
        dec_ref = next(it)
    if rotate:
        cos_ref, sin_ref = next(it), next(it)
    s0_ref = next(it)
    if accumulate:
        oprev_ref = next(it)
    o_ref, sfin_ref, st_ref = next(it), next(it), next(it)
    dterm_refs = list(it)

    i = pl.program_id(1)

    @pl.when(i == 0)
    def _():
        st_ref[...] = s0_ref[0]

    tb = n_chunks * CHUNK
    mask = _causal_mask(tb, reverse)
    maskb = mask.astype(BF16)
    order = [(n_chunks - 1 - c) if reverse else c for c in range(n_chunks)]
    rows = [slice(c * CHUNK, (c + 1) * CHUNK) for c in range(n_chunks)]

    q = q_ref[0].astype(F32)
    k = k_ref[0].astype(F32)
    vb = v_ref[0]

    def decay_terms(la):
        bc = _dot_exact_lhs(maskb, la)
        tots = [bc[r.start:r.start + 1, :] if reverse else bc[r.stop - 1:r.stop, :] for r in rows]
        tot_rows = jnp.concatenate([jnp.broadcast_to(t, (CHUNK, GROUP_W)) for t in tots], axis=0)
        tot_pad = tots + [jnp.zeros_like(tots[0])] * (-n_chunks % 8)
        return jnp.exp(bc), jnp.exp(-bc), jnp.exp(tot_rows - bc), jnp.exp(jnp.concatenate(tot_pad, axis=0))

    if mode == "gla":
        g_hi, g_lo = _split_bf16(gsrc_ref[0], 2)
        logit = (_dot(g_hi, wah_ref[...]) + _dot(g_lo, wah_ref[...]) + _dot(g_hi, wal_ref[...])) + ba_ref[...]
        e_bc, e_nbc, e_st, e_tot = decay_terms(_log_sigmoid(logit) * (1.0 / GLA_TAU))
    else:
        @pl.when(i == 0)
        def _():
            terms = decay_terms(jnp.broadcast_to(_log_sigmoid(dec_ref[...]), (tb, GROUP_W)))
            for t_ref, t in zip(dterm_refs, terms):
                t_ref[...] = t

        e_bc, e_nbc, e_st, e_tot = (t_ref[...] for t_ref in dterm_refs)
    e_tots = [e_tot[c:c + 1, :] for c in range(n_chunks)]
    if rotate:
        cos2 = cos_ref[...]
        sin2 = sin_ref[...]

    outs = []
    for h in range(HEADS):
        hs = slice(h * HEAD_W, (h + 1) * HEAD_W)
        qh, kh, vh = q[:, hs], k[:, hs], vb[:, hs]
        if rotate:
            qh = qh * cos2 + pltpu.roll(qh, HEAD_W // 2, axis=1) * sin2
            kh = kh * cos2 + pltpu.roll(kh, HEAD_W // 2, axis=1) * sin2
        q_in = (qh * q_scale * e_bc[:, hs]).astype(BF16)
        k_in = (kh * e_nbc[:, hs]).astype(BF16)
        k_st = (kh * e_st[:, hs]).astype(BF16)
        att = jnp.where(mask, _dot_nt(q_in, k_in), 0.0).astype(BF16)
        intra = _dot(att, vh)
        st = st_ref[h]
        inter = [None] * n_chunks
        for cc in order:
            r = rows[cc]
            inter[cc] = _dot_nt(q_in[r], st.astype(BF16))
            st = st * e_tots[cc][:, hs] + _dot_tn(vh[r], k_st[r])
        st_ref[h] = st
        outs.append(intra + jnp.concatenate(inter, axis=0))
    o = jnp.concatenate(outs, axis=-1)
    if accumulate:
        o = o + oprev_ref[0]
    o_ref[0] = o

    @pl.when(i == pl.num_programs(1) - 1)
    def _():
        sfin_ref[0] = st_ref[...]


def gla_scan(p, gblock, off_q, off_k, off_v, s0, *, reverse, mode, gate=None, rot=None, o_prev=None, q_scale):
    b, l, _ = p.shape
    tb = _pick(l, (SCAN_BLOCK, 128, 64))
    nblk = l // tb
    blk = (lambda i: nblk - 1 - i) if reverse else (lambda i: i)
    gspec = lambda off: pl.BlockSpec((1, tb, GROUP_W), lambda bi, i: (bi, blk(i), off // GROUP_W))
    const = lambda shape: pl.BlockSpec(shape, lambda bi, i: (0,) * len(shape))
    in_specs = [gspec(off_q), gspec(off_k), gspec(off_v)]
    args = [p, p, p]
    if mode == "gla":
        wa_hi, wa_lo, ba = gate
        in_specs += [pl.BlockSpec((1, tb, HEAD_W), lambda bi, i: (bi, blk(i), 0)),
                     const((HEAD_W, GROUP_W)), const((HEAD_W, GROUP_W)), const((1, GROUP_W))]
        args += [gblock, wa_hi, wa_lo, ba]
    else:
        in_specs += [const((1, GROUP_W))]
        args += [gate]
    if rot is not None:
        in_specs += [pl.BlockSpec((tb, HEAD_W), lambda bi, i: (blk(i), 0))] * 2
        args += [rot[0], rot[1]]
    sspec = pl.BlockSpec((1, HEADS, HEAD_W, HEAD_W), lambda bi, i: (bi, 0, 0, 0))
    in_specs += [sspec]
    args += [s0]
    o_spec = pl.BlockSpec((1, tb, GROUP_W), lambda bi, i: (bi, blk(i), 0))
    aliases = {}
    if o_prev is not None:
        in_specs += [o_spec]
        aliases = {len(args): 0}
        args += [o_prev]
    kern = functools.partial(_gla_scan_kernel, reverse=reverse, mode=mode, rotate=rot is not None,
                             accumulate=o_prev is not None, n_chunks=tb // CHUNK, q_scale=q_scale)
    return pl.pallas_call(
        kern,
        grid=(b, nblk),
        in_specs=in_specs,
        out_specs=[o_spec, sspec],
        out_shape=[jax.ShapeDtypeStruct((b, l, GROUP_W), F32),
                   jax.ShapeDtypeStruct((b, HEADS, HEAD_W, HEAD_W), F32)],
        scratch_shapes=[pltpu.VMEM((HEADS, HEAD_W, HEAD_W), F32)] + (
            [pltpu.VMEM((tb, GROUP_W), F32)] * 3 + [pltpu.VMEM((8 * pl.cdiv(tb // CHUNK, 8), GROUP_W), F32)]
            if mode == "ret" else []),
        input_output_aliases=aliases,
        compiler_params=_cparams("parallel", "arbitrary"),
        name=f"{mode}_scan_{'bwd' if reverse else 'fwd'}",
    )(*args)


def _mlstm_kernel(*refs, reverse, accumulate, n_chunks, k_scale):
    it = iter(refs)
    q_ref, k_ref, v_ref, gsrc_ref, bias_ref = (next(it) for _ in range(5))
    c0_ref, n0_ref, m0_ref = next(it), next(it), next(it)
    if accumulate:
        oprev_ref = next(it)
    o_ref, cfin_ref, nfin_ref, mfin_ref = next(it), next(it), next(it), next(it)
    ct_ref, n_ref, m_ref = next(it), next(it), next(it)

    i = pl.program_id(1)

    @pl.when(i == 0)
    def _():
        ct_ref[...] = c0_ref[0]
        n_ref[...] = n0_ref[0]
        m_ref[...] = m0_ref[0]

    tb = n_chunks * CHUNK
    mask = _causal_mask(tb, reverse)
    maskb = mask.astype(BF16)
    order = [(n_chunks - 1 - c) if reverse else c for c in range(n_chunks)]
    rows = [slice(c * CHUNK, (c + 1) * CHUNK) for c in range(n_chunks)]
    lane_i = SMALL_MLG + (8 if reverse else 0)
    lane_f = lane_i + HEADS

    q = q_ref[0].astype(F32)
    k = k_ref[0].astype(F32) * k_scale
    vb = v_ref[0]
    gates = gsrc_ref[0] + bias_ref[...]
    bc_all = _dot_exact_lhs(maskb, _log_sigmoid(gates))
    rep = lambda a, c: jnp.broadcast_to(a[:, c:c + 1], (a.shape[0], HEAD_W))
    outs = []
    for h in range(HEADS):
        hs = slice(h * HEAD_W, (h + 1) * HEAD_W)
        qh, kh, vh = q[:, hs], k[:, hs], vb[:, hs]
        bch = rep(bc_all, lane_f + h)
        wh = rep(gates, lane_i + h) - bch
        bends = [bch[r.start:r.start + 1, :] if reverse else bch[r.stop - 1:r.stop, :] for r in rows]
        w_row = jnp.broadcast_to(jnp.transpose(wh)[0:1, :], (tb, tb))
        dmat = jnp.where(mask, bch[:, 0:1] + w_row, -jnp.inf)
        rowmax = jnp.max(dmat, axis=-1, keepdims=True)
        qb = qh.astype(BF16)
        s_raw = _dot_nt(qb, kh.astype(BF16)) * jnp.exp(dmat - rowmax)
        sv = _dot(s_raw.astype(BF16), vh)
        s_sum = jnp.sum(s_raw, axis=-1, keepdims=True)
        ct, nv, m_prev = ct_ref[h], n_ref[h], m_ref[h]
        out = [None] * n_chunks
        for cc in order:
            r = rows[cc]
            bend = bends[cc]
            glh = bend + wh[r]
            gmax = jnp.max(glh, axis=0, keepdims=True)
            gk = jnp.exp(glh - gmax) * kh[r]
            inter_log = bch[r] + m_prev
            m_t = jnp.maximum(inter_log, rowmax[r])
            e_intra = jnp.exp(rowmax[r] - m_t)
            inter = jnp.exp(inter_log - m_t)
            num = inter * _dot_nt(qb[r], ct.astype(BF16)) + e_intra * sv[r]
            den = jnp.abs(inter * jnp.sum(qh[r] * nv, axis=-1, keepdims=True) + e_intra * s_sum[r])
            out[cc] = num / jnp.maximum(den, jnp.exp(-m_t))
            m_new = jnp.maximum(bend + m_prev, gmax)
            dec = jnp.exp(bend + m_prev - m_new)
            e_upd = jnp.exp(gmax - m_new)
            ct = dec * ct + e_upd * _dot_tn(vh[r], gk.astype(BF16))
            nv = dec * nv + e_upd * jnp.sum(gk, axis=0, keepdims=True)
            m_prev = m_new
        ct_ref[h] = ct
        n_ref[h] = nv
        m_ref[h] = m_prev
        outs.append(jnp.concatenate(out, axis=0))
    o = jnp.concatenate(outs, axis=-1)
    if accumulate:
        o = o + oprev_ref[0]
    o_ref[0] = o

    @pl.when(i == pl.num_programs(1) - 1)
    def _():
        cfin_ref[0] = ct_ref[...]
        nfin_ref[0] = n_ref[...]
        mfin_ref[0] = m_ref[...]


def mlstm_scan(p, gblock, state, gate_bias, *, reverse, o_prev=None):
    b, l, _ = p.shape
    tb = _pick(l, (SCAN_BLOCK, 128, 64))
    nblk = l // tb
    blk = (lambda i: nblk - 1 - i) if reverse else (lambda i: i)
    c0, n0, m0 = state
    gspec = lambda off: pl.BlockSpec((1, tb, GROUP_W), lambda bi, i: (bi, blk(i), off // GROUP_W))
    const = lambda shape: pl.BlockSpec(shape, lambda bi, i: (0,) * len(shape))
    cspec = pl.BlockSpec((1, HEADS, HEAD_W, HEAD_W), lambda bi, i: (bi, 0, 0, 0))
    vspec = pl.BlockSpec((1, HEADS, 1, HEAD_W), lambda bi, i: (bi, 0, 0, 0))
    in_specs = [gspec(OFF_ML_Q), gspec(OFF_ML_K), gspec(OFF_ML_V),
                pl.BlockSpec((1, tb, HEAD_W), lambda bi, i: (bi, blk(i), 0)),
                const((1, HEAD_W)), cspec, vspec, vspec]
    args = [p, p, p, gblock, gate_bias, c0, n0, m0]
    o_spec = pl.BlockSpec((1, tb, GROUP_W), lambda bi, i: (bi, blk(i), 0))
    aliases = {}
    if o_prev is not None:
        in_specs += [o_spec]
        aliases = {len(args): 0}
        args += [o_prev]
    kern = functools.partial(_mlstm_kernel, reverse=reverse, accumulate=o_prev is not None,
                             n_chunks=tb // CHUNK, k_scale=HEAD_W ** -0.5)
    outs = pl.pallas_call(
        kern,
        grid=(b, nblk),
        in_specs=in_specs,
        out_specs=[o_spec, cspec, vspec, vspec],
        out_shape=[jax.ShapeDtypeStruct((b, l, GROUP_W), F32),
                   jax.ShapeDtypeStruct((b, HEADS, HEAD_W, HEAD_W), F32),
                   jax.ShapeDtypeStruct((b, HEADS, 1, HEAD_W), F32),
                   jax.ShapeDtypeStruct((b, HEADS, 1, HEAD_W), F32)],
        scratch_shapes=[pltpu.VMEM((HEADS, HEAD_W, HEAD_W), F32), pltpu.VMEM((HEADS, 1, HEAD_W), F32),
                        pltpu.VMEM((HEADS, 1, HEAD_W), F32)],
        input_output_aliases=aliases,
        compiler_params=_cparams("parallel", "arbitrary"),
        name=f"mlstm_scan_{'bwd' if reverse else 'fwd'}",
    )(*args)
    return outs[0], (outs[1], outs[2], outs[3])


def _hy_pre_kernel(u_ref, up_ref, un_ref, w_ref, b_ref, o_ref, ob_ref, y_ref, *, rows):
    i = pl.program_id(2)
    u = u_ref[0].astype(F32)
    prev_row = jnp.where(i == 0, 0.0, up_ref[0, HALO - 1:HALO, :].astype(F32))
    next_row = jnp.where(i == pl.num_programs(2) - 1, 0.0, un_ref[0, 0:1, :].astype(F32))
    ridx = lax.broadcasted_iota(jnp.int32, u.shape, 0)
    u_dn = jnp.where(ridx == 0, prev_row, pltpu.roll(u, 1, axis=0))
    u_up = jnp.where(ridx == rows - 1, next_row, pltpu.roll(u, rows - 1, axis=0))
    y = w_ref[0:1, :] * u_dn + w_ref[1:2, :] * u + w_ref[2:3, :] * u_up + b_ref[...]
    lane_tiles = HY_W // HEAD_W
    for j in range(lane_tiles):
        y_ref[j] = y[:, j * HEAD_W:(j + 1) * HEAD_W]
    for par in range(2):
        plane = jnp.concatenate([y_ref[j, pl.ds(par, rows // 2, stride=2), :] for j in range(lane_tiles)], axis=-1)
        o_ref[0, par] = plane
        ob_ref[0, par] = plane.astype(BF16)


def hy_pre(p, conv_w, conv_b):
    b, l, _ = p.shape
    rows = _pick(l, (512, 256, 128, 64))
    nr = l // rows
    r8 = rows // HALO
    n8 = l // HALO
    wpad = jnp.zeros((8, 3 * HY_W), F32).at[:3].set(conv_w)
    return pl.pallas_call(
        functools.partial(_hy_pre_kernel, rows=rows),
        grid=(b, 3, nr),
        in_specs=[pl.BlockSpec((1, rows, HY_W), lambda bi, j, i: (bi, i, j)),
                  pl.BlockSpec((1, HALO, HY_W), lambda bi, j, i: (bi, jnp.maximum(i * r8 - 1, 0), j)),
                  pl.BlockSpec((1, HALO, HY_W), lambda bi, j, i: (bi, jnp.minimum((i + 1) * r8, n8 - 1), j)),
                  pl.BlockSpec((8, HY_W), lambda bi, j, i: (0, j)),
                  pl.BlockSpec((1, HY_W), lambda bi, j, i: (0, j))],
        out_specs=[pl.BlockSpec((1, 2, rows // 2, HY_W), lambda bi, j, i: (bi, 0, i, j))] * 2,
        out_shape=[jax.ShapeDtypeStruct((b, 2, l // 2, 3 * HY_W), F32),
                   jax.ShapeDtypeStruct((b, 2, l // 2, 3 * HY_W), BF16)],
        scratch_shapes=[pltpu.VMEM((HY_W // HEAD_W, rows, HEAD_W), F32)],
        compiler_params=_cparams("parallel", "parallel", "parallel"),
        name="hy_shortconv",
    )(p, p, p, wpad, conv_b.reshape(1, 3 * HY_W))


def _mm_kernel(a_ref, b_ref, o_ref):
    o_ref[...] = _dot(a_ref[...], b_ref[...])


def matmul_bf16(a, bm):
    m, k = a.shape
    n = bm.shape[1]
    tm = _pick(m, (512, 256, 128, 64))
    tn = _pick(n, (512, 256, 128))
    return pl.pallas_call(
        _mm_kernel,
        grid=(n // tn, m // tm),
        in_specs=[pl.BlockSpec((tm, k), lambda j, i: (i, 0)),
                  pl.BlockSpec((k, tn), lambda j, i: (0, j))],
        out_specs=pl.BlockSpec((tm, tn), lambda j, i: (i, j)),
        out_shape=jax.ShapeDtypeStruct((m, n), F32),
        compiler_params=_cparams("parallel", "parallel"),
        name="matmul_bf16",
    )(a, bm)


def _dft_fwd_kernel(c_ref, s_ref, z_ref, cp_ref, sp_ref, hr_ref, hi_ref, hbr_ref, hbi_ref, ur_ref, ui_ref):
    c = c_ref[...]
    s = s_ref[...]
    z0 = z_ref[0, 0]
    z1 = z_ref[0, 1]
    e0r, e0i = _dot(c, z0), -_dot(s, z0)
    e1r, e1i = _dot(c, z1), -_dot(s, z1)
    cp = cp_ref[...]
    sp = sp_ref[...]
    t1r = cp * e1r + sp * e1i
    t1i = cp * e1i - sp * e1r
    ar, ai, br, bi = e0r + t1r, e0i + t1i, e0r - t1r, e0i - t1i
    hr, hi, hbr, hbi = hr_ref[...], hi_ref[...], hbr_ref[...], hbi_ref[...]
    pr = ar * hr - ai * hi
    pim = ar * hi + ai * hr
    qr = br * hbr - bi * hbi
    qim = br * hbi + bi * hbr
    wr, wim = pr - qr, pim - qim
    ur_ref[0, 0] = (pr + qr).astype(BF16)
    ui_ref[0, 0] = (pim + qim).astype(BF16)
    ur_ref[0, 1] = (cp * wr - sp * wim).astype(BF16)
    ui_ref[0, 1] = (sp * wr + cp * wim).astype(BF16)


def dft_fwd(cmat, smat, zb, z_col, cpsi, spsi, h4, h_col):
    b, _, m, _ = zb.shape
    tf = _pick(m, (512, 256, 128, 64))
    hspec = pl.BlockSpec((tf, HY_W), lambda i, bi: (i, h_col))
    vspec = pl.BlockSpec((tf, 1), lambda i, bi: (i, 0))
    ospec = pl.BlockSpec((1, 2, tf, HY_W), lambda i, bi: (bi, 0, i, 0))
    return pl.pallas_call(
        _dft_fwd_kernel,
        grid=(m // tf, b),
        in_specs=[pl.BlockSpec((tf, m), lambda i, bi: (i, 0)),
                  pl.BlockSpec((tf, m), lambda i, bi: (i, 0)),
                  pl.BlockSpec((1, 2, m, HY_W), lambda i, bi: (bi, 0, 0, z_col)),
                  vspec, vspec, hspec, hspec, hspec, hspec],
        out_specs=[ospec, ospec],
        out_shape=[jax.ShapeDtypeStruct((b, 2, m, HY_W), BF16)] * 2,
        compiler_params=_cparams("parallel", "parallel"),
        name="hy_dft_fwd",
    )(cmat, smat, zb, cpsi, spsi, *h4)


def _dft_inv_kernel(c_ref, s_ref, ur_ref, ui_ref, x_ref, zp_ref, skip_ref, *o_refs, interleave):
    c = c_ref[...]
    s = s_ref[...]
    tt = c.shape[0]
    lane_tiles = HY_W // HEAD_W
    for par in range(2):
        conv = _dot(c, ur_ref[0, par]) - _dot(s, ui_ref[0, par])
        z = x_ref[0, par] * (conv + skip_ref[...] * zp_ref[0, par])
        if interleave:
            for j in range(lane_tiles):
                o_refs[1][j, pl.ds(par, tt, stride=2), :] = z[:, j * HEAD_W:(j + 1) * HEAD_W]
        else:
            o_refs[0][0, par] = z
            o_refs[1][0, par] = z.astype(BF16)
    if interleave:
        o_refs[0][0] = jnp.concatenate([o_refs[1][j] for j in range(lane_tiles)], axis=-1)


def dft_inv(cmat, smat, ur, ui, x_arr, x_col, zp_arr, zp_col, skip, interleave):
    b, _, m, _ = ur.shape
    tt = _pick(m, (512, 256, 128, 64))
    uspec = pl.BlockSpec((1, 2, m, HY_W), lambda i, bi: (bi, 0, 0, 0))
    pspec = lambda col: pl.BlockSpec((1, 2, tt, HY_W), lambda i, bi: (bi, 0, i, col))
    if interleave:
        out_specs = [pl.BlockSpec((1, 2 * tt, HY_W), lambda i, bi: (bi, i, 0))]
        out_shape = [jax.ShapeDtypeStruct((b, 2 * m, HY_W), F32)]
    else:
        out_specs = [pspec(0), pspec(0)]
        out_shape = [jax.ShapeDtypeStruct((b, 2, m, HY_W), F32), jax.ShapeDtypeStruct((b, 2, m, HY_W), BF16)]
    return pl.pallas_call(
        functools.partial(_dft_inv_kernel, interleave=interleave),
        grid=(m // tt, b),
        in_specs=[pl.BlockSpec((tt, m), lambda i, bi: (i, 0)),
                  pl.BlockSpec((tt, m), lambda i, bi: (i, 0)),
                  uspec, uspec, pspec(x_col), pspec(zp_col),
                  pl.BlockSpec((1, HY_W), lambda i, bi: (0, 0))],
        out_specs=out_specs,
        out_shape=out_shape,
        scratch_shapes=[pltpu.VMEM((HY_W // HEAD_W, 2 * tt, HEAD_W), F32)] if interleave else [],
        compiler_params=_cparams("parallel", "parallel"),
        name="hy_dft_inv",
    )(cmat, smat, ur, ui, x_arr, zp_arr, skip.reshape(1, HY_W))


def dft_tables(l):
    r = _pick(l, (64, 32, 16, 8))
    period = 8 * l
    odd = 2 * jnp.arange(l, dtype=jnp.int32) + 1
    s1 = jnp.arange(l // r, dtype=jnp.int32) * (2 * r)
    s0 = 2 * jnp.arange(r, dtype=jnp.int32) + 1
    ang = lambda ph: ph.astype(F32) * (2.0 * math.pi / period)
    a = ang((odd[:, None] * s1[None, :]) % period)
    bb = ang((odd[:, None] * s0[None, :]) % period)
    ca, sa, cb, sb = jnp.cos(a), jnp.sin(a), jnp.cos(bb), jnp.sin(bb)
    cmat = (ca[:, :, None] * cb[:, None, :] - sa[:, :, None] * sb[:, None, :]).reshape(l, l).astype(BF16)
    smat = (sa[:, :, None] * cb[:, None, :] + ca[:, :, None] * sb[:, None, :]).reshape(l, l).astype(BF16)
    phi = ang(odd)[:, None]
    return cmat, smat, jnp.cos(phi), jnp.sin(phi)


def hyena_filter_taps(length, lp, parity):
    pos = 2.0 * jnp.arange(length // 2, dtype=F32) + parity
    t = pos / (length - 1)
    bands = (HY_EMB - 1) // 2
    fr = jnp.linspace(1e-4, bands - 1, bands, dtype=F32)
    ang = (2.0 * math.pi / length) * pos[:, None] * fr[None, :]
    z = jnp.concatenate([t[:, None], jnp.cos(ang), -jnp.sin(ang)], axis=-1)
    mm = functools.partial(jnp.matmul, precision=HIGHEST)
    hdn = jnp.sin(lp['hy_f_freq'][0] * (mm(z, lp['hy_f_w1']) + lp['hy_f_b1']))
    hdn = jnp.sin(lp['hy_f_freq'][1] * (mm(hdn, lp['hy_f_w2']) + lp['hy_f_b2']))
    return mm(hdn, lp['hy_f_w3']) * jnp.exp(-t[:, None] * jnp.abs(lp['hy_decay']))


def hyena_mixer(p, lp, tables):
    b, l, _ = p.shape
    cmat, smat, cpsi, spsi = tables
    uc, ucb = hy_pre(p, lp['hy_conv_w'], lp['hy_conv_b'])
    w2 = 2 * HY_W
    m = l // 2
    taps = [hyena_filter_taps(l, lp, par).reshape(m, 2, 2, HY_W) for par in range(2)]
    gf0, gf1 = (tp[:, :, 0, :].reshape(m, w2) for tp in taps)
    hb_even, gb1 = (tp[:, :, 1, :].reshape(m, w2) for tp in taps)
    gb0 = jnp.concatenate([hb_even[1:], jnp.zeros((1, w2), F32)], axis=0)
    gr_s = matmul_bf16(cmat, jnp.concatenate([gf0 + gb0, gf1 + gb1], axis=1).astype(BF16))
    gi_s = -matmul_bf16(smat, jnp.concatenate([gf0 - gb0, gf1 - gb1], axis=1).astype(BF16))
    gr = cpsi * gr_s - spsi * gi_s
    gi = spsi * gr_s + cpsi * gi_s
    g0r, g1r, g0i, g1i = gr[:, :w2], gr[:, w2:], gi[:, :w2], gi[:, w2:]
    tr = cpsi * g1r + spsi * g1i
    ti = cpsi * g1i - spsi * g1r
    scale = 1.0 / l
    h4 = ((g0r + tr) * scale, (g0i + ti) * scale,
          (g0r - tr) * scale, (g0i - ti) * scale)
    ur, ui = dft_fwd(cmat, smat, ucb, 0, cpsi, spsi, h4, 0)
    z1, z1b = dft_inv(cmat, smat, ur, ui, uc, 1, uc, 0, lp['hy_skip'][0], interleave=False)
    ur, ui = dft_fwd(cmat, smat, z1b, 0, cpsi, spsi, h4, 1)
    (z2,) = dft_inv(cmat, smat, ur, ui, uc, 2, z1, 0, lp['hy_skip'][1], interleave=True)
    return z2


def _head_rms(y):
    parts = []
    for h in range(HEADS):
        yh = y[:, h * HEAD_W:(h + 1) * HEAD_W]
        parts.append(yh * lax.rsqrt(jnp.mean(yh * yh, axis=-1, keepdims=True) + EPS))
    return jnp.concatenate(parts, axis=-1)


def _outproj_kernel(hy_ref, gla_ref, glag_ref, ml_ref, mlo_ref, ret_ref, retg_ref,
                    gn_ref, w_ref, res_ref, gate_ref, o_ref):
    gn = gn_ref[...]
    y_gla = _head_rms(gla_ref[0]) * gn[0:1, :] * _silu(glag_ref[0].astype(F32))
    y_ml = _head_rms(_sigmoid(mlo_ref[0].astype(F32)) * ml_ref[0]) * gn[1:2, :]
    y_ret = _head_rms(ret_ref[0]) * gn[2:3, :] * _silu(retg_ref[0].astype(F32))
    acc = _dot(hy_ref[0].astype(BF16), w_ref[0:GROUP_W, :])
    acc += _dot(y_gla.astype(BF16), w_ref[GROUP_W:2 * GROUP_W, :])
    acc += _dot(y_ml.astype(BF16), w_ref[2 * GROUP_W:3 * GROUP_W, :])
    acc += _dot(y_ret.astype(BF16), w_ref[3 * GROUP_W:4 * GROUP_W, :])
    o_ref[0] = res_ref[0] + gate_ref[...] * acc


def outproj(hy, o_gla, o_ml, o_ret, p, gains, w_out, res, mod4, row_of_batch, i_gate):
    b, l, d = res.shape
    tm = _pick(l, (512, 256, 128, 64))
    gw = GROUP_W
    ospec = pl.BlockSpec((1, tm, gw), lambda bi, i: (bi, i, 0))
    pspec = lambda off: pl.BlockSpec((1, tm, gw), lambda bi, i: (bi, i, off // gw))
    return pl.pallas_call(
        _outproj_kernel,
        grid=(b, l // tm),
        in_specs=[ospec, ospec, pspec(OFF_GLA_G), ospec, pspec(OFF_ML_O), ospec, pspec(OFF_RET_G),
                  pl.BlockSpec((8, gw), lambda bi, i: (0, 0)),
                  pl.BlockSpec((4 * gw, d), lambda bi, i: (0, 0)),
                  pl.BlockSpec((1, tm, d), lambda bi, i: (bi, i, 0)),
                  pl.BlockSpec((None, None, 1, d), lambda bi, i: (row_of_batch(bi), i_gate, 0, 0))],
        out_specs=pl.BlockSpec((1, tm, d), lambda bi, i: (bi, i, 0)),
        out_shape=jax.ShapeDtypeStruct((b, l, d), F32),
        compiler_params=_cparams("parallel", "parallel"),
        name="outproj",
    )(hy, o_gla, p, o_ml, p, o_ret, p, gains, w_out, res, mod4)


def _moe_pre_kernel(x_ref, g_ref, sc_ref, sh_ref, wr_ref, br_ref, t_ref, lg_ref):
    t = _norm_mod(x_ref[0], g_ref[...], sc_ref[...], sh_ref[...])
    t_ref[0] = _pack_pairs(t)
    lg_ref[0] = _dot_hi(t, wr_ref[...]) + br_ref[...]


def moe_pre(x, g, mod4, row_of_batch, i_sc, i_sh, w_router, b_router):
    b, l, d = x.shape
    tm = _pick(l, (512, 256, 128, 64))
    nr = w_router.shape[1]
    return pl.pallas_call(
        _moe_pre_kernel,
        grid=(b, l // tm),
        in_specs=[pl.BlockSpec((1, tm, d), lambda bi, i: (bi, i, 0)),
                  pl.BlockSpec((1, d), lambda bi, i: (0, 0)),
                  pl.BlockSpec((None, None, 1, d), lambda bi, i: (row_of_batch(bi), i_sc, 0, 0)),
                  pl.BlockSpec((None, None, 1, d), lambda bi, i: (row_of_batch(bi), i_sh, 0, 0)),
                  pl.BlockSpec((d, nr), lambda bi, i: (0, 0)),
                  pl.BlockSpec((1, nr), lambda bi, i: (0, 0))],
        out_specs=[pl.BlockSpec((1, tm, d // 2), lambda bi, i: (bi, i, 0)),
                   pl.BlockSpec((1, tm, nr), lambda bi, i: (bi, i, 0))],
        out_shape=[jax.ShapeDtypeStruct((b, l, d // 2), jnp.uint32), jax.ShapeDtypeStruct((b, l, nr), F32)],
        compiler_params=_cparams("parallel", "parallel"),
        name="moe_pre",
    )(x, g.reshape(1, d), mod4, mod4, w_router, b_router)


ROUTE_E1, ROUTE_E2, ROUTE_R1, ROUTE_R2, ROUTE_W1, ROUTE_W2 = range(6)


def _moe_route_kernel(lg_ref, route_ref, cnt_ref, run_ref, *, tr):
    i = pl.program_id(0)

    @pl.when(i == 0)
    def _():
        run_ref[...] = jnp.zeros_like(run_ref)

    lg = lg_ref[...]
    lane = lax.broadcasted_iota(jnp.int32, lg.shape, 1).astype(F32)
    no_lane = float(HEAD_W)
    neg = -jnp.inf
    first_max = lambda vals, vmax: jnp.min(jnp.where(vals == vmax, lane, no_lane), axis=-1, keepdims=True)
    gl = jnp.where(lane < MOE_GROUPS, lg, neg)
    gmax = jnp.max(gl, axis=-1, keepdims=True)
    pg = 1.0 / jnp.sum(jnp.exp(gl - gmax), axis=-1, keepdims=True)
    lo = MOE_GROUPS + first_max(gl, gmax) * MOE_EPG
    sel = jnp.where(jnp.logical_and(lane >= lo, lane < lo + MOE_EPG), lg, neg)
    v1 = jnp.max(sel, axis=-1, keepdims=True)
    i1 = first_max(sel, v1)
    sel2 = jnp.where(lane == i1, neg, sel)
    v2 = jnp.max(sel2, axis=-1, keepdims=True)
    i2 = first_max(sel2, v2)
    e21 = jnp.exp(v2 - v1)
    w1 = pg / (1.0 + e21)
    w2 = w1 * e21
    e1 = i1 - MOE_GROUPS
    e2 = i2 - MOE_GROUPS
    oh1 = (lane == e1).astype(F32)
    oh2 = (lane == e2).astype(F32)
    both = oh1 + oh2
    row = lax.broadcasted_iota(jnp.int32, (tr, tr), 0)
    col = lax.broadcasted_iota(jnp.int32, (tr, tr), 1)
    earlier = (col < row).astype(BF16)
    before = _dot(earlier, both.astype(BF16)) + run_ref[...]
    r1 = jnp.sum(before * oh1, axis=-1, keepdims=True)
    r2 = jnp.sum(before * oh2, axis=-1, keepdims=True)
    run_ref[...] += jnp.sum(both, axis=0, keepdims=True)
    rec = jnp.zeros_like(lg)
    for k, val in ((ROUTE_E1, e1), (ROUTE_E2, e2), (ROUTE_R1, r1), (ROUTE_R2, r2), (ROUTE_W1, w1), (ROUTE_W2, w2)):
        rec = jnp.where(lane == k, val, rec)
    route_ref[...] = rec

    @pl.when(i == pl.num_programs(0) - 1)
    def _():
        cnt_ref[...] = run_ref[...]


def moe_route(logits):
    t, nl = logits.shape
    tr = _pick(t, (512, 256, 128, 64))
    return pl.pallas_call(
        functools.partial(_moe_route_kernel, tr=tr),
        grid=(t // tr,),
        in_specs=[pl.BlockSpec((tr, nl), lambda i: (i, 0))],
        out_specs=[pl.BlockSpec((tr, nl), lambda i: (i, 0)), pl.BlockSpec((1, nl), lambda i: (0, 0))],
        out_shape=[jax.ShapeDtypeStruct((t, nl), F32), jax.ShapeDtypeStruct((1, nl), F32)],
        scratch_shapes=[pltpu.VMEM((1, nl), F32)],
        compiler_params=_cparams("arbitrary"),
        name="moe_route",
    )(logits)


def _moe_pos_kernel(route_ref, pstart_ref, pos_ref):
    rec = route_ref[...]
    lane = lax.broadcasted_iota(jnp.int32, rec.shape, 1).astype(F32)
    start = lambda e: jnp.sum(jnp.where(lane == e, pstart_ref[...], 0.0), axis=-1, keepdims=True)
    p1 = start(rec[:, ROUTE_E1:ROUTE_E1 + 1]) + rec[:, ROUTE_R1:ROUTE_R1 + 1]
    p2 = start(rec[:, ROUTE_E2:ROUTE_E2 + 1]) + rec[:, ROUTE_R2:ROUTE_R2 + 1]
    pos_ref[...] = jnp.where(lane == 0, p1, jnp.where(lane == 1, p2, 0.0))


def moe_pos(route, pstart_row):
    t, nl = route.shape
    tr = _pick(t, (512, 256, 128, 64))
    return pl.pallas_call(
        _moe_pos_kernel,
        grid=(t // tr,),
        in_specs=[pl.BlockSpec((tr, nl), lambda i: (i, 0)), pl.BlockSpec((1, nl), lambda i: (0, 0))],
        out_specs=pl.BlockSpec((tr, nl), lambda i: (i, 0)),
        out_shape=jax.ShapeDtypeStruct((t, nl), F32),
        compiler_params=_cparams("parallel"),
        name="moe_pos",
    )(route, pstart_row)


ISSUE_UNROLL = 8


def _row_copy_wait(src_rows, dst_rows, sem):
    pltpu.make_async_copy(src_rows, dst_rows, sem).wait()


def _moe_dispatch_kernel(pend_ref, padded_ref, pos_ref, tok_ref, xs_ref, zero_ref, sem, *, tr, tm):
    @pl.when(pl.program_id(0) == 0)
    def _():
        zero_ref[...] = jnp.zeros_like(zero_ref)

        def clear(e, carry):
            @pl.when(padded_ref[e] > 0)
            def _():
                dst = xs_ref.at[pl.ds(pl.multiple_of(pend_ref[e] - tm, tm), tm)]
                cp = pltpu.make_async_copy(zero_ref, dst, sem)
                cp.start()
                cp.wait()
            return carry

        lax.fori_loop(0, N_EXPERTS, clear, 0)

    def issue(r, carry):
        for slot in range(2):
            dst = pos_ref[0, 0, 2 * r + slot]
            pltpu.make_async_copy(tok_ref.at[pl.ds(r, 1)], xs_ref.at[pl.ds(dst, 1)], sem).start()
        return carry

    lax.fori_loop(0, tr, issue, 0, unroll=ISSUE_UNROLL)
    for _ in range(2):
        _row_copy_wait(tok_ref, xs_ref.at[pl.ds(0, tr)], sem)


def moe_dispatch(tok, pos, pend, padded, pmax, tm):
    t, d = tok.shape
    tr = pos.shape[2] // 2
    grid_spec = pltpu.PrefetchScalarGridSpec(
        num_scalar_prefetch=2,
        grid=(t // tr,),
        in_specs=[pl.BlockSpec((1, 1, 2 * tr), lambda i, pe, pa: (i, 0, 0), memory_space=pltpu.SMEM),
                  pl.BlockSpec((tr, d), lambda i, pe, pa: (i, 0))],
        out_specs=pl.BlockSpec(memory_space=pl.ANY),
        scratch_shapes=[pltpu.VMEM((tm, d), tok.dtype), pltpu.SemaphoreType.DMA],
    )
    return pl.pallas_call(
        functools.partial(_moe_dispatch_kernel, tr=tr, tm=tm),
        grid_spec=grid_spec,
        out_shape=jax.ShapeDtypeStruct((pmax, d), tok.dtype),
        compiler_params=_cparams("arbitrary"),
        name="moe_dispatch",
    )(pend, padded, pos, tok)


def _moe_expert_kernel(te_ref, nu_ref, x_ref, w1_ref, w3_ref, w2_ref, o_ref, w1b_ref, w3b_ref, w2b_ref):
    t = pl.program_id(0)

    @pl.when(t < nu_ref[0])
    def _():
        @pl.when(jnp.logical_or(t == 0, te_ref[t] != te_ref[jnp.maximum(t - 1, 0)]))
        def _():
            w1b_ref[...] = w1_ref[0].astype(BF16)
            w3b_ref[...] = w3_ref[0].astype(BF16)
            w2b_ref[...] = w2_ref[0].astype(BF16)

        x = _unpack_pairs(x_ref[...]).astype(BF16)
        act = _silu(_dot(x, w1b_ref[...])) * _dot(x, w3b_ref[...])
        o_ref[...] = _pack_pairs(_dot(act.astype(BF16), w2b_ref[...]))


def moe_experts(x_sorted, w1, w3, w2, tile_expert, n_used, tm):
    pmax, dp = x_sorted.shape
    d = 2 * dp
    ff = w1.shape[2]
    used = lambda t, nu: jnp.minimum(t, nu[0] - 1)
    grid_spec = pltpu.PrefetchScalarGridSpec(
        num_scalar_prefetch=2,
        grid=(pmax // tm,),
        in_specs=[pl.BlockSpec((tm, dp), lambda t, te, nu: (used(t, nu), 0)),
                  pl.BlockSpec((1, d, ff), lambda t, te, nu: (te[t], 0, 0)),
                  pl.BlockSpec((1, d, ff), lambda t, te, nu: (te[t], 0, 0)),
                  pl.BlockSpec((1, ff, d), lambda t, te, nu: (te[t], 0, 0))],
        out_specs=pl.BlockSpec((tm, dp), lambda t, te, nu: (used(t, nu), 0)),
        scratch_shapes=[pltpu.VMEM((d, ff), BF16), pltpu.VMEM((d, ff), BF16), pltpu.VMEM((ff, d), BF16)],
    )
    return pl.pallas_call(
        _moe_expert_kernel,
        grid_spec=grid_spec,
        out_shape=jax.ShapeDtypeStruct((pmax, dp), jnp.uint32),
        compiler_params=_cparams("arbitrary"),
        name="moe_experts",
    )(tile_expert, n_used, x_sorted, w1, w3, w2)


def _moe_combine_kernel(p1_ref, p2_ref, p1n_ref, p2n_ref, ys_ref, res_ref, route_ref, gate_ref, fin_ref, o_ref,
                        y1_ref, y2_ref, sem, *, tc, final_norm):
    i = pl.program_id(0)
    slot = i % 2

    def issue(pa_ref, pb_ref, s):
        def body(r, carry):
            pltpu.make_async_copy(ys_ref.at[pl.ds(pa_ref[0, 0, r], 1)], y1_ref.at[s, pl.ds(r, 1)], sem.at[s]).start()
            pltpu.make_async_copy(ys_ref.at[pl.ds(pb_ref[0, 0, r], 1)], y2_ref.at[s, pl.ds(r, 1)], sem.at[s]).start()
            return carry
        lax.fori_loop(0, tc, body, 0, unroll=ISSUE_UNROLL)

    @pl.when(i == 0)
    def _():
        issue(p1_ref, p2_ref, 0)

    @pl.when(i + 1 < pl.num_programs(0))
    def _():
        issue(p1n_ref, p2n_ref, 1 - slot)

    _row_copy_wait(ys_ref.at[pl.ds(0, tc)], y1_ref.at[slot], sem.at[slot])
    _row_copy_wait(ys_ref.at[pl.ds(0, tc)], y2_ref.at[slot], sem.at[slot])
    rec = route_ref[...]
    w1 = rec[:, ROUTE_W1:ROUTE_W1 + 1]
    w2 = rec[:, ROUTE_W2:ROUTE_W2 + 1]
    y = w1 * _unpack_pairs(y1_ref[slot]) + w2 * _unpack_pairs(y2_ref[slot])
    out = res_ref[...] + gate_ref[...] * y
    if final_norm:
        out = out * lax.rsqrt(jnp.mean(out * out, axis=-1, keepdims=True) + EPS) * fin_ref[...]
    o_ref[...] = out


def moe_combine(res, y_sorted, pos1, pos2, route, mod4, row_of_tile, i_gate, final_g=None):
    t, d = res.shape
    fin = jnp.ones((1, d), F32) if final_g is None else final_g.reshape(1, d)
    n, _, tc = pos1.shape
    cur = lambda i: (i, 0, 0)
    nxt = lambda i: (jnp.minimum(i + 1, n - 1), 0, 0)
    smem = lambda im: pl.BlockSpec((1, 1, tc), im, memory_space=pltpu.SMEM)
    return pl.pallas_call(
        functools.partial(_moe_combine_kernel, tc=tc, final_norm=final_g is not None),
        grid=(n,),
        in_specs=[smem(cur), smem(cur), smem(nxt), smem(nxt),
                  pl.BlockSpec(memory_space=pl.ANY),
                  pl.BlockSpec((tc, d), lambda i: (i, 0)),
                  pl.BlockSpec((tc, route.shape[1]), lambda i: (i, 0)),
                  pl.BlockSpec((None, None, 1, d), lambda i: (row_of_tile(i), i_gate, 0, 0)),
                  pl.BlockSpec((1, d), lambda i: (0, 0))],
        out_specs=pl.BlockSpec((tc, d), lambda i: (i, 0)),
        out_shape=jax.ShapeDtypeStruct((t, d), F32),
        scratch_shapes=[pltpu.VMEM((2, tc, d // 2), y_sorted.dtype), pltpu.VMEM((2, tc, d // 2), y_sorted.dtype),
                        pltpu.SemaphoreType.DMA((2,))],
        compiler_params=_cparams("arbitrary"),
        name="moe_combine",
    )(pos1, pos2, pos1, pos2, y_sorted, res, route, mod4, fin)


def hier_moe(x, lp, mod4, row_of_batch, final_g=None):
    b, l, d = x.shape
    t = b * l
    tm = 512 if t >= 4096 else 64
    tok, logits = moe_pre(x, lp['norm2_g'], mod4, row_of_batch, 4, 3, lp['w_router'], lp['b_router'])
    tok = tok.reshape(t, d // 2)
    route, cnt = moe_route(logits.reshape(t, -1))
    counts = cnt[0, :N_EXPERTS].astype(jnp.int32)
    padded = ((counts + tm - 1) // tm) * tm
    pend = jnp.cumsum(padded)
    pstart = pend - padded
    pmax = 2 * t + N_EXPERTS * tm
    tile_row = jnp.arange(pmax // tm, dtype=jnp.int32) * tm
    tile_expert = jnp.minimum(jnp.sum(tile_row[:, None] >= pend[None, :], axis=1), N_EXPERTS - 1).astype(jnp.int32)
    n_used = (pend[-1] // tm).astype(jnp.int32).reshape(1)
    pstart_row = jnp.zeros((1, route.shape[1]), F32).at[0, :N_EXPERTS].set(pstart.astype(F32))
    pos12 = moe_pos(route, pstart_row)[:, :2].astype(jnp.int32)
    pos1, pos2 = pos12[:, 0], pos12[:, 1]
    tr = _pick(t, (512, 256, 128, 64))
    tc = _pick(l, (256, 128, 64))
    pos = pos12.reshape(t // tr, 1, 2 * tr)
    x_sorted = moe_dispatch(tok, pos, pend, padded, pmax, tm)
    y_sorted = moe_experts(x_sorted, lp['moe_w1e'], lp['moe_w3e'], lp['moe_w2e'],
                           tile_expert + lp['expert_base'], n_used, tm)
    tiles_per_batch = l // tc
    out = moe_combine(x.reshape(t, d), y_sorted, pos1.reshape(t // tc, 1, tc), pos2.reshape(t // tc, 1, tc),
                      route, mod4, lambda i: row_of_batch(i // tiles_per_batch), 5, final_g=final_g)
    return out.reshape(b, l, d)


def _pad_heads(w, dk):
    d = w.shape[0]
    return jnp.pad(w.reshape(d, HEADS, dk), ((0, 0), (0, 0), (0, HEAD_W - dk))).reshape(d, HEADS * HEAD_W)


def prep_w_in(w_in):
    o = 0
    take = lambda n: (w_in[:, o:o + n], o + n)
    hy, o = take(3 * HY_W)
    gq, o = take(HEADS * GLA_DK)
    gk, o = take(HEADS * GLA_DK)
    gv, o = take(GROUP_W)
    gg, o = take(GROUP_W)
    gr, o = take(2 * GLA_RANK)
    ml, o = take(4 * GROUP_W)
    mg, o = take(16)
    ret, o = take(4 * GROUP_W)
    gk_p = _pad_heads(gk, GLA_DK)
    gk_p = gk_p.at[:, SMALL_R:SMALL_R + 2 * GLA_RANK].set(gr).at[:, SMALL_MLG:SMALL_MLG + 16].set(mg)
    return jnp.concatenate([hy, _pad_heads(gq, GLA_DK), gk_p, gv, gg, ml, ret], axis=1).astype(BF16)


def prep_layer(l, prm):
    lp = {k: v[l] for k, v in prm.items()}
    lp['w_in_p'] = prep_w_in(lp['w_in'])
    lp['w_out_b'] = lp['w_out'].astype(BF16)
    gates = []
    for d in range(2):
        w = jnp.zeros((HEAD_W, HEADS * HEAD_W), F32)
        w = w.at[SMALL_R + d * GLA_RANK:SMALL_R + (d + 1) * GLA_RANK].set(_pad_heads(lp['gla_wa2'][d], GLA_DK))
        w_hi = w.astype(BF16)
        w_lo = (w - w_hi.astype(F32)).astype(BF16)
        gates.append((w_hi, w_lo, _pad_heads(lp['gla_ba'][d][None, :], GLA_DK)))
    lp['gla_gate'] = gates
    bias = jnp.zeros((1, HEAD_W), F32).at[0, SMALL_MLG:SMALL_MLG + 16].set(lp['ml_gate_b'].reshape(-1))
    lp['ml_bias'] = bias
    lp['ret_dec'] = [jnp.repeat(lp['ret_decay'][d], HEAD_W)[None, :] for d in range(2)]
    gains = jnp.zeros((8, GROUP_W), F32)
    lp['gains'] = gains.at[0].set(lp['gla_norm_g']).at[1].set(lp['ml_norm_g']).at[2].set(lp['ret_norm_g'])
    d_model = lp['moe_wg'].shape[0]
    wr = jnp.zeros((d_model, HEAD_W), F32)
    lp['w_router'] = wr.at[:, :MOE_GROUPS].set(lp['moe_wg']).at[:, MOE_GROUPS:MOE_GROUPS + N_EXPERTS].set(lp['moe_we'])
    br = jnp.zeros((1, HEAD_W), F32)
    lp['b_router'] = br.at[0, :MOE_GROUPS].set(lp['moe_bg']).at[0, MOE_GROUPS:MOE_GROUPS + N_EXPERTS].set(lp['moe_be'])
    lp['moe_w1e'] = prm['moe_w1'].reshape(-1, d_model, MOE_FF)
    lp['moe_w3e'] = prm['moe_w3'].reshape(-1, d_model, MOE_FF)
    lp['moe_w2e'] = prm['moe_w2'].reshape(-1, MOE_FF, d_model)
    lp['expert_base'] = l * N_EXPERTS
    return lp


def rotary_tables(length):
    rows = length // GRID_W
    row = jnp.repeat(jnp.arange(rows, dtype=F32), GRID_W)
    col = jnp.tile(jnp.arange(GRID_W, dtype=F32), rows)
    nf = HEAD_W // 4
    inv = ROPE_BASE ** (-jnp.arange(nf, dtype=F32) / nf)
    ang = jnp.concatenate([row[:, None] * inv, col[:, None] * inv], axis=-1)
    cos, sin = jnp.cos(ang), jnp.sin(ang)
    return jnp.concatenate([cos, cos], axis=-1), jnp.concatenate([-sin, sin], axis=-1)


def bidir_scans(pg_ctx, pg_lat, lp, rot):
    (p_ctx, g_ctx), (p_lat, g_lat) = pg_ctx, pg_lat
    b = p_lat.shape[0]
    zeros_s = jnp.zeros((b, HEADS, HEAD_W, HEAD_W), F32)
    zeros_v = jnp.zeros((b, HEADS, 1, HEAD_W), F32)
    out = {}
    for name, mode, offs, scale in (("gla", "gla", (OFF_GLA_Q, OFF_GLA_K, OFF_GLA_V), GLA_DK ** -0.5),
                                    ("ret", "ret", (OFF_RET_Q, OFF_RET_K, OFF_RET_V), HEAD_W ** -0.5)):
        o_c, o_l = None, None
        for d in range(2):
            gate = lp['gla_gate'][d] if mode == "gla" else lp['ret_dec'][d]
            o_c, st = gla_scan(p_ctx, g_ctx, *offs, zeros_s, reverse=d == 1, mode=mode, gate=gate,
                               o_prev=o_c, q_scale=scale)
            o_l, _ = gla_scan(p_lat, g_lat, *offs, st, reverse=d == 1, mode=mode, gate=gate,
                              rot=rot if mode == "ret" else None, o_prev=o_l, q_scale=scale)
        out[name] = (o_c, o_l)
    o_c, o_l = None, None
    for d in range(2):
        o_c, st = mlstm_scan(p_ctx, g_ctx, (zeros_s, zeros_v, zeros_v), lp['ml_bias'], reverse=d == 1, o_prev=o_c)
        o_l, _ = mlstm_scan(p_lat, g_lat, st, lp['ml_bias'], reverse=d == 1, o_prev=o_l)
    out["ml"] = (o_c, o_l)
    return out


def kernel(x, c, ctx, c_ctx, ada_w, ada_b, norm1_g, norm2_g, w_in, hy_conv_w, hy_conv_b, hy_f_w1, hy_f_b1, hy_f_w2, hy_f_b2, hy_f_freq, hy_f_w3, hy_decay, hy_skip, gla_wa2, gla_ba, gla_norm_g, ml_gate_b, ml_norm_g, ret_decay, ret_norm_g, w_out, moe_wg, moe_bg, moe_we, moe_be, moe_w1, moe_w3, moe_w2, final_g):
    prm = dict(norm1_g=norm1_g, norm2_g=norm2_g, w_in=w_in, hy_conv_w=hy_conv_w, hy_conv_b=hy_conv_b,
               hy_f_w1=hy_f_w1, hy_f_b1=hy_f_b1, hy_f_w2=hy_f_w2, hy_f_b2=hy_f_b2, hy_f_freq=hy_f_freq,
               hy_f_w3=hy_f_w3, hy_decay=hy_decay, hy_skip=hy_skip, gla_wa2=gla_wa2, gla_ba=gla_ba,
               gla_norm_g=gla_norm_g, ml_gate_b=ml_gate_b, ml_norm_g=ml_norm_g, ret_decay=ret_decay,
               ret_norm_g=ret_norm_g, w_out=w_out, moe_wg=moe_wg, moe_bg=moe_bg, moe_we=moe_we,
               moe_be=moe_be, moe_w1=moe_w1, moe_w3=moe_w3, moe_w2=moe_w2)
    depth = ada_w.shape[0]
    b, seq, d = x.shape
    lc = ctx.shape[1]
    lat = x.astype(F32)
    cx = ctx.astype(F32)
    cvec = jnp.zeros((8, d), F32).at[:b].set(c.astype(F32)).at[b].set(c_ctx.astype(F32))
    mod_all = adaln(cvec, ada_w, ada_b)
    lat_row = lambda bi: bi
    ctx_row = lambda bi: b
    rot = rotary_tables(seq)
    tab_lat = dft_tables(seq // 2)
    tab_ctx = dft_tables(lc // 2)
    for l in range(depth):
        with_ctx = l < depth - 1
        lp = prep_layer(l, prm)
        mod4 = mod_all[l].reshape(8, 6, 1, d)
        p_lat, g_lat = inproj(lat, lp['norm1_g'], mod4, lat_row, 1, 0, lp['w_in_p'])
        p_ctx, g_ctx = inproj(cx, lp['norm1_g'], mod4, ctx_row, 1, 0, lp['w_in_p'])
        scans = bidir_scans((p_ctx, g_ctx), (p_lat, g_lat), lp, rot)
        hy_lat = hyena_mixer(p_lat, lp, tab_lat)
        lat = outproj(hy_lat, scans["gla"][1], scans["ml"][1], scans["ret"][1], p_lat, lp['gains'],
                      lp['w_out_b'], lat, mod4, lat_row, 2)
        lat = hier_moe(lat, lp, mod4, lat_row, final_g=None if with_ctx else final_g)
        if with_ctx:
            hy_ctx = hyena_mixer(p_ctx, lp, tab_ctx)
            cx = outproj(hy_ctx, scans["gla"][0], scans["ml"][0], scans["ret"][0], p_ctx, lp['gains'],
                         lp['w_out_b'], cx, mod4, ctx_row, 2)
            cx = hier_moe(cx, lp, mod4, ctx_row)
    return lat.astype(x.dtype)
```

```python
import functools
import math

import jax
import jax.numpy as jnp
from jax import lax
from jax.experimental import pallas as pl
from jax.experimental.pallas import tpu as pltpu

F32 = jnp.float32
BF16 = jnp.bfloat16
HIGHEST = lax.Precision.HIGHEST

EPS = 1e-6
CHUNK = 64
SCAN_BLOCK = 256
NORM_ROWS = 128
HALO = 16
GRID_W = 64
ROPE_BASE = 10000.0
HEADS = 4
HEAD_W = 128
HY_W = 512
HY_EMB = 33
GLA_DK = 64
GLA_RANK = 16
GLA_TAU = 16.0
MOE_GROUPS = 4
MOE_EPG = 8
MOE_FF = 256
N_EXPERTS = MOE_GROUPS * MOE_EPG

GROUP_W = 512
OFF_HY = 0
OFF_GLA_Q = 1536
OFF_GLA_K = OFF_GLA_Q + GROUP_W
OFF_GLA_V = OFF_GLA_K + GROUP_W
OFF_GLA_G = OFF_GLA_V + GROUP_W
OFF_ML_Q = OFF_GLA_G + GROUP_W
OFF_ML_K = OFF_ML_Q + GROUP_W
OFF_ML_V = OFF_ML_K + GROUP_W
OFF_ML_O = OFF_ML_V + GROUP_W
OFF_RET_Q = OFF_ML_O + GROUP_W
OFF_RET_K = OFF_RET_Q + GROUP_W
OFF_RET_V = OFF_RET_K + GROUP_W
OFF_RET_G = OFF_RET_V + GROUP_W
N_PROJ = OFF_RET_G + GROUP_W
SMALL_BASE = GLA_DK
SMALL_R = SMALL_BASE
SMALL_MLG = SMALL_BASE + 2 * GLA_RANK

VMEM_LIMIT_BYTES = 56 * 1024 * 1024


def _cparams(*sem):
    return pltpu.CompilerParams(dimension_semantics=sem, vmem_limit_bytes=VMEM_LIMIT_BYTES)


def _pick(n, candidates):
    for c in candidates:
        if n % c == 0:
            return c
    return n


def _silu(x):
    return x / (1.0 + jnp.exp(-x))


def _sigmoid(x):
    return 1.0 / (1.0 + jnp.exp(-x))


def _log_sigmoid(x):
    return jnp.minimum(x, 0.0) - jnp.log(1.0 + jnp.exp(-jnp.abs(x)))


def _dot(a, b):
    return jnp.dot(a, b, preferred_element_type=F32)


def _dot_hi(a, b):
    return jnp.dot(a, b, preferred_element_type=F32, precision=HIGHEST)


def _split_bf16(x, parts):
    out = []
    for _ in range(parts - 1):
        hi = x.astype(BF16)
        out.append(hi)
        x = x - hi.astype(F32)
    out.append(x.astype(BF16))
    return out


def _dot_exact_lhs(a_bf16, x):
    return sum(_dot(a_bf16, p) for p in _split_bf16(x, 3))


def _pack_pairs(x):
    h = x.shape[1] // 2
    bits = lambda a: lax.bitcast_convert_type(a.astype(BF16).astype(F32), jnp.uint32)
    return bits(x[:, :h]) | (bits(x[:, h:]) >> 16)


def _unpack_pairs(w):
    hi = lax.bitcast_convert_type(w & jnp.uint32(0xFFFF0000), F32)
    lo = lax.bitcast_convert_type(w << 16, F32)
    return jnp.concatenate([hi, lo], axis=-1)


def _dot_nt(a, b):
    return lax.dot_general(a, b, (((1,), (1,)), ((), ())), preferred_element_type=F32)


def _dot_tn(a, b):
    return lax.dot_general(a, b, (((0,), (0,)), ((), ())), preferred_element_type=F32)


def _adaln_kernel(c_ref, w_ref, b_ref, o_ref):
    o_ref[0] = _dot_hi(_silu(c_ref[...]), w_ref[0]) + b_ref[0]


def adaln(cvec, ada_w, ada_b):
    depth, d, n = ada_w.shape
    tn = _pick(n, (1024, 512, 256, 128))
    return pl.pallas_call(
        _adaln_kernel,
        grid=(depth, n // tn),
        in_specs=[pl.BlockSpec((8, d), lambda l, j: (0, 0)),
                  pl.BlockSpec((1, d, tn), lambda l, j: (l, 0, j)),
                  pl.BlockSpec((1, 1, tn), lambda l, j: (l, 0, j))],
        out_specs=pl.BlockSpec((1, 8, tn), lambda l, j: (l, 0, j)),
        out_shape=jax.ShapeDtypeStruct((depth, 8, n), F32),
        compiler_params=_cparams("parallel", "parallel"),
        name="adaln",
    )(cvec, ada_w, ada_b.reshape(depth, 1, n))


def _norm_mod(x, g, sc, sh):
    ms = jnp.mean(x * x, axis=-1, keepdims=True)
    return (x * lax.rsqrt(ms + EPS) * g) * (1.0 + sc) + sh


def _inproj_kernel(x_ref, g_ref, sc_ref, sh_ref, w_ref, o_ref, gate_ref, xn_ref, *, gate_tile, gate_off):
    j = pl.program_id(2)

    @pl.when(j == 0)
    def _():
        tm = xn_ref.shape[0]
        rc = min(tm, NORM_ROWS)
        for r in range(0, tm, rc):
            xn = _norm_mod(x_ref[0, r:r + rc, :], g_ref[...], sc_ref[...], sh_ref[...]).astype(BF16)
            xn_ref[r:r + rc, :] = xn
            o_ref[0, r:r + rc, :] = _dot(xn, w_ref[...]).astype(BF16)

    @pl.when(j != 0)
    def _():
        acc = _dot(xn_ref[...], w_ref[...])
        o_ref[0] = acc.astype(BF16)

        @pl.when(j == gate_tile)
        def _():
            gate_ref[0] = acc[:, gate_off:gate_off + HEAD_W]


def inproj(x, g, mod4, row_of_batch, i_sc, i_sh, w):
    b, l, d = x.shape
    n = w.shape[1]
    tm = _pick(l, (1024, 512, 256, 128, 64))
    tn = _pick(n, (1536, 768, 512, 256, 128))
    assert OFF_GLA_K // tn > 0 and OFF_GLA_K % tn + HEAD_W <= tn
    kern = functools.partial(_inproj_kernel, gate_tile=OFF_GLA_K // tn, gate_off=OFF_GLA_K % tn)
    return pl.pallas_call(
        kern,
        grid=(b, l // tm, n // tn),
        in_specs=[pl.BlockSpec((1, tm, d), lambda bi, i, j: (bi, i, 0)),
                  pl.BlockSpec((1, d), lambda bi, i, j: (0, 0)),
                  pl.BlockSpec((None, None, 1, d), lambda bi, i, j: (row_of_batch(bi), i_sc, 0, 0)),
                  pl.BlockSpec((None, None, 1, d), lambda bi, i, j: (row_of_batch(bi), i_sh, 0, 0)),
                  pl.BlockSpec((d, tn), lambda bi, i, j: (0, j))],
        out_specs=[pl.BlockSpec((1, tm, tn), lambda bi, i, j: (bi, i, j)),
                   pl.BlockSpec((1, tm, HEAD_W), lambda bi, i, j: (bi, i, 0))],
        out_shape=[jax.ShapeDtypeStruct((b, l, n), BF16), jax.ShapeDtypeStruct((b, l, HEAD_W), F32)],
        scratch_shapes=[pltpu.VMEM((tm, d), BF16)],
        compiler_params=_cparams("parallel", "parallel", "arbitrary"),
        name="inproj",
    )(x, g.reshape(1, d), mod4, mod4, w)


def _causal_mask(n, reverse):
    r = lax.broadcasted_iota(jnp.int32, (n, n), 0)
    c = lax.broadcasted_iota(jnp.int32, (n, n), 1)
    same_chunk = (r // CHUNK) == (c // CHUNK)
    return jnp.logical_and(same_chunk, (c >= r) if reverse else (c <= r))


def _gla_scan_kernel(*refs, reverse, mode, rotate, accumulate, n_chunks, q_scale):
    it = iter(refs)
    q_ref, k_ref, v_ref = next(it), next(it), next(it)
    if mode == "gla":
        gsrc_ref, wah_ref, wal_ref, ba_ref = next(it), next(it), next(it), next(it)
    else:
        dec_ref = next(it)
    if rotate:
        cos_ref, sin_ref = next(it), next(it)
    s0_ref = next(it)
    if accumulate:
        oprev_ref = next(it)
    o_ref, sfin_ref, st_ref = next(it), next(it), next(it)
    dterm_refs = list(it)

    i = pl.program_id(1)

    @pl.when(i == 0)
    def _():
        st_ref[...] = s0_ref[0]

    tb = n_chunks * CHUNK
    mask = _causal_mask(tb, reverse)
    maskb = mask.astype(BF16)
    order = [(n_chunks - 1 - c) if reverse else c for c in range(n_chunks)]
    rows = [slice(c * CHUNK, (c + 1) * CHUNK) for c in range(n_chunks)]

    q = q_ref[0].astype(F32)
    k = k_ref[0].astype(F32)
    vb = v_ref[0]

    def decay_terms(la):
        bc = _dot_exact_lhs(maskb, la)
        tots = [bc[r.start:r.start + 1, :] if reverse else bc[r.stop - 1:r.stop, :] for r in rows]
        tot_rows = jnp.concatenate([jnp.broadcast_to(t, (CHUNK, GROUP_W)) for t in tots], axis=0)
        tot_pad = tots + [jnp.zeros_like(tots[0])] * (-n_chunks % 8)
        return jnp.exp(bc), jnp.exp(-bc), jnp.exp(tot_rows - bc), jnp.exp(jnp.concatenate(tot_pad, axis=0))

    if mode == "gla":
        g_hi, g_lo = _split_bf16(gsrc_ref[0], 2)
        logit = (_dot(g_hi, wah_ref[...]) + _dot(g_lo, wah_ref[...]) + _dot(g_hi, wal_ref[...])) + ba_ref[...]
        e_bc, e_nbc, e_st, e_tot = decay_terms(_log_sigmoid(logit) * (1.0 / GLA_TAU))
    else:
        @pl.when(i == 0)
        def _():
            terms = decay_terms(jnp.broadcast_to(_log_sigmoid(dec_ref[...]), (tb, GROUP_W)))
            for t_ref, t in zip(dterm_refs, terms):
                t_ref[...] = t

        e_bc, e_nbc, e_st, e_tot = (t_ref[...] for t_ref in dterm_refs)
    e_tots = [e_tot[c:c + 1, :] for c in range(n_chunks)]
    if rotate:
        cos2 = cos_ref[...]
        sin2 = sin_ref[...]

    outs = []
    for h in range(HEADS):
        hs = slice(h * HEAD_W, (h + 1) * HEAD_W)
        qh, kh, vh = q[:, hs], k[:, hs], vb[:, hs]
        if rotate:
            qh = qh * cos2 + pltpu.roll(qh, HEAD_W // 2, axis=1) * sin2
            kh = kh * cos2 + pltpu.roll(kh, HEAD_W // 2, axis=1) * sin2
        q_in = (qh * q_scale * e_bc[:, hs]).astype(BF16)
        k_in = (kh * e_nbc[:, hs]).astype(BF16)
        k_st = (kh * e_st[:, hs]).astype(BF16)
        att = jnp.where(mask, _dot_nt(q_in, k_in), 0.0).astype(BF16)
        intra = _dot(att, vh)
        st = st_ref[h]
        inter = [None] * n_chunks
        for cc in order:
            r = rows[cc]
            inter[cc] = _dot_nt(q_in[r], st.astype(BF16))
            st = st * e_tots[cc][:, hs] + _dot_tn(vh[r], k_st[r])
        st_ref[h] = st
        outs.append(intra + jnp.concatenate(inter, axis=0))
    o = jnp.concatenate(outs, axis=-1)
    if accumulate:
        o = o + oprev_ref[0]
    o_ref[0] = o

    @pl.when(i == pl.num_programs(1) - 1)
    def _():
        sfin_ref[0] = st_ref[...]


def gla_scan(p, gblock, off_q, off_k, off_v, s0, *, reverse, mode, gate=None, rot=None, o_prev=None, q_scale):
    b, l, _ = p.shape
    tb = _pick(l, (SCAN_BLOCK, 128, 64))
    nblk = l // tb
    blk = (lambda i: nblk - 1 - i) if reverse else (lambda i: i)
    gspec = lambda off: pl.BlockSpec((1, tb, GROUP_W), lambda bi, i: (bi, blk(i), off // GROUP_W))
    const = lambda shape: pl.BlockSpec(shape, lambda bi, i: (0,) * len(shape))
    in_specs = [gspec(off_q), gspec(off_k), gspec(off_v)]
    args = [p, p, p]
    if mode == "gla":
        wa_hi, wa_lo, ba = gate
        in_specs += [pl.BlockSpec((1, tb, HEAD_W), lambda bi, i: (bi, blk(i), 0)),
                     const((HEAD_W, GROUP_W)), const((HEAD_W, GROUP_W)), const((1, GROUP_W))]
        args += [gblock, wa_hi, wa_lo, ba]
    else:
        in_specs += [const((1, GROUP_W))]
        args += [gate]
    if rot is not None:
        in_specs += [pl.BlockSpec((tb, HEAD_W), lambda bi, i: (blk(i), 0))] * 2
        args += [rot[0], rot[1]]
    sspec = pl.BlockSpec((1, HEADS, HEAD_W, HEAD_W), lambda bi, i: (bi, 0, 0, 0))
    in_specs += [sspec]
    args += [s0]
    o_spec = pl.BlockSpec((1, tb, GROUP_W), lambda bi, i: (bi, blk(i), 0))
    aliases = {}
    if o_prev is not None:
        in_specs += [o_spec]
        aliases = {len(args): 0}
        args += [o_prev]
    kern = functools.partial(_gla_scan_kernel, reverse=reverse, mode=mode, rotate=rot is not None,
                             accumulate=o_prev is not None, n_chunks=tb // CHUNK, q_scale=q_scale)
    return pl.pallas_call(
        kern,
        grid=(b, nblk),
        in_specs=in_specs,
        out_specs=[o_spec, sspec],
        out_shape=[jax.ShapeDtypeStruct((b, l, GROUP_W), F32),
                   jax.ShapeDtypeStruct((b, HEADS, HEAD_W, HEAD_W), F32)],
        scratch_shapes=[pltpu.VMEM((HEADS, HEAD_W, HEAD_W), F32)] + (
            [pltpu.VMEM((tb, GROUP_W), F32)] * 3 + [pltpu.VMEM((8 * pl.cdiv(tb // CHUNK, 8), GROUP_W), F32)]
            if mode == "ret" else []),
        input_output_aliases=aliases,
        compiler_params=_cparams("parallel", "arbitrary"),
        name=f"{mode}_scan_{'bwd' if reverse else 'fwd'}",
    )(*args)


def _mlstm_kernel(*refs, reverse, accumulate, n_chunks, k_scale):
    it = iter(refs)
    q_ref, k_ref, v_ref, gsrc_ref, bias_ref = (next(it) for _ in range(5))
    c0_ref, n0_ref, m0_ref = next(it), next(it), next(it)
    if accumulate:
        oprev_ref = next(it)
    o_ref, cfin_ref, nfin_ref, mfin_ref = next(it), next(it), next(it), next(it)
    ct_ref, n_ref, m_ref = next(it), next(it), next(it)

    i = pl.program_id(1)

    @pl.when(i == 0)
    def _():
        ct_ref[...] = c0_ref[0]
        n_ref[...] = n0_ref[0]
        m_ref[...] = m0_ref[0]

    tb = n_chunks * CHUNK
    mask = _causal_mask(tb, reverse)
    maskb = mask.astype(BF16)
    order = [(n_chunks - 1 - c) if reverse else c for c in range(n_chunks)]
    rows = [slice(c * CHUNK, (c + 1) * CHUNK) for c in range(n_chunks)]
    lane_i = SMALL_MLG + (8 if reverse else 0)
    lane_f = lane_i + HEADS

    q = q_ref[0].astype(F32)
    k = k_ref[0].astype(F32) * k_scale
    vb = v_ref[0]
    gates = gsrc_ref[0] + bias_ref[...]
    bc_all = _dot_exact_lhs(maskb, _log_sigmoid(gates))
    rep = lambda a, c: jnp.broadcast_to(a[:, c:c + 1], (a.shape[0], HEAD_W))
    outs = []
    for h in range(HEADS):
        hs = slice(h * HEAD_W, (h + 1) * HEAD_W)
        qh, kh, vh = q[:, hs], k[:, hs], vb[:, hs]
        bch = rep(bc_all, lane_f + h)
        wh = rep(gates, lane_i + h) - bch
        bends = [bch[r.start:r.start + 1, :] if reverse else bch[r.stop - 1:r.stop, :] for r in rows]
        w_row = jnp.broadcast_to(jnp.transpose(wh)[0:1, :], (tb, tb))
        dmat = jnp.where(mask, bch[:, 0:1] + w_row, -jnp.inf)
        rowmax = jnp.max(dmat, axis=-1, keepdims=True)
        qb = qh.astype(BF16)
        s_raw = _dot_nt(qb, kh.astype(BF16)) * jnp.exp(dmat - rowmax)
        sv = _dot(s_raw.astype(BF16), vh)
        s_sum = jnp.sum(s_raw, axis=-1, keepdims=True)
        ct, nv, m_prev = ct_ref[h], n_ref[h], m_ref[h]
        out = [None] * n_chunks
        for cc in order:
            r = rows[cc]
            bend = bends[cc]
            glh = bend + wh[r]
            gmax = jnp.max(glh, axis=0, keepdims=True)
            gk = jnp.exp(glh - gmax) * kh[r]
            inter_log = bch[r] + m_prev
            m_t = jnp.maximum(inter_log, rowmax[r])
            e_intra = jnp.exp(rowmax[r] - m_t)
            inter = jnp.exp(inter_log - m_t)
            num = inter * _dot_nt(qb[r], ct.astype(BF16)) + e_intra * sv[r]
            den = jnp.abs(inter * jnp.sum(qh[r] * nv, axis=-1, keepdims=True) + e_intra * s_sum[r])
            out[cc] = num / jnp.maximum(den, jnp.exp(-m_t))
            m_new = jnp.maximum(bend + m_prev, gmax)
            dec = jnp.exp(bend + m_prev - m_new)
            e_upd = jnp.exp(gmax - m_new)
            ct = dec * ct + e_upd * _dot_tn(vh[r], gk.astype(BF16))
            nv = dec * nv + e_upd * jnp.sum(gk, axis=0, keepdims=True)
            m_prev = m_new
        ct_ref[h] = ct
        n_ref[h] = nv
        m_ref[h] = m_prev
        outs.append(jnp.concatenate(out, axis=0))
    o = jnp.concatenate(outs, axis=-1)
    if accumulate:
        o = o + oprev_ref[0]
    o_ref[0] = o

    @pl.when(i == pl.num_programs(1) - 1)
    def _():
        cfin_ref[0] = ct_ref[...]
        nfin_ref[0] = n_ref[...]
        mfin_ref[0] = m_ref[...]


def mlstm_scan(p, gblock, state, gate_bias, *, reverse, o_prev=None):
    b, l, _ = p.shape
    tb = _pick(l, (SCAN_BLOCK, 128, 64))
    nblk = l // tb
    blk = (lambda i: nblk - 1 - i) if reverse else (lambda i: i)
    c0, n0, m0 = state
    gspec = lambda off: pl.BlockSpec((1, tb, GROUP_W), lambda bi, i: (bi, blk(i), off // GROUP_W))
    const = lambda shape: pl.BlockSpec(shape, lambda bi, i: (0,) * len(shape))
    cspec = pl.BlockSpec((1, HEADS, HEAD_W, HEAD_W), lambda bi, i: (bi, 0, 0, 0))
    vspec = pl.BlockSpec((1, HEADS, 1, HEAD_W), lambda bi, i: (bi, 0, 0, 0))
    in_specs = [gspec(OFF_ML_Q), gspec(OFF_ML_K), gspec(OFF_ML_V),
                pl.BlockSpec((1, tb, HEAD_W), lambda bi, i: (bi, blk(i), 0)),
                const((1, HEAD_W)), cspec, vspec, vspec]
    args = [p, p, p, gblock, gate_bias, c0, n0, m0]
    o_spec = pl.BlockSpec((1, tb, GROUP_W), lambda bi, i: (bi, blk(i), 0))
    aliases = {}
    if o_prev is not None:
        in_specs += [o_spec]
        aliases = {len(args): 0}
        args += [o_prev]
    kern = functools.partial(_mlstm_kernel, reverse=reverse, accumulate=o_prev is not None,
                             n_chunks=tb // CHUNK, k_scale=HEAD_W ** -0.5)
    outs = pl.pallas_call(
        kern,
        grid=(b, nblk),
        in_specs=in_specs,
        out_specs=[o_spec, cspec, vspec, vspec],
        out_shape=[jax.ShapeDtypeStruct((b, l, GROUP_W), F32),
                   jax.ShapeDtypeStruct((b, HEADS, HEAD_W, HEAD_W), F32),
                   jax.ShapeDtypeStruct((b, HEADS, 1, HEAD_W), F32),
                   jax.ShapeDtypeStruct((b, HEADS, 1, HEAD_W), F32)],
        scratch_shapes=[pltpu.VMEM((HEADS, HEAD_W, HEAD_W), F32), pltpu.VMEM((HEADS, 1, HEAD_W), F32),
                        pltpu.VMEM((HEADS, 1, HEAD_W), F32)],
        input_output_aliases=aliases,
        compiler_params=_cparams("parallel", "arbitrary"),
        name=f"mlstm_scan_{'bwd' if reverse else 'fwd'}",
    )(*args)
    return outs[0], (outs[1], outs[2], outs[3])


def _hy_pre_kernel(u_ref, up_ref, un_ref, w_ref, b_ref, o_ref, ob_ref, y_ref, *, rows):
    i = pl.program_id(2)
    u = u_ref[0].astype(F32)
    prev_row = jnp.where(i == 0, 0.0, up_ref[0, HALO - 1:HALO, :].astype(F32))
    next_row = jnp.where(i == pl.num_programs(2) - 1, 0.0, un_ref[0, 0:1, :].astype(F32))
    ridx = lax.broadcasted_iota(jnp.int32, u.shape, 0)
    u_dn = jnp.where(ridx == 0, prev_row, pltpu.roll(u, 1, axis=0))
    u_up = jnp.where(ridx == rows - 1, next_row, pltpu.roll(u, rows - 1, axis=0))
    y = w_ref[0:1, :] * u_dn + w_ref[1:2, :] * u + w_ref[2:3, :] * u_up + b_ref[...]
    lane_tiles = HY_W // HEAD_W
    for j in range(lane_tiles):
        y_ref[j] = y[:, j * HEAD_W:(j + 1) * HEAD_W]
    for par in range(2):
        plane = jnp.concatenate([y_ref[j, pl.ds(par, rows // 2, stride=2), :] for j in range(lane_tiles)], axis=-1)
        o_ref[0, par] = plane
        ob_ref[0, par] = plane.astype(BF16)


def hy_pre(p, conv_w, conv_b):
    b, l, _ = p.shape
    rows = _pick(l, (512, 256, 128, 64))
    nr = l // rows
    r8 = rows // HALO
    n8 = l // HALO
    wpad = jnp.zeros((8, 3 * HY_W), F32).at[:3].set(conv_w)
    return pl.pallas_call(
        functools.partial(_hy_pre_kernel, rows=rows),
        grid=(b, 3, nr),
        in_specs=[pl.BlockSpec((1, rows, HY_W), lambda bi, j, i: (bi, i, j)),
                  pl.BlockSpec((1, HALO, HY_W), lambda bi, j, i: (bi, jnp.maximum(i * r8 - 1, 0), j)),
                  pl.BlockSpec((1, HALO, HY_W), lambda bi, j, i: (bi, jnp.minimum((i + 1) * r8, n8 - 1), j)),
                  pl.BlockSpec((8, HY_W), lambda bi, j, i: (0, j)),
                  pl.BlockSpec((1, HY_W), lambda bi, j, i: (0, j))],
        out_specs=[pl.BlockSpec((1, 2, rows // 2, HY_W), lambda bi, j, i: (bi, 0, i, j))] * 2,
        out_shape=[jax.ShapeDtypeStruct((b, 2, l // 2, 3 * HY_W), F32),
                   jax.ShapeDtypeStruct((b, 2, l // 2, 3 * HY_W), BF16)],
        scratch_shapes=[pltpu.VMEM((HY_W // HEAD_W, rows, HEAD_W), F32)],
        compiler_params=_cparams("parallel", "parallel", "parallel"),
        name="hy_shortconv",
    )(p, p, p, wpad, conv_b.reshape(1, 3 * HY_W))


def _mm_kernel(a_ref, b_ref, o_ref):
    o_ref[...] = _dot(a_ref[...], b_ref[...])


def matmul_bf16(a, bm):
    m, k = a.shape
    n = bm.shape[1]
    tm = _pick(m, (512, 256, 128, 64))
    tn = _pick(n, (512, 256, 128))
    return pl.pallas_call(
        _mm_kernel,
        grid=(n // tn, m // tm),
        in_specs=[pl.BlockSpec((tm, k), lambda j, i: (i, 0)),
                  pl.BlockSpec((k, tn), lambda j, i: (0, j))],
        out_specs=pl.BlockSpec((tm, tn), lambda j, i: (i, j)),
        out_shape=jax.ShapeDtypeStruct((m, n), F32),
        compiler_params=_cparams("parallel", "parallel"),
        name="matmul_bf16",
    )(a, bm)


def _dft_fwd_kernel(c_ref, s_ref, z_ref, cp_ref, sp_ref, hr_ref, hi_ref, hbr_ref, hbi_ref, ur_ref, ui_ref):
    c = c_ref[...]
    s = s_ref[...]
    z0 = z_ref[0, 0]
    z1 = z_ref[0, 1]
    e0r, e0i = _dot(c, z0), -_dot(s, z0)
    e1r, e1i = _dot(c, z1), -_dot(s, z1)
    cp = cp_ref[...]
    sp = sp_ref[...]
    t1r = cp * e1r + sp * e1i
    t1i = cp * e1i - sp * e1r
    ar, ai, br, bi = e0r + t1r, e0i + t1i, e0r - t1r, e0i - t1i
    hr, hi, hbr, hbi = hr_ref[...], hi_ref[...], hbr_ref[...], hbi_ref[...]
    pr = ar * hr - ai * hi
    pim = ar * hi + ai * hr
    qr = br * hbr - bi * hbi
    qim = br * hbi + bi * hbr
    wr, wim = pr - qr, pim - qim
    ur_ref[0, 0] = (pr + qr).astype(BF16)
    ui_ref[0, 0] = (pim + qim).astype(BF16)
    ur_ref[0, 1] = (cp * wr - sp * wim).astype(BF16)
    ui_ref[0, 1] = (sp * wr + cp * wim).astype(BF16)


def dft_fwd(cmat, smat, zb, z_col, cpsi, spsi, h4, h_col):
    b, _, m, _ = zb.shape
    tf = _pick(m, (512, 256, 128, 64))
    hspec = pl.BlockSpec((tf, HY_W), lambda i, bi: (i, h_col))
    vspec = pl.BlockSpec((tf, 1), lambda i, bi: (i, 0))
    ospec = pl.BlockSpec((1, 2, tf, HY_W), lambda i, bi: (bi, 0, i, 0))
    return pl.pallas_call(
        _dft_fwd_kernel,
        grid=(m // tf, b),
        in_specs=[pl.BlockSpec((tf, m), lambda i, bi: (i, 0)),
                  pl.BlockSpec((tf, m), lambda i, bi: (i, 0)),
                  pl.BlockSpec((1, 2, m, HY_W), lambda i, bi: (bi, 0, 0, z_col)),
                  vspec, vspec, hspec, hspec, hspec, hspec],
        out_specs=[ospec, ospec],
        out_shape=[jax.ShapeDtypeStruct((b, 2, m, HY_W), BF16)] * 2,
        compiler_params=_cparams("parallel", "parallel"),
        name="hy_dft_fwd",
    )(cmat, smat, zb, cpsi, spsi, *h4)


def _dft_inv_kernel(c_ref, s_ref, ur_ref, ui_ref, x_ref, zp_ref, skip_ref, *o_refs, interleave):
    c = c_ref[...]
    s = s_ref[...]
    tt = c.shape[0]
    lane_tiles = HY_W // HEAD_W
    for par in range(2):
        conv = _dot(c, ur_ref[0, par]) - _dot(s, ui_ref[0, par])
        z = x_ref[0, par] * (conv + skip_ref[...] * zp_ref[0, par])
        if interleave:
            for j in range(lane_tiles):
                o_refs[1][j, pl.ds(par, tt, stride=2), :] = z[:, j * HEAD_W:(j + 1) * HEAD_W]
        else:
            o_refs[0][0, par] = z
            o_refs[1][0, par] = z.astype(BF16)
    if interleave:
        o_refs[0][0] = jnp.concatenate([o_refs[1][j] for j in range(lane_tiles)], axis=-1)


def dft_inv(cmat, smat, ur, ui, x_arr, x_col, zp_arr, zp_col, skip, interleave):
    b, _, m, _ = ur.shape
    tt = _pick(m, (512, 256, 128, 64))
    uspec = pl.BlockSpec((1, 2, m, HY_W), lambda i, bi: (bi, 0, 0, 0))
    pspec = lambda col: pl.BlockSpec((1, 2, tt, HY_W), lambda i, bi: (bi, 0, i, col))
    if interleave:
        out_specs = [pl.BlockSpec((1, 2 * tt, HY_W), lambda i, bi: (bi, i, 0))]
        out_shape = [jax.ShapeDtypeStruct((b, 2 * m, HY_W), F32)]
    else:
        out_specs = [pspec(0), pspec(0)]
        out_shape = [jax.ShapeDtypeStruct((b, 2, m, HY_W), F32), jax.ShapeDtypeStruct((b, 2, m, HY_W), BF16)]
    return pl.pallas_call(
        functools.partial(_dft_inv_kernel, interleave=interleave),
        grid=(m // tt, b),
        in_specs=[pl.BlockSpec((tt, m), lambda i, bi: (i, 0)),
                  pl.BlockSpec((tt, m), lambda i, bi: (i, 0)),
                  uspec, uspec, pspec(x_col), pspec(zp_col),
                  pl.BlockSpec((1, HY_W), lambda i, bi: (0, 0))],
        out_specs=out_specs,
        out_shape=out_shape,
        scratch_shapes=[pltpu.VMEM((HY_W // HEAD_W, 2 * tt, HEAD_W), F32)] if interleave else [],
        compiler_params=_cparams("parallel", "parallel"),
        name="hy_dft_inv",
    )(cmat, smat, ur, ui, x_arr, zp_arr, skip.reshape(1, HY_W))


def dft_tables(l):
    r = _pick(l, (64, 32, 16, 8))
    period = 8 * l
    odd = 2 * jnp.arange(l, dtype=jnp.int32) + 1
    s1 = jnp.arange(l // r, dtype=jnp.int32) * (2 * r)
    s0 = 2 * jnp.arange(r, dtype=jnp.int32) + 1
    ang = lambda ph: ph.astype(F32) * (2.0 * math.pi / period)
    a = ang((odd[:, None] * s1[None, :]) % period)
    bb = ang((odd[:, None] * s0[None, :]) % period)
    ca, sa, cb, sb = jnp.cos(a), jnp.sin(a), jnp.cos(bb), jnp.sin(bb)
    cmat = (ca[:, :, None] * cb[:, None, :] - sa[:, :, None] * sb[:, None, :]).reshape(l, l).astype(BF16)
    smat = (sa[:, :, None] * cb[:, None, :] + ca[:, :, None] * sb[:, None, :]).reshape(l, l).astype(BF16)
    phi = ang(odd)[:, None]
    return cmat, smat, jnp.cos(phi), jnp.sin(phi)


def hyena_filter_taps(length, lp, parity):
    pos = 2.0 * jnp.arange(length // 2, dtype=F32) + parity
    t = pos / (length - 1)
    bands = (HY_EMB - 1) // 2
    fr = jnp.linspace(1e-4, bands - 1, bands, dtype=F32)
    ang = (2.0 * math.pi / length) * pos[:, None] * fr[None, :]
    z = jnp.concatenate([t[:, None], jnp.cos(ang), -jnp.sin(ang)], axis=-1)
    mm = functools.partial(jnp.matmul, precision=HIGHEST)
    hdn = jnp.sin(lp['hy_f_freq'][0] * (mm(z, lp['hy_f_w1']) + lp['hy_f_b1']))
    hdn = jnp.sin(lp['hy_f_freq'][1] * (mm(hdn, lp['hy_f_w2']) + lp['hy_f_b2']))
    return mm(hdn, lp['hy_f_w3']) * jnp.exp(-t[:, None] * jnp.abs(lp['hy_decay']))


def hyena_mixer(p, lp, tables):
    b, l, _ = p.shape
    cmat, smat, cpsi, spsi = tables
    uc, ucb = hy_pre(p, lp['hy_conv_w'], lp['hy_conv_b'])
    w2 = 2 * HY_W
    m = l // 2
    taps = [hyena_filter_taps(l, lp, par).reshape(m, 2, 2, HY_W) for par in range(2)]
    gf0, gf1 = (tp[:, :, 0, :].reshape(m, w2) for tp in taps)
    hb_even, gb1 = (tp[:, :, 1, :].reshape(m, w2) for tp in taps)
    gb0 = jnp.concatenate([hb_even[1:], jnp.zeros((1, w2), F32)], axis=0)
    gr_s = matmul_bf16(cmat, jnp.concatenate([gf0 + gb0, gf1 + gb1], axis=1).astype(BF16))
    gi_s = -matmul_bf16(smat, jnp.concatenate([gf0 - gb0, gf1 - gb1], axis=1).astype(BF16))
    gr = cpsi * gr_s - spsi * gi_s
    gi = spsi * gr_s + cpsi * gi_s
    g0r, g1r, g0i, g1i = gr[:, :w2], gr[:, w2:], gi[:, :w2], gi[:, w2:]
    tr = cpsi * g1r + spsi * g1i
    ti = cpsi * g1i - spsi * g1r
    scale = 1.0 / l
    h4 = ((g0r + tr) * scale, (g0i + ti) * scale,
          (g0r - tr) * scale, (g0i - ti) * scale)
    ur, ui = dft_fwd(cmat, smat, ucb, 0, cpsi, spsi, h4, 0)
    z1, z1b = dft_inv(cmat, smat, ur, ui, uc, 1, uc, 0, lp['hy_skip'][0], interleave=False)
    ur, ui = dft_fwd(cmat, smat, z1b, 0, cpsi, spsi, h4, 1)
    (z2,) = dft_inv(cmat, smat, ur, ui, uc, 2, z1, 0, lp['hy_skip'][1], interleave=True)
    return z2


def _head_rms(y):
    parts = []
    for h in range(HEADS):
        yh = y[:, h * HEAD_W:(h + 1) * HEAD_W]
        parts.append(yh * lax.rsqrt(jnp.mean(yh * yh, axis=-1, keepdims=True) + EPS))
    return jnp.concatenate(parts, axis=-1)


def _outproj_kernel(hy_ref, gla_ref, glag_ref, ml_ref, mlo_ref, ret_ref, retg_ref,
                    gn_ref, w_ref, res_ref, gate_ref, o_ref):
    gn = gn_ref[...]
    y_gla = _head_rms(gla_ref[0]) * gn[0:1, :] * _silu(glag_ref[0].astype(F32))
    y_ml = _head_rms(_sigmoid(mlo_ref[0].astype(F32)) * ml_ref[0]) * gn[1:2, :]
    y_ret = _head_rms(ret_ref[0]) * gn[2:3, :] * _silu(retg_ref[0].astype(F32))
    acc = _dot(hy_ref[0].astype(BF16), w_ref[0:GROUP_W, :])
    acc += _dot(y_gla.astype(BF16), w_ref[GROUP_W:2 * GROUP_W, :])
    acc += _dot(y_ml.astype(BF16), w_ref[2 * GROUP_W:3 * GROUP_W, :])
    acc += _dot(y_ret.astype(BF16), w_ref[3 * GROUP_W:4 * GROUP_W, :])
    o_ref[0] = res_ref[0] + gate_ref[...] * acc


def outproj(hy, o_gla, o_ml, o_ret, p, gains, w_out, res, mod4, row_of_batch, i_gate):
    b, l, d = res.shape
    tm = _pick(l, (512, 256, 128, 64))
    gw = GROUP_W
    ospec = pl.BlockSpec((1, tm, gw), lambda bi, i: (bi, i, 0))
    pspec = lambda off: pl.BlockSpec((1, tm, gw), lambda bi, i: (bi, i, off // gw))
    return pl.pallas_call(
        _outproj_kernel,
        grid=(b, l // tm),
        in_specs=[ospec, ospec, pspec(OFF_GLA_G), ospec, pspec(OFF_ML_O), ospec, pspec(OFF_RET_G),
                  pl.BlockSpec((8, gw), lambda bi, i: (0, 0)),
                  pl.BlockSpec((4 * gw, d), lambda bi, i: (0, 0)),
                  pl.BlockSpec((1, tm, d), lambda bi, i: (bi, i, 0)),
                  pl.BlockSpec((None, None, 1, d), lambda bi, i: (row_of_batch(bi), i_gate, 0, 0))],
        out_specs=pl.BlockSpec((1, tm, d), lambda bi, i: (bi, i, 0)),
        out_shape=jax.ShapeDtypeStruct((b, l, d), F32),
        compiler_params=_cparams("parallel", "parallel"),
        name="outproj",
    )(hy, o_gla, p, o_ml, p, o_ret, p, gains, w_out, res, mod4)


def _moe_pre_kernel(x_ref, g_ref, sc_ref, sh_ref, wr_ref, br_ref, t_ref, lg_ref):
    t = _norm_mod(x_ref[0], g_ref[...], sc_ref[...], sh_ref[...])
    t_ref[0] = _pack_pairs(t)
    lg_ref[0] = _dot_hi(t, wr_ref[...]) + br_ref[...]


def moe_pre(x, g, mod4, row_of_batch, i_sc, i_sh, w_router, b_router):
    b, l, d = x.shape
    tm = _pick(l, (512, 256, 128, 64))
    nr = w_router.shape[1]
    return pl.pallas_call(
        _moe_pre_kernel,
        grid=(b, l // tm),
        in_specs=[pl.BlockSpec((1, tm, d), lambda bi, i: (bi, i, 0)),
                  pl.BlockSpec((1, d), lambda bi, i: (0, 0)),
                  pl.BlockSpec((None, None, 1, d), lambda bi, i: (row_of_batch(bi), i_sc, 0, 0)),
                  pl.BlockSpec((None, None, 1, d), lambda bi, i: (row_of_batch(bi), i_sh, 0, 0)),
                  pl.BlockSpec((d, nr), lambda bi, i: (0, 0)),
                  pl.BlockSpec((1, nr), lambda bi, i: (0, 0))],
        out_specs=[pl.BlockSpec((1, tm, d // 2), lambda bi, i: (bi, i, 0)),
                   pl.BlockSpec((1, tm, nr), lambda bi, i: (bi, i, 0))],
        out_shape=[jax.ShapeDtypeStruct((b, l, d // 2), jnp.uint32), jax.ShapeDtypeStruct((b, l, nr), F32)],
        compiler_params=_cparams("parallel", "parallel"),
        name="moe_pre",
    )(x, g.reshape(1, d), mod4, mod4, w_router, b_router)


ROUTE_E1, ROUTE_E2, ROUTE_R1, ROUTE_R2, ROUTE_W1, ROUTE_W2 = range(6)


def _moe_route_kernel(lg_ref, route_ref, cnt_ref, run_ref, *, tr):
    i = pl.program_id(0)

    @pl.when(i == 0)
    def _():
        run_ref[...] = jnp.zeros_like(run_ref)

    lg = lg_ref[...]
    lane = lax.broadcasted_iota(jnp.int32, lg.shape, 1).astype(F32)
    no_lane = float(HEAD_W)
    neg = -jnp.inf
    first_max = lambda vals, vmax: jnp.min(jnp.where(vals == vmax, lane, no_lane), axis=-1, keepdims=True)
    gl = jnp.where(lane < MOE_GROUPS, lg, neg)
    gmax = jnp.max(gl, axis=-1, keepdims=True)
    pg = 1.0 / jnp.sum(jnp.exp(gl - gmax), axis=-1, keepdims=True)
    lo = MOE_GROUPS + first_max(gl, gmax) * MOE_EPG
    sel = jnp.where(jnp.logical_and(lane >= lo, lane < lo + MOE_EPG), lg, neg)
    v1 = jnp.max(sel, axis=-1, keepdims=True)
    i1 = first_max(sel, v1)
    sel2 = jnp.where(lane == i1, neg, sel)
    v2 = jnp.max(sel2, axis=-1, keepdims=True)
    i2 = first_max(sel2, v2)
    e21 = jnp.exp(v2 - v1)
    w1 = pg / (1.0 + e21)
    w2 = w1 * e21
    e1 = i1 - MOE_GROUPS
    e2 = i2 - MOE_GROUPS
    oh1 = (lane == e1).astype(F32)
    oh2 = (lane == e2).astype(F32)
    both = oh1 + oh2
    row = lax.broadcasted_iota(jnp.int32, (tr, tr), 0)
    col = lax.broadcasted_iota(jnp.int32, (tr, tr), 1)
    earlier = (col < row).astype(BF16)
    before = _dot(earlier, both.astype(BF16)) + run_ref[...]
    r1 = jnp.sum(before * oh1, axis=-1, keepdims=True)
    r2 = jnp.sum(before * oh2, axis=-1, keepdims=True)
    run_ref[...] += jnp.sum(both, axis=0, keepdims=True)
    rec = jnp.zeros_like(lg)
    for k, val in ((ROUTE_E1, e1), (ROUTE_E2, e2), (ROUTE_R1, r1), (ROUTE_R2, r2), (ROUTE_W1, w1), (ROUTE_W2, w2)):
        rec = jnp.where(lane == k, val, rec)
    route_ref[...] = rec

    @pl.when(i == pl.num_programs(0) - 1)
    def _():
        cnt_ref[...] = run_ref[...]


def moe_route(logits):
    t, nl = logits.shape
    tr = _pick(t, (512, 256, 128, 64))
    return pl.pallas_call(
        functools.partial(_moe_route_kernel, tr=tr),
        grid=(t // tr,),
        in_specs=[pl.BlockSpec((tr, nl), lambda i: (i, 0))],
        out_specs=[pl.BlockSpec((tr, nl), lambda i: (i, 0)), pl.BlockSpec((1, nl), lambda i: (0, 0))],
        out_shape=[jax.ShapeDtypeStruct((t, nl), F32), jax.ShapeDtypeStruct((1, nl), F32)],
        scratch_shapes=[pltpu.VMEM((1, nl), F32)],
        compiler_params=_cparams("arbitrary"),
        name="moe_route",
    )(logits)


def _moe_pos_kernel(route_ref, pstart_ref, pos_ref):
    rec = route_ref[...]
    lane = lax.broadcasted_iota(jnp.int32, rec.shape, 1).astype(F32)
    start = lambda e: jnp.sum(jnp.where(lane == e, pstart_ref[...], 0.0), axis=-1, keepdims=True)
    p1 = start(rec[:, ROUTE_E1:ROUTE_E1 + 1]) + rec[:, ROUTE_R1:ROUTE_R1 + 1]
    p2 = start(rec[:, ROUTE_E2:ROUTE_E2 + 1]) + rec[:, ROUTE_R2:ROUTE_R2 + 1]
    pos_ref[...] = jnp.where(lane == 0, p1, jnp.where(lane == 1, p2, 0.0))


def moe_pos(route, pstart_row):
    t, nl = route.shape
    tr = _pick(t, (512, 256, 128, 64))
    return pl.pallas_call(
        _moe_pos_kernel,
        grid=(t // tr,),
        in_specs=[pl.BlockSpec((tr, nl), lambda i: (i, 0)), pl.BlockSpec((1, nl), lambda i: (0, 0))],
        out_specs=pl.BlockSpec((tr, nl), lambda i: (i, 0)),
        out_shape=jax.ShapeDtypeStruct((t, nl), F32),
        compiler_params=_cparams("parallel"),
        name="moe_pos",
    )(route, pstart_row)


ISSUE_UNROLL = 8


def _row_copy_wait(src_rows, dst_rows, sem):
    pltpu.make_async_copy(src_rows, dst_rows, sem).wait()


def _moe_dispatch_kernel(pend_ref, padded_ref, pos_ref, tok_ref, xs_ref, zero_ref, sem, *, tr, tm):
    @pl.when(pl.program_id(0) == 0)
    def _():
        zero_ref[...] = jnp.zeros_like(zero_ref)

        def clear(e, carry):
            @pl.when(padded_ref[e] > 0)
            def _():
                dst = xs_ref.at[pl.ds(pl.multiple_of(pend_ref[e] - tm, tm), tm)]
                cp = pltpu.make_async_copy(zero_ref, dst, sem)
                cp.start()
                cp.wait()
            return carry

        lax.fori_loop(0, N_EXPERTS, clear, 0)

    def issue(r, carry):
        for slot in range(2):
            dst = pos_ref[0, 0, 2 * r + slot]
            pltpu.make_async_copy(tok_ref.at[pl.ds(r, 1)], xs_ref.at[pl.ds(dst, 1)], sem).start()
        return carry

    lax.fori_loop(0, tr, issue, 0, unroll=ISSUE_UNROLL)
    for _ in range(2):
        _row_copy_wait(tok_ref, xs_ref.at[pl.ds(0, tr)], sem)


def moe_dispatch(tok, pos, pend, padded, pmax, tm):
    t, d = tok.shape
    tr = pos.shape[2] // 2
    grid_spec = pltpu.PrefetchScalarGridSpec(
        num_scalar_prefetch=2,
        grid=(t // tr,),
        in_specs=[pl.BlockSpec((1, 1, 2 * tr), lambda i, pe, pa: (i, 0, 0), memory_space=pltpu.SMEM),
                  pl.BlockSpec((tr, d), lambda i, pe, pa: (i, 0))],
        out_specs=pl.BlockSpec(memory_space=pl.ANY),
        scratch_shapes=[pltpu.VMEM((tm, d), tok.dtype), pltpu.SemaphoreType.DMA],
    )
    return pl.pallas_call(
        functools.partial(_moe_dispatch_kernel, tr=tr, tm=tm),
        grid_spec=grid_spec,
        out_shape=jax.ShapeDtypeStruct((pmax, d), tok.dtype),
        compiler_params=_cparams("arbitrary"),
        name="moe_dispatch",
    )(pend, padded, pos, tok)


def _moe_expert_kernel(te_ref, nu_ref, x_ref, w1_ref, w3_ref, w2_ref, o_ref, w1b_ref, w3b_ref, w2b_ref):
    t = pl.program_id(0)

    @pl.when(t < nu_ref[0])
    def _():
        @pl.when(jnp.logical_or(t == 0, te_ref[t] != te_ref[jnp.maximum(t - 1, 0)]))
        def _():
            w1b_ref[...] = w1_ref[0].astype(BF16)
            w3b_ref[...] = w3_ref[0].astype(BF16)
            w2b_ref[...] = w2_ref[0].astype(BF16)

        x = _unpack_pairs(x_ref[...]).astype(BF16)
        act = _silu(_dot(x, w1b_ref[...])) * _dot(x, w3b_ref[...])
        o_ref[...] = _pack_pairs(_dot(act.astype(BF16), w2b_ref[...]))


def moe_experts(x_sorted, w1, w3, w2, tile_expert, n_used, tm):
    pmax, dp = x_sorted.shape
    d = 2 * dp
    ff = w1.shape[2]
    used = lambda t, nu: jnp.minimum(t, nu[0] - 1)
    grid_spec = pltpu.PrefetchScalarGridSpec(
        num_scalar_prefetch=2,
        grid=(pmax // tm,),
        in_specs=[pl.BlockSpec((tm, dp), lambda t, te, nu: (used(t, nu), 0)),
                  pl.BlockSpec((1, d, ff), lambda t, te, nu: (te[t], 0, 0)),
                  pl.BlockSpec((1, d, ff), lambda t, te, nu: (te[t], 0, 0)),
                  pl.BlockSpec((1, ff, d), lambda t, te, nu: (te[t], 0, 0))],
        out_specs=pl.BlockSpec((tm, dp), lambda t, te, nu: (used(t, nu), 0)),
        scratch_shapes=[pltpu.VMEM((d, ff), BF16), pltpu.VMEM((d, ff), BF16), pltpu.VMEM((ff, d), BF16)],
    )
    return pl.pallas_call(
        _moe_expert_kernel,
        grid_spec=grid_spec,
        out_shape=jax.ShapeDtypeStruct((pmax, dp), jnp.uint32),
        compiler_params=_cparams("arbitrary"),
        name="moe_experts",
    )(tile_expert, n_used, x_sorted, w1, w3, w2)


def _moe_combine_kernel(p1_ref, p2_ref, p1n_ref, p2n_ref, ys_ref, res_ref, route_ref, gate_ref, fin_ref, o_ref,
                        y1_ref, y2_ref, sem, *, tc, final_norm):
    i = pl.program_id(0)
    slot = i % 2

    def issue(pa_ref, pb_ref, s):
        def body(r, carry):
            pltpu.make_async_copy(ys_ref.at[pl.ds(pa_ref[0, 0, r], 1)], y1_ref.at[s, pl.ds(r, 1)], sem.at[s]).start()
            pltpu.make_async_copy(ys_ref.at[pl.ds(pb_ref[0, 0, r], 1)], y2_ref.at[s, pl.ds(r, 1)], sem.at[s]).start()
            return carry
        lax.fori_loop(0, tc, body, 0, unroll=ISSUE_UNROLL)

    @pl.when(i == 0)
    def _():
        issue(p1_ref, p2_ref, 0)

    @pl.when(i + 1 < pl.num_programs(0))
    def _():
        issue(p1n_ref, p2n_ref, 1 - slot)

    _row_copy_wait(ys_ref.at[pl.ds(0, tc)], y1_ref.at[slot], sem.at[slot])
    _row_copy_wait(ys_ref.at[pl.ds(0, tc)], y2_ref.at[slot], sem.at[slot])
    rec = route_ref[...]
    w1 = rec[:, ROUTE_W1:ROUTE_W1 + 1]
    w2 = rec[:, ROUTE_W2:ROUTE_W2 + 1]
    y = w1 * _unpack_pairs(y1_ref[slot]) + w2 * _unpack_pairs(y2_ref[slot])
    out = res_ref[...] + gate_ref[...] * y
    if final_norm:
        out = out * lax.rsqrt(jnp.mean(out * out, axis=-1, keepdims=True) + EPS) * fin_ref[...]
    o_ref[...] = out


def moe_combine(res, y_sorted, pos1, pos2, route, mod4, row_of_tile, i_gate, final_g=None):
    t, d = res.shape
    fin = jnp.ones((1, d), F32) if final_g is None else final_g.reshape(1, d)
    n, _, tc = pos1.shape
    cur = lambda i: (i, 0, 0)
    nxt = lambda i: (jnp.minimum(i + 1, n - 1), 0, 0)
    smem = lambda im: pl.BlockSpec((1, 1, tc), im, memory_space=pltpu.SMEM)
    return pl.pallas_call(
        functools.partial(_moe_combine_kernel, tc=tc, final_norm=final_g is not None),
        grid=(n,),
        in_specs=[smem(cur), smem(cur), smem(nxt), smem(nxt),
                  pl.BlockSpec(memory_space=pl.ANY),
                  pl.BlockSpec((tc, d), lambda i: (i, 0)),
                  pl.BlockSpec((tc, route.shape[1]), lambda i: (i, 0)),
                  pl.BlockSpec((None, None, 1, d), lambda i: (row_of_tile(i), i_gate, 0, 0)),
                  pl.BlockSpec((1, d), lambda i: (0, 0))],
        out_specs=pl.BlockSpec((tc, d), lambda i: (i, 0)),
        out_shape=jax.ShapeDtypeStruct((t, d), F32),
        scratch_shapes=[pltpu.VMEM((2, tc, d // 2), y_sorted.dtype), pltpu.VMEM((2, tc, d // 2), y_sorted.dtype),
                        pltpu.SemaphoreType.DMA((2,))],
        compiler_params=_cparams("arbitrary"),
        name="moe_combine",
    )(pos1, pos2, pos1, pos2, y_sorted, res, route, mod4, fin)


def hier_moe(x, lp, mod4, row_of_batch, final_g=None):
    b, l, d = x.shape
    t = b * l
    tm = 512 if t >= 4096 else 64
    tok, logits = moe_pre(x, lp['norm2_g'], mod4, row_of_batch, 4, 3, lp['w_router'], lp['b_router'])
    tok = tok.reshape(t, d // 2)
    route, cnt = moe_route(logits.reshape(t, -1))
    counts = cnt[0, :N_EXPERTS].astype(jnp.int32)
    padded = ((counts + tm - 1) // tm) * tm
    pend = jnp.cumsum(padded)
    pstart = pend - padded
    pmax = 2 * t + N_EXPERTS * tm
    tile_row = jnp.arange(pmax // tm, dtype=jnp.int32) * tm
    tile_expert = jnp.minimum(jnp.sum(tile_row[:, None] >= pend[None, :], axis=1), N_EXPERTS - 1).astype(jnp.int32)
    n_used = (pend[-1] // tm).astype(jnp.int32).reshape(1)
    pstart_row = jnp.zeros((1, route.shape[1]), F32).at[0, :N_EXPERTS].set(pstart.astype(F32))
    pos12 = moe_pos(route, pstart_row)[:, :2].astype(jnp.int32)
    pos1, pos2 = pos12[:, 0], pos12[:, 1]
    tr = _pick(t, (512, 256, 128, 64))
    tc = _pick(l, (256, 128, 64))
    pos = pos12.reshape(t // tr, 1, 2 * tr)
    x_sorted = moe_dispatch(tok, pos, pend, padded, pmax, tm)
    y_sorted = moe_experts(x_sorted, lp['moe_w1e'], lp['moe_w3e'], lp['moe_w2e'],
                           tile_expert + lp['expert_base'], n_used, tm)
    tiles_per_batch = l // tc
    out = moe_combine(x.reshape(t, d), y_sorted, pos1.reshape(t // tc, 1, tc), pos2.reshape(t // tc, 1, tc),
                      route, mod4, lambda i: row_of_batch(i // tiles_per_batch), 5, final_g=final_g)
    return out.reshape(b, l, d)


def _pad_heads(w, dk):
    d = w.shape[0]
    return jnp.pad(w.reshape(d, HEADS, dk), ((0, 0), (0, 0), (0, HEAD_W - dk))).reshape(d, HEADS * HEAD_W)


def prep_w_in(w_in):
    o = 0
    take = lambda n: (w_in[:, o:o + n], o + n)
    hy, o = take(3 * HY_W)
    gq, o = take(HEADS * GLA_DK)
    gk, o = take(HEADS * GLA_DK)
    gv, o = take(GROUP_W)
    gg, o = take(GROUP_W)
    gr, o = take(2 * GLA_RANK)
    ml, o = take(4 * GROUP_W)
    mg, o = take(16)
    ret, o = take(4 * GROUP_W)
    gk_p = _pad_heads(gk, GLA_DK)
    gk_p = gk_p.at[:, SMALL_R:SMALL_R + 2 * GLA_RANK].set(gr).at[:, SMALL_MLG:SMALL_MLG + 16].set(mg)
    return jnp.concatenate([hy, _pad_heads(gq, GLA_DK), gk_p, gv, gg, ml, ret], axis=1).astype(BF16)


def prep_layer(l, prm):
    lp = {k: v[l] for k, v in prm.items()}
    lp['w_in_p'] = prep_w_in(lp['w_in'])
    lp['w_out_b'] = lp['w_out'].astype(BF16)
    gates = []
    for d in range(2):
        w = jnp.zeros((HEAD_W, HEADS * HEAD_W), F32)
        w = w.at[SMALL_R + d * GLA_RANK:SMALL_R + (d + 1) * GLA_RANK].set(_pad_heads(lp['gla_wa2'][d], GLA_DK))
        w_hi = w.astype(BF16)
        w_lo = (w - w_hi.astype(F32)).astype(BF16)
        gates.append((w_hi, w_lo, _pad_heads(lp['gla_ba'][d][None, :], GLA_DK)))
    lp['gla_gate'] = gates
    bias = jnp.zeros((1, HEAD_W), F32).at[0, SMALL_MLG:SMALL_MLG + 16].set(lp['ml_gate_b'].reshape(-1))
    lp['ml_bias'] = bias
    lp['ret_dec'] = [jnp.repeat(lp['ret_decay'][d], HEAD_W)[None, :] for d in range(2)]
    gains = jnp.zeros((8, GROUP_W), F32)
    lp['gains'] = gains.at[0].set(lp['gla_norm_g']).at[1].set(lp['ml_norm_g']).at[2].set(lp['ret_norm_g'])
    d_model = lp['moe_wg'].shape[0]
    wr = jnp.zeros((d_model, HEAD_W), F32)
    lp['w_router'] = wr.at[:, :MOE_GROUPS].set(lp['moe_wg']).at[:, MOE_GROUPS:MOE_GROUPS + N_EXPERTS].set(lp['moe_we'])
    br = jnp.zeros((1, HEAD_W), F32)
    lp['b_router'] = br.at[0, :MOE_GROUPS].set(lp['moe_bg']).at[0, MOE_GROUPS:MOE_GROUPS + N_EXPERTS].set(lp['moe_be'])
    lp['moe_w1e'] = prm['moe_w1'].reshape(-1, d_model, MOE_FF)
    lp['moe_w3e'] = prm['moe_w3'].reshape(-1, d_model, MOE_FF)
    lp['moe_w2e'] = prm['moe_w2'].reshape(-1, MOE_FF, d_model)
    lp['expert_base'] = l * N_EXPERTS
    return lp


def rotary_tables(length):
    rows = length // GRID_W
    row = jnp.repeat(jnp.arange(rows, dtype=F32), GRID_W)
    col = jnp.tile(jnp.arange(GRID_W, dtype=F32), rows)
    nf = HEAD_W // 4
    inv = ROPE_BASE ** (-jnp.arange(nf, dtype=F32) / nf)
    ang = jnp.concatenate([row[:, None] * inv, col[:, None] * inv], axis=-1)
    cos, sin = jnp.cos(ang), jnp.sin(ang)
    return jnp.concatenate([cos, cos], axis=-1), jnp.concatenate([-sin, sin], axis=-1)


def bidir_scans(pg_ctx, pg_lat, lp, rot):
    (p_ctx, g_ctx), (p_lat, g_lat) = pg_ctx, pg_lat
    b = p_lat.shape[0]
    zeros_s = jnp.zeros((b, HEADS, HEAD_W, HEAD_W), F32)
    zeros_v = jnp.zeros((b, HEADS, 1, HEAD_W), F32)
    out = {}
    for name, mode, offs, scale in (("gla", "gla", (OFF_GLA_Q, OFF_GLA_K, OFF_GLA_V), GLA_DK ** -0.5),
                                    ("ret", "ret", (OFF_RET_Q, OFF_RET_K, OFF_RET_V), HEAD_W ** -0.5)):
        o_c, o_l = None, None
        for d in range(2):
            gate = lp['gla_gate'][d] if mode == "gla" else lp['ret_dec'][d]
            o_c, st = gla_scan(p_ctx, g_ctx, *offs, zeros_s, reverse=d == 1, mode=mode, gate=gate,
                               o_prev=o_c, q_scale=scale)
            o_l, _ = gla_scan(p_lat, g_lat, *offs, st, reverse=d == 1, mode=mode, gate=gate,
                              rot=rot if mode == "ret" else None, o_prev=o_l, q_scale=scale)
        out[name] = (o_c, o_l)
    o_c, o_l = None, None
    for d in range(2):
        o_c, st = mlstm_scan(p_ctx, g_ctx, (zeros_s, zeros_v, zeros_v), lp['ml_bias'], reverse=d == 1, o_prev=o_c)
        o_l, _ = mlstm_scan(p_lat, g_lat, st, lp['ml_bias'], reverse=d == 1, o_prev=o_l)
    out["ml"] = (o_c, o_l)
    return out


def kernel(x, c, ctx, c_ctx, ada_w, ada_b, norm1_g, norm2_g, w_in, hy_conv_w, hy_conv_b, hy_f_w1, hy_f_b1, hy_f_w2, hy_f_b2, hy_f_freq, hy_f_w3, hy_decay, hy_skip, gla_wa2, gla_ba, gla_norm_g, ml_gate_b, ml_norm_g, ret_decay, ret_norm_g, w_out, moe_wg, moe_bg, moe_we, moe_be, moe_w1, moe_w3, moe_w2, final_g):
    prm = dict(norm1_g=norm1_g, norm2_g=norm2_g, w_in=w_in, hy_conv_w=hy_conv_w, hy_conv_b=hy_conv_b,
               hy_f_w1=hy_f_w1, hy_f_b1=hy_f_b1, hy_f_w2=hy_f_w2, hy_f_b2=hy_f_b2, hy_f_freq=hy_f_freq,
               hy_f_w3=hy_f_w3, hy_decay=hy_decay, hy_skip=hy_skip, gla_wa2=gla_wa2, gla_ba=gla_ba,
               gla_norm_g=gla_norm_g, ml_gate_b=ml_gate_b, ml_norm_g=ml_norm_g, ret_decay=ret_decay,
               ret_norm_g=ret_norm_g, w_out=w_out, moe_wg=moe_wg, moe_bg=moe_bg, moe_we=moe_we,
               moe_be=moe_be, moe_w1=moe_w1, moe_w3=moe_w3, moe_w2=moe_w2)
    depth = ada_w.shape[0]
    b, seq, d = x.shape
    lc = ctx.shape[1]
    lat = x.astype(F32)
    cx = ctx.astype(F32)
    cvec = jnp.zeros((8, d), F32).at[:b].set(c.astype(F32)).at[b].set(c_ctx.astype(F32))
    mod_all = adaln(cvec, ada_w, ada_b)
    lat_row = lambda bi: bi
    ctx_row = lambda bi: b
    rot = rotary_tables(seq)
    tab_lat = dft_tables(seq // 2)
    tab_ctx = dft_tables(lc // 2)
    for l in range(depth):
        with_ctx = l < depth - 1
        lp = prep_layer(l, prm)
        mod4 = mod_all[l].reshape(8, 6, 1, d)
        p_lat, g_lat = inproj(lat, lp['norm1_g'], mod4, lat_row, 1, 0, lp['w_in_p'])
        p_ctx, g_ctx = inproj(cx, lp['norm1_g'], mod4, ctx_row, 1, 0, lp['w_in_p'])
        scans = bidir_scans((p_ctx, g_ctx), (p_lat, g_lat), lp, rot)
        hy_lat = hyena_mixer(p_lat, lp, tab_lat)
        lat = outproj(hy_lat, scans["gla"][1], scans["ml"][1], scans["ret"][1], p_lat, lp['gains'],
                      lp['w_out_b'], lat, mod4, lat_row, 2)
        lat = hier_moe(lat, lp, mod4, lat_row, final_g=None if with_ctx else final_g)
        if with_ctx:
            hy_ctx = hyena_mixer(p_ctx, lp, tab_ctx)
            cx = outproj(hy_ctx, scans["gla"][0], scans["ml"][0], scans["ret"][0], p_ctx, lp['gains'],
                         lp['w_out_b'], cx, mod4, ctx_row, 2)
            cx = hier_moe(cx, lp, mod4, ctx_row)
    return lat.astype(x.dtype)
```
